```python
import functools
import jax
import jax.numpy as jnp
from jax import lax
import numpy as np

D_MODEL = 1024
BATCH = 16
SEQ = 2048
DEPTH = 2

GRID_W = 64
CTX_LEN = 256
N_BRANCH = 4
BR_W = D_MODEL // 2
RWKV_HS = 64
RWKV_H = BR_W // RWKV_HS
RWKV_W_LORA = 64
RWKV_A_LORA = 64
RWKV_G_LORA = 128
RWKV_GN_EPS = 64e-5
L2_EPS = 1e-12
CONV_K = 31
LRU_BLOCKS = 8
LRU_BS = BR_W // LRU_BLOCKS
LRU_CONV = 4
LRU_C = 8.0
RET_H = 4
RET_DK = 64
RET_DV = BR_W // RET_H
RET_CHUNK = 128
ROPE_BASE = 10000.0
N_EXPERTS = 32
N_GROUPS = 4
EXPERTS_PER_GROUP = N_EXPERTS // N_GROUPS
TOP_K = 2
D_EXPERT = D_MODEL // 2
MOE_BLOCK = 128
RMS_EPS = 1e-6
LN_EPS = 1e-5
F32 = jnp.float32

RWKV_COLS = (('r', BR_W), ('k', BR_W), ('v', BR_W), ('wlo0', RWKV_W_LORA), ('wlo1', RWKV_W_LORA),
             ('alo0', RWKV_A_LORA), ('alo1', RWKV_A_LORA), ('glo', RWKV_G_LORA))
REST_COLS = (('conv_val', BR_W), ('conv_gate', BR_W), ('lru_x', BR_W), ('lru_gate', BR_W),
             ('ret_q', RET_H * RET_DK), ('ret_k', RET_H * RET_DK), ('ret_v', BR_W), ('ret_g', BR_W),
             ('merge', N_BRANCH * D_MODEL))
RWKV_WIDTH = 3 * BR_W + 2 * RWKV_W_LORA + 2 * RWKV_A_LORA + RWKV_G_LORA
IN_WIDTH = RWKV_WIDTH + 4 * BR_W + 2 * RET_H * RET_DK + 2 * BR_W + N_BRANCH * D_MODEL

kernel_name = 'hybrid_flow_backbone_block'


def _rmsnorm(x, g):
    xf = x.astype(F32)
    y = xf * lax.rsqrt(jnp.mean(xf * xf, axis=-1, keepdims=True) + RMS_EPS)
    return (y * g.astype(F32)).astype(x.dtype)


def _layernorm(x, g, b, eps):
    xf = x.astype(F32)
    xc = xf - jnp.mean(xf, axis=-1, keepdims=True)
    return xc * lax.rsqrt(jnp.mean(xc * xc, axis=-1, keepdims=True) + eps) * g.astype(F32) + b.astype(F32)


def _head_norm(y, g, b, eps):
    yf = y.astype(F32)
    yc = yf - jnp.mean(yf, axis=-1, keepdims=True)
    yn = yc * lax.rsqrt(jnp.mean(yc * yc, axis=-1, keepdims=True) + eps)
    return yn.reshape(*y.shape[:2], -1) * g.astype(F32) + b.astype(F32)


def _heads(z, n_heads, head_dim):
    return z.reshape(*z.shape[:2], n_heads, head_dim)


def _split(p, layout):
    out, off = {}, 0
    for name, width in layout:
        out[name] = p[..., off:off + width]
        off += width
    return out


def _shift_prev(z):
    return jnp.pad(z, ((0, 0), (1, 0), (0, 0)))[:, :-1]


def _shift_next(z):
    return jnp.pad(z, ((0, 0), (0, 1), (0, 0)))[:, 1:]


def _depthwise_conv(z, w, b, pad_lo, pad_hi):
    ch = z.shape[-1]
    y = lax.conv_general_dilated(z, w.astype(z.dtype)[:, None, :], (1,), [(pad_lo, pad_hi)],
                                 dimension_numbers=('NWC', 'WIO', 'NWC'), feature_group_count=ch)
    return y + b.astype(y.dtype)


def _modulate(h, shift, scale):
    return h * (1 + scale) + shift


def _project(h, lp):
    p = h @ lp['w_in']
    za = p[..., :RWKV_WIDTH]
    za = za + lp['rwkv_mu_prev'] * (_shift_prev(za) - za) + lp['rwkv_mu_next'] * (_shift_next(za) - za)
    return _split(za, RWKV_COLS), _split(p[..., RWKV_WIDTH:], REST_COLS)


def _two_way(scan_f, scan_b, ctx_f, ctx_b, lat_f, lat_b, s_zero, need_ctx):
    flip = lambda tup: tuple(jnp.flip(t, 1) for t in tup)
    yc_f, sc_f = scan_f(ctx_f, s_zero)
    yc_b, sc_b = scan_b(flip(ctx_b), s_zero)
    yl_f, _ = scan_f(lat_f, sc_f)
    yl_b, _ = scan_b(flip(lat_b), sc_b)
    y_lat = yl_f + jnp.flip(yl_b, 1)
    y_ctx = yc_f + jnp.flip(yc_b, 1) if need_ctx else None
    return y_ctx, y_lat


def _rwkv_scan_inputs(ca, lp):
    r = _heads(ca['r'].astype(F32), RWKV_H, RWKV_HS)
    k = ca['k'].astype(F32)
    v = _heads(ca['v'].astype(F32), RWKV_H, RWKV_HS)
    kk = _heads(k * lp['rwkv_k_k'], RWKV_H, RWKV_HS)
    kk = kk * lax.rsqrt(jnp.sum(kk * kk, axis=-1, keepdims=True) + L2_EPS)
    dirs = []
    for d in range(2):
        w_raw = lp['rwkv_w0'][d] + jnp.tanh(ca['wlo%d' % d].astype(F32)) @ lp['rwkv_w2'][d]
        decay = jnp.exp(-jnp.exp(-jax.nn.softplus(-w_raw) - 0.5))
        a = jax.nn.sigmoid(lp['rwkv_a0'][d] + ca['alo%d' % d].astype(F32) @ lp['rwkv_a2'][d])
        k_d = k * (1.0 + (a - 1.0) * lp['rwkv_k_a'])
        dirs.append((r, _heads(decay, RWKV_H, RWKV_HS), _heads(k_d, RWKV_H, RWKV_HS), v, kk,
                     _heads(a, RWKV_H, RWKV_HS)))
    return dirs


def _rwkv7_scan(inputs, s0):
    def step(S, xs):
        r_t, w_t, k_t, v_t, kk_t, a_t = xs
        sa = jnp.einsum('bhvk,bhk->bhv', S, kk_t)
        S = (S * w_t[:, :, None, :] - sa[..., None] * (kk_t * a_t)[:, :, None, :]
             + v_t[..., None] * k_t[:, :, None, :])
        return S, jnp.einsum('bhvk,bhk->bhv', S, r_t)
    s_fin, y = lax.scan(step, s0, tuple(jnp.swapaxes(t, 0, 1) for t in inputs))
    return jnp.swapaxes(y, 0, 1), s_fin


def _rwkv_out(y, ca, lp):
    r = _heads(ca['r'].astype(F32), RWKV_H, RWKV_HS)
    k = _heads(ca['k'].astype(F32), RWKV_H, RWKV_HS)
    v = _heads(ca['v'].astype(F32), RWKV_H, RWKV_HS)
    g = jax.nn.sigmoid(ca['glo'].astype(F32)) @ lp['rwkv_g2']
    bonus = jnp.sum(r * k * lp['rwkv_r_k'].reshape(RWKV_H, RWKV_HS), axis=-1, keepdims=True) * v
    yn = _head_norm(y, lp['rwkv_gn_g'], lp['rwkv_gn_b'], RWKV_GN_EPS) + bonus.reshape(*bonus.shape[:2], BR_W)
    return (yn * g) @ lp['rwkv_proj']


def _conformer_conv(cr, lp):
    u = cr['conv_val'] * jax.nn.sigmoid(cr['conv_gate'])
    u = _depthwise_conv(u, lp['conv_w'], lp['conv_b'], CONV_K // 2, CONV_K // 2)
    u = jax.nn.silu(_layernorm(u, lp['conv_ln_g'], lp['conv_ln_b'], LN_EPS))
    return u @ lp['conv_proj']


def _lru_scan_inputs(cr, lp):
    dirs = []
    for d in range(2):
        pad = (LRU_CONV - 1, 0) if d == 0 else (0, LRU_CONV - 1)
        xc = _depthwise_conv(cr['lru_x'], lp['lru_conv_w'][d], lp['lru_conv_b'][d], *pad).astype(F32)
        xb = _heads(xc, LRU_BLOCKS, LRU_BS)
        gate_r = jax.nn.sigmoid(jnp.einsum('btgi,gij->btgj', xb, lp['lru_wr'][d].astype(F32)).reshape(xc.shape)
                                + lp['lru_br'][d])
        gate_i = jax.nn.sigmoid(jnp.einsum('btgi,gij->btgj', xb, lp['lru_wi'][d].astype(F32)).reshape(xc.shape)
                                + lp['lru_bi'][d])
        log_a = -LRU_C * gate_r * jax.nn.softplus(-lp['lru_lambda'][d].astype(F32))
        dirs.append((jnp.exp(log_a), jnp.sqrt(-jnp.expm1(2.0 * log_a)) * (gate_i * xc)))
    return dirs


def _linear_scan(inputs, h0):
    a, b = inputs
    def combine(lhs, rhs):
        return lhs[0] * rhs[0], rhs[0] * lhs[1] + rhs[1]
    a_cum, h = lax.associative_scan(combine, (a, b), axis=1)
    h = h + a_cum * h0[:, None, :]
    return h, h[:, -1]


def _lru_out(h, cr, lp):
    return (h * jax.nn.gelu(cr['lru_gate'].astype(F32))) @ lp['lru_proj']


def _rotary_2d(z, rows, cols):
    n_freq = RET_DK // 4
    inv = ROPE_BASE ** (-jnp.arange(n_freq, dtype=F32) / n_freq)
    ang = jnp.concatenate([rows[:, None] * inv, cols[:, None] * inv], axis=-1)[None, :, None, :]
    cos, sin = jnp.cos(ang), jnp.sin(ang)
    z1, z2 = z[..., :RET_DK // 2], z[..., RET_DK // 2:]
    return jnp.concatenate([z1 * cos - z2 * sin, z1 * sin + z2 * cos], axis=-1)


def _ret_scan_inputs(cr, rows, cols):
    q = _heads(cr['ret_q'].astype(F32), RET_H, RET_DK)
    k = _heads(cr['ret_k'].astype(F32), RET_H, RET_DK) * (RET_DK ** -0.5)
    v = _heads(cr['ret_v'].astype(F32), RET_H, RET_DV)
    if rows is not None:
        q, k = _rotary_2d(q, rows, cols), _rotary_2d(k, rows, cols)
    return (q, k, v)


def _retention_chunkwise(inputs, s0, include_diag):
    q, k, v = inputs
    bsz, n_tok = q.shape[:2]
    n_chunk = n_tok // RET_CHUNK
    log_g = jnp.log1p(-jnp.exp2(-5.0 - jnp.arange(RET_H, dtype=F32)))
    idx = jnp.arange(RET_CHUNK, dtype=F32)
    diff = idx[:, None] - idx[None, :]
    keep = diff >= 0 if include_diag else diff > 0
    dmat = jnp.where(keep[None], jnp.exp(log_g[:, None, None] * jnp.maximum(diff, 0.0)[None]), 0.0)
    xi = jnp.exp(log_g[None, :] * (idx[:, None] + 1.0))
    zeta = jnp.exp(log_g[None, :] * (RET_CHUNK - 1.0 - idx)[:, None])
    g_chunk = jnp.exp(log_g * RET_CHUNK)
    chunks = lambda t: jnp.moveaxis(t.reshape(bsz, n_chunk, RET_CHUNK, *t.shape[2:]), 1, 0)

    def step(S, xs):
        qc, kc, vc = xs
        att = jnp.einsum('bihd,bjhd->bhij', qc, kc) * dmat
        o = (jnp.einsum('bhij,bjhv->bihv', att, vc)
             + jnp.einsum('bihd,bhdv->bihv', qc, S) * xi[None, :, :, None])
        S = S * g_chunk[None, :, None, None] + jnp.einsum('bjhd,bjhv->bhdv', kc * zeta[None, :, :, None], vc)
        return S, o
    s_fin, o = lax.scan(step, s0, (chunks(q), chunks(k), chunks(v)))
    return jnp.moveaxis(o, 0, 1).reshape(bsz, n_tok, RET_H, RET_DV), s_fin


def _ret_out(o, cr, lp):
    yn = _head_norm(o, lp['ret_gn_g'], lp['ret_gn_b'], LN_EPS)
    return (jax.nn.silu(cr['ret_g'].astype(F32)) * yn) @ lp['ret_proj']


def _merge(cr, branches, lp):
    gates = jax.nn.sigmoid(cr['merge'].astype(F32) + lp['b_merge'])
    gates = gates.reshape(*gates.shape[:2], N_BRANCH, D_MODEL)
    mixed = sum(gates[:, :, i] * br for i, br in enumerate(branches))
    return mixed @ lp['w_out']


def _mixer(hc, hl, lp, rows, cols, need_ctx):
    bsz = hl.shape[0]
    ca_c, cr_c = _project(hc, lp)
    ca_l, cr_l = _project(hl, lp)
    rc, rl = _rwkv_scan_inputs(ca_c, lp), _rwkv_scan_inputs(ca_l, lp)
    ya_c, ya_l = _two_way(_rwkv7_scan, _rwkv7_scan, rc[0], rc[1], rl[0], rl[1],
                          jnp.zeros((bsz, RWKV_H, RWKV_HS, RWKV_HS), F32), need_ctx)
    lc, ll = _lru_scan_inputs(cr_c, lp), _lru_scan_inputs(cr_l, lp)
    yl_c, yl_l = _two_way(_linear_scan, _linear_scan, lc[0], lc[1], ll[0], ll[1],
                          jnp.zeros((bsz, BR_W), F32), need_ctx)
    ret_f = functools.partial(_retention_chunkwise, include_diag=True)
    ret_b = functools.partial(_retention_chunkwise, include_diag=False)
    qc = _ret_scan_inputs(cr_c, None, None)
    ql = _ret_scan_inputs(cr_l, rows, cols)
    yr_c, yr_l = _two_way(ret_f, ret_b, qc, qc, ql, ql,
                          jnp.zeros((bsz, RET_H, RET_DK, RET_DV), F32), need_ctx)
    out_l = _merge(cr_l, [_rwkv_out(ya_l, ca_l, lp), _conformer_conv(cr_l, lp),
                          _lru_out(yl_l, cr_l, lp), _ret_out(yr_l, cr_l, lp)], lp)
    out_c = None
    if need_ctx:
        out_c = _merge(cr_c, [_rwkv_out(ya_c, ca_c, lp), _conformer_conv(cr_c, lp),
                              _lru_out(yl_c, cr_c, lp), _ret_out(yr_c, cr_c, lp)], lp)
    return out_c, out_l


def _grouped_experts(tokens, e_idx, wts, w1, w3, w2):
    n_tok = tokens.shape[0]
    n_asg = n_tok * TOP_K
    flat_e = e_idx.reshape(-1)
    flat_t = jnp.repeat(jnp.arange(n_tok, dtype=jnp.int32), TOP_K)
    order = jnp.argsort(flat_e)
    se, st, sw = flat_e[order], flat_t[order], wts.reshape(-1)[order]
    counts = jnp.bincount(flat_e, length=N_EXPERTS)
    starts = jnp.cumsum(counts) - counts
    padded = (counts + MOE_BLOCK - 1) // MOE_BLOCK * MOE_BLOCK
    pad_end = jnp.cumsum(padded)
    pad_start = pad_end - padded
    dest = pad_start[se] + jnp.arange(n_asg, dtype=jnp.int32) - starts[se]
    n_blocks = -(-n_asg // MOE_BLOCK) + N_EXPERTS
    slot_tok = jnp.full((n_blocks * MOE_BLOCK,), n_tok, jnp.int32).at[dest].set(st)
    block_e = jnp.minimum(jnp.searchsorted(pad_end, jnp.arange(n_blocks) * MOE_BLOCK, side='right'),
                          N_EXPERTS - 1)
    x_pad = jnp.concatenate([tokens, jnp.zeros((1, tokens.shape[1]), tokens.dtype)], axis=0)
    xb = x_pad[slot_tok].reshape(n_blocks, MOE_BLOCK, -1)

    def expert_block(args):
        xblk, e = args
        return (jax.nn.silu(xblk @ w1[e]) * (xblk @ w3[e])) @ w2[e]
    yb = lax.map(expert_block, (xb, block_e)).reshape(n_blocks * MOE_BLOCK, -1)
    return jax.ops.segment_sum(yb[dest] * sw[:, None], st, num_segments=n_tok)


def _moe(tokens, w_router, b_router, w1, w3, w2):
    aff = jax.nn.sigmoid(tokens.astype(F32) @ w_router.astype(F32))
    sel = (aff + b_router.astype(F32)).reshape(-1, N_GROUPS, EXPERTS_PER_GROUP)
    grp_score = jnp.sum(lax.top_k(sel, TOP_K)[0], axis=-1)
    grp = jnp.argmax(grp_score, axis=-1)
    grp_mask = jnp.arange(N_GROUPS)[None, :] == grp[:, None]
    masked = jnp.where(grp_mask[:, :, None], sel, -jnp.inf).reshape(-1, N_EXPERTS)
    _, e_idx = lax.top_k(masked, TOP_K)
    wts = jnp.take_along_axis(aff, e_idx, axis=1)
    wts = wts / jnp.sum(wts, axis=-1, keepdims=True)
    return _grouped_experts(tokens, e_idx, wts, w1, w3, w2)


def setup_inputs(seed: int = 0) -> dict:
    key = jax.random.key(seed)
    keys = jax.random.split(key, 80)
    count = [0]

    def nk():
        count[0] += 1
        return keys[count[0] - 1]

    def nrm(shape, scale):
        return scale * jax.random.normal(nk(), shape, F32)

    def uni(shape, lo, hi):
        return jax.random.uniform(nk(), shape, F32, lo, hi)

    L, D = DEPTH, D_MODEL
    s_lam = uni((L, 2, BR_W), 0.9, 0.999) ** (1.0 / LRU_C)
    return {
        'x': nrm((BATCH, SEQ, D), 1.0),
        'c': nrm((BATCH, D), 1.0),
        'ctx': nrm((BATCH, CTX_LEN, D), 1.0),
        'c_ctx': nrm((D,), 1.0),
        'w_ada': nrm((L, D, 6 * D), 0.5 * D ** -0.5),
        'b_ada': nrm((L, 6 * D), 0.01),
        'norm1_g': 1.0 + nrm((L, D), 0.05),
        'norm2_g': 1.0 + nrm((L, D), 0.05),
        'w_in': nrm((L, D, IN_WIDTH), D ** -0.5),
        'b_merge': nrm((L, N_BRANCH * D), 0.01),
        'rwkv_mu_prev': uni((L, RWKV_WIDTH), 0.0, 0.5),
        'rwkv_mu_next': uni((L, RWKV_WIDTH), 0.0, 0.5),
        'rwkv_w0': uni((L, 2, BR_W), -6.5, -1.5),
        'rwkv_w2': nrm((L, 2, RWKV_W_LORA, BR_W), 0.1 * RWKV_W_LORA ** -0.5),
        'rwkv_a0': nrm((L, 2, BR_W), 0.1),
        'rwkv_a2': nrm((L, 2, RWKV_A_LORA, BR_W), 0.1 * RWKV_A_LORA ** -0.5),
        'rwkv_g2': nrm((L, RWKV_G_LORA, BR_W), RWKV_G_LORA ** -0.5),
        'rwkv_k_k': 0.85 + nrm((L, BR_W), 0.05),
        'rwkv_k_a': 1.0 + nrm((L, BR_W), 0.05),
        'rwkv_r_k': nrm((L, BR_W), 0.1),
        'rwkv_gn_g': 1.0 + nrm((L, BR_W), 0.05),
        'rwkv_gn_b': nrm((L, BR_W), 0.01),
        'rwkv_proj': nrm((L, BR_W, D), BR_W ** -0.5),
        'conv_w': nrm((L, CONV_K, BR_W), CONV_K ** -0.5),
        'conv_b': nrm((L, BR_W), 0.01),
        'conv_ln_g': 1.0 + nrm((L, BR_W), 0.05),
        'conv_ln_b': nrm((L, BR_W), 0.01),
        'conv_proj': nrm((L, BR_W, D), BR_W ** -0.5),
        'lru_conv_w': nrm((L, 2, LRU_CONV, BR_W), LRU_CONV ** -0.5),
        'lru_conv_b': nrm((L, 2, BR_W), 0.01),
        'lru_wr': nrm((L, 2, LRU_BLOCKS, LRU_BS, LRU_BS), LRU_BS ** -0.5),
        'lru_br': nrm((L, 2, BR_W), 0.01),
        'lru_wi': nrm((L, 2, LRU_BLOCKS, LRU_BS, LRU_BS), LRU_BS ** -0.5),
        'lru_bi': nrm((L, 2, BR_W), 0.01),
        'lru_lambda': jnp.log(s_lam) - jnp.log1p(-s_lam),
        'lru_proj': nrm((L, BR_W, D), BR_W ** -0.5),
        'ret_gn_g': 1.0 + nrm((L, BR_W), 0.05),
        'ret_gn_b': nrm((L, BR_W), 0.01),
        'ret_proj': nrm((L, BR_W, D), BR_W ** -0.5),
        'w_out': nrm((L, D, D), D ** -0.5),
        'w_router': nrm((D, N_EXPERTS), D ** -0.5),
        'b_router': nrm((N_EXPERTS,), 0.01),
        'e_w1': nrm((L, N_EXPERTS, D, D_EXPERT), D ** -0.5),
        'e_w3': nrm((L, N_EXPERTS, D, D_EXPERT), D ** -0.5),
        'e_w2': nrm((L, N_EXPERTS, D_EXPERT, D), D_EXPERT ** -0.5),
        'final_g': 1.0 + nrm((D,), 0.05),
    }


def reference(x, c, ctx, c_ctx, w_ada, b_ada, norm1_g, norm2_g, w_in, b_merge,
              rwkv_mu_prev, rwkv_mu_next, rwkv_w0, rwkv_w2, rwkv_a0, rwkv_a2, rwkv_g2,
              rwkv_k_k, rwkv_k_a, rwkv_r_k, rwkv_gn_g, rwkv_gn_b, rwkv_proj,
              conv_w, conv_b, conv_ln_g, conv_ln_b, conv_proj,
              lru_conv_w, lru_conv_b, lru_wr, lru_br, lru_wi, lru_bi, lru_lambda, lru_proj,
              ret_gn_g, ret_gn_b, ret_proj, w_out, w_router, b_router, e_w1, e_w3, e_w2, final_g):
    n_tok = x.shape[1]
    n_rows = n_tok // GRID_W
    rows = jnp.repeat(jnp.arange(n_rows, dtype=F32), GRID_W)
    cols = jnp.tile(jnp.arange(GRID_W, dtype=F32), n_rows)
    per_layer = {
        'w_in': w_in, 'b_merge': b_merge,
        'rwkv_mu_prev': rwkv_mu_prev, 'rwkv_mu_next': rwkv_mu_next, 'rwkv_w0': rwkv_w0,
        'rwkv_w2': rwkv_w2, 'rwkv_a0': rwkv_a0, 'rwkv_a2': rwkv_a2, 'rwkv_g2': rwkv_g2,
        'rwkv_k_k': rwkv_k_k, 'rwkv_k_a': rwkv_k_a, 'rwkv_r_k': rwkv_r_k,
        'rwkv_gn_g': rwkv_gn_g, 'rwkv_gn_b': rwkv_gn_b, 'rwkv_proj': rwkv_proj,
        'conv_w': conv_w, 'conv_b': conv_b, 'conv_ln_g': conv_ln_g, 'conv_ln_b': conv_ln_b,
        'conv_proj': conv_proj,
        'lru_conv_w': lru_conv_w, 'lru_conv_b': lru_conv_b, 'lru_wr': lru_wr, 'lru_br': lru_br,
        'lru_wi': lru_wi, 'lru_bi': lru_bi, 'lru_lambda': lru_lambda, 'lru_proj': lru_proj,
        'ret_gn_g': ret_gn_g, 'ret_gn_b': ret_gn_b, 'ret_proj': ret_proj, 'w_out': w_out,
    }
    h_ctx = ctx
    for layer in range(DEPTH):
        lp = {name: arr[layer] for name, arr in per_layer.items()}
        need_ctx = layer < DEPTH - 1
        mod_l = jax.nn.silu(c) @ w_ada[layer] + b_ada[layer]
        mod_c = jax.nn.silu(c_ctx) @ w_ada[layer] + b_ada[layer]
        sh1, sc1, g1, sh2, sc2, g2 = jnp.split(mod_l[:, None, :], 6, axis=-1)
        csh1, csc1, cg1, csh2, csc2, cg2 = jnp.split(mod_c, 6, axis=-1)
        hl = _modulate(_rmsnorm(x, norm1_g[layer]), sh1, sc1)
        hc = _modulate(_rmsnorm(h_ctx, norm1_g[layer]), csh1, csc1)
        out_c, out_l = _mixer(hc, hl, lp, rows, cols, need_ctx)
        x = x + (g1 * out_l).astype(x.dtype)
        hl2 = _modulate(_rmsnorm(x, norm2_g[layer]), sh2, sc2)
        if need_ctx:
            h_ctx = h_ctx + (cg1 * out_c).astype(h_ctx.dtype)
            hc2 = _modulate(_rmsnorm(h_ctx, norm2_g[layer]), csh2, csc2)
            n_c = hc2.shape[0] * hc2.shape[1]
            y = _moe(jnp.concatenate([hc2.reshape(-1, D_MODEL), hl2.reshape(-1, D_MODEL)], axis=0),
                     w_router, b_router, e_w1[layer], e_w3[layer], e_w2[layer])
            h_ctx = h_ctx + (cg2 * y[:n_c].reshape(hc2.shape)).astype(h_ctx.dtype)
            y_lat = y[n_c:].reshape(hl2.shape)
        else:
            y_lat = _moe(hl2.reshape(-1, D_MODEL), w_router, b_router,
                         e_w1[layer], e_w3[layer], e_w2[layer]).reshape(hl2.shape)
        x = x + (g2 * y_lat).astype(x.dtype)
    return _rmsnorm(x, final_g)
```

```python
import functools

import jax
import jax.numpy as jnp
from jax import lax
from jax.experimental import pallas as pl
from jax.experimental.pallas import tpu as pltpu

F32 = jnp.float32
BF16 = jnp.bfloat16

D_MODEL = 1024
GRID_W = 64
N_BRANCH = 4
BR_W = D_MODEL // 2
RWKV_HS = 64
RWKV_H = BR_W // RWKV_HS
RWKV_W_LORA = 64
RWKV_A_LORA = 64
RWKV_G_LORA = 128
RWKV_GN_EPS = 64e-5
L2_EPS = 1e-12
CONV_K = 31
LRU_BLOCKS = 8
LRU_BS = BR_W // LRU_BLOCKS
LRU_CONV = 4
LRU_C = 8.0
RET_H = 4
RET_DK = 64
RET_DV = BR_W // RET_H
RET_CHUNK = 128
ROPE_BASE = 10000.0
N_EXPERTS = 32
N_GROUPS = 4
EXPERTS_PER_GROUP = N_EXPERTS // N_GROUPS
TOP_K = 2
D_EXPERT = D_MODEL // 2
RMS_EPS = 1e-6
LN_EPS = 1e-5

RWKV_COLS = (('r', BR_W), ('k', BR_W), ('v', BR_W), ('wlo0', RWKV_W_LORA), ('wlo1', RWKV_W_LORA),
             ('alo0', RWKV_A_LORA), ('alo1', RWKV_A_LORA), ('glo', RWKV_G_LORA))
REST_COLS = (('conv_val', BR_W), ('conv_gate', BR_W), ('lru_x', BR_W), ('lru_gate', BR_W),
             ('ret_q', RET_H * RET_DK), ('ret_k', RET_H * RET_DK), ('ret_v', BR_W), ('ret_g', BR_W),
             ('merge', N_BRANCH * D_MODEL))
RWKV_WIDTH = 3 * BR_W + 2 * RWKV_W_LORA + 2 * RWKV_A_LORA + RWKV_G_LORA

VMEM_LIMIT = 48 * 1024 * 1024
RWKV_CHUNK = 64
MOE_BM = 256


def _fused_mm_body(*refs, nr, na, ner, nea, prologue, epilogue, out_dtype):
    rows = refs[:nr]
    auxs = refs[nr:nr + na]
    w_ref = refs[nr + na]
    er = refs[nr + na + 1:nr + na + 1 + ner]
    ea = refs[nr + na + 1 + ner:nr + na + 1 + ner + nea]
    o_ref, act = refs[-2], refs[-1]
    i = pl.program_id(0)

    @pl.when(pl.program_id(1) == 0)
    def _():
        act[...] = prologue(i, *[r[...] for r in rows], *[a[...] for a in auxs]).astype(BF16)

    acc = jnp.dot(act[...], w_ref[...], preferred_element_type=F32)
    if epilogue is not None:
        acc = epilogue(i, acc, *[r[...] for r in er], *[a[...] for a in ea])
    o_ref[...] = acc.astype(out_dtype)


def _fused_mm(row_ins, aux_ins, w, prologue, *, tm, tn, out_dtype=F32, epilogue=None,
              epi_rows=(), epi_aux=()):
    n = row_ins[0].shape[0]
    k, m = w.shape
    assert n % tm == 0 and m % tn == 0, (n, tm, m, tn)
    in_specs = [pl.BlockSpec((tm, r.shape[1]), lambda i, j: (i, 0)) for r in row_ins]
    in_specs += [pl.BlockSpec(bs, im) for (_, bs, im) in aux_ins]
    in_specs += [pl.BlockSpec((k, tn), lambda i, j: (0, j))]
    in_specs += [pl.BlockSpec((tm, tn), lambda i, j: (i, j)) for _ in epi_rows]
    in_specs += [pl.BlockSpec(bs, im) for (_, bs, im) in epi_aux]
    body = functools.partial(_fused_mm_body, nr=len(row_ins), na=len(aux_ins), ner=len(epi_rows),
                             nea=len(epi_aux), prologue=prologue, epilogue=epilogue, out_dtype=out_dtype)
    return pl.pallas_call(
        body,
        grid=(n // tm, m // tn),
        in_specs=in_specs,
        out_specs=pl.BlockSpec((tm, tn), lambda i, j: (i, j)),
        out_shape=jax.ShapeDtypeStruct((n, m), out_dtype),
        scratch_shapes=[pltpu.VMEM((tm, k), BF16)],
        compiler_params=pltpu.CompilerParams(dimension_semantics=("parallel", "arbitrary"),
                                             vmem_limit_bytes=VMEM_LIMIT),
    )(*row_ins, *[a for (a, _, _) in aux_ins], w, *epi_rows, *[a for (a, _, _) in epi_aux])


def _whole(a):
    nd = a.ndim
    return (a, a.shape, lambda i, j: (0,) * nd)


def _plain_mm(x, w, *, tm, tn, out_dtype=F32):
    return _fused_mm([x], [], w, lambda i, xb: xb, tm=tm, tn=tn, out_dtype=out_dtype)


def _bdot(a, b):
    return jnp.dot(a.astype(BF16), b.astype(BF16), preferred_element_type=F32)


_NT = (((1,), (1,)), ((), ()))
_TN = (((0,), (0,)), ((), ()))


def _rwkv_body(r_ref, lw_ref, k_ref, v_ref, kk_ref, a_ref, o_ref, s_ref, *, n_chunks):
    c_len = RWKV_CHUNK

    @pl.when(pl.program_id(2) == 0)
    def _():
        s_ref[...] = jnp.zeros_like(s_ref)

    ri = lax.broadcasted_iota(jnp.int32, (c_len, c_len), 0)
    ci = lax.broadcasted_iota(jnp.int32, (c_len, c_len), 1)
    tri_s = (ri > ci).astype(F32)
    tri_i = (ri >= ci).astype(F32)
    eye = (ri == ci).astype(F32)

    def level_mask(size):
        same = (ri // size) == (ci // size)
        inner = (ri // (size // 2)) == (ci // (size // 2))
        return jnp.where(same & jnp.logical_not(inner) & (ri > ci), 1.0, 0.0).astype(F32)

    m8 = jnp.where(((ri // 8) == (ci // 8)) & (ri > ci), 1.0, 0.0).astype(F32)
    merge_masks = [level_mask(s) for s in (16, 32, 64)]
    tri_b = tri_i.astype(BF16)

    def chunk(c, carry):
        sl = pl.ds(pl.multiple_of(c * c_len, c_len), c_len)
        r = r_ref[0, 0, sl, :]
        lw = lw_ref[0, 0, sl, :]
        k = k_ref[0, 0, sl, :]
        v = v_ref[0, 0, sl, :]
        kk = kk_ref[0, 0, sl, :]
        a = a_ref[0, 0, sl, :]
        hi = lw.astype(BF16)
        lo = (lw - hi.astype(F32)).astype(BF16)
        cum = (jnp.dot(tri_b, hi, preferred_element_type=F32)
               + jnp.dot(tri_b, lo, preferred_element_type=F32))
        eg = jnp.exp(cum)
        ieg = jnp.exp(-cum)
        rt = (r * eg).astype(BF16)
        kt = (k * ieg).astype(BF16)
        at = (-(kk * a) * ieg).astype(BF16)
        bt = (kk * jnp.exp(cum - lw)).astype(BF16)
        vb = v.astype(BF16)
        g_last = eg[c_len - 1:c_len, :]
        for h in range(RWKV_H):
            hs = slice(h * RWKV_HS, (h + 1) * RWKV_HS)
            rb = jnp.concatenate([rt[:, hs], bt[:, hs]], axis=0)
            at_h, kt_h, v_h = at[:, hs], kt[:, hs], vb[:, hs]
            m_a = lax.dot_general(rb, at_h, _NT, preferred_element_type=F32)
            m_k = lax.dot_general(rb, kt_h, _NT, preferred_element_type=F32)
            ra = m_a[:c_len] * tri_i
            nmat = m_a[c_len:] * tri_s
            rk = m_k[:c_len] * tri_i
            bk = m_k[c_len:] * tri_s
            d1 = nmat * m8
            d2 = _bdot(d1, d1)
            d4 = _bdot(d2, d2)
            tinv = eye + d1
            tinv = tinv + _bdot(tinv, d2)
            tinv = tinv + _bdot(tinv, d4)
            for mk in merge_masks:
                tinv = tinv + _bdot(_bdot(tinv, nmat * mk), tinv)
            s_h = s_ref[h]
            rbs = lax.dot_general(rb, s_h.astype(BF16), _NT, preferred_element_type=F32)
            x = rbs[c_len:] + _bdot(bk, v_h)
            u = _bdot(tinv, x)
            y = rbs[:c_len] + _bdot(ra, u) + _bdot(rk, v_h)
            uv = jnp.concatenate([u.astype(BF16), v_h], axis=0)
            ak = jnp.concatenate([at_h, kt_h], axis=0)
            ds = lax.dot_general(uv, ak, _TN, preferred_element_type=F32)
            s_ref[h] = (s_h + ds) * g_last[:, hs]
            o_ref[0, 0, sl, hs] = y
        return carry

    lax.fori_loop(0, n_chunks, chunk, 0)


def _rwkv_scan(r, lw, k, v, kk, a, *, tb):
    nd, bsz, seq, width = r.shape
    assert seq % tb == 0 and tb % RWKV_CHUNK == 0
    spec = pl.BlockSpec((1, 1, tb, width), lambda d, b, t: (d, b, t, 0))
    return pl.pallas_call(
        functools.partial(_rwkv_body, n_chunks=tb // RWKV_CHUNK),
        grid=(nd, bsz, seq // tb),
        in_specs=[spec] * 6,
        out_specs=spec,
        out_shape=jax.ShapeDtypeStruct(r.shape, F32),
        scratch_shapes=[pltpu.VMEM((RWKV_H, RWKV_HS, RWKV_HS), F32)],
        compiler_params=pltpu.CompilerParams(
            dimension_semantics=("parallel", "parallel", "arbitrary"), vmem_limit_bytes=VMEM_LIMIT),
    )(r, lw, k, v, kk, a)


def _moe_body(be_ref, nu_ref, x_ref, w1_ref, w3_ref, w2_ref, o_ref):
    i = pl.program_id(0)

    @pl.when(i < nu_ref[0])
    def _():
        xb = x_ref[...]
        h1 = jnp.dot(xb, w1_ref[0], preferred_element_type=F32)
        h3 = jnp.dot(xb, w3_ref[0], preferred_element_type=F32)
        hid = (h1 * jax.nn.sigmoid(h1) * h3).astype(BF16)
        o_ref[...] = jnp.dot(hid, w2_ref[0], preferred_element_type=F32)

    @pl.when(i >= nu_ref[0])
    def _():
        o_ref[...] = jnp.zeros_like(o_ref)


def _moe_experts(xb, block_e, n_used, w1, w3, w2):
    n_rows, d = xb.shape
    n_blocks = n_rows // MOE_BM
    de = w1.shape[2]
    grid_spec = pltpu.PrefetchScalarGridSpec(
        num_scalar_prefetch=2,
        grid=(n_blocks,),
        in_specs=[
            pl.BlockSpec((MOE_BM, d), lambda i, be, nu: (i, 0)),
            pl.BlockSpec((1, d, de), lambda i, be, nu: (be[i], 0, 0)),
            pl.BlockSpec((1, d, de), lambda i, be, nu: (be[i], 0, 0)),
            pl.BlockSpec((1, de, d), lambda i, be, nu: (be[i], 0, 0)),
        ],
        out_specs=pl.BlockSpec((MOE_BM, d), lambda i, be, nu: (i, 0)),
    )
    return pl.pallas_call(
        _moe_body,
        grid_spec=grid_spec,
        out_shape=jax.ShapeDtypeStruct((n_rows, d), F32),
        compiler_params=pltpu.CompilerParams(dimension_semantics=("arbitrary",),
                                             vmem_limit_bytes=VMEM_LIMIT),
    )(block_e, n_used, xb, w1, w3, w2)


def _moe(tokens, w_router, b_router, w1, w3, w2):
    n_tok = tokens.shape[0]
    aff = jax.nn.sigmoid(jnp.dot(tokens, w_router, precision=lax.Precision.HIGHEST))
    sel = (aff + b_router).reshape(-1, N_GROUPS, EXPERTS_PER_GROUP)
    grp_score = jnp.sum(lax.top_k(sel, TOP_K)[0], axis=-1)
    grp = jnp.argmax(grp_score, axis=-1)
    grp_mask = jnp.arange(N_GROUPS)[None, :] == grp[:, None]
    masked = jnp.where(grp_mask[:, :, None], sel, -jnp.inf).reshape(-1, N_EXPERTS)
    _, e_idx = lax.top_k(masked, TOP_K)
    wts = jnp.take_along_axis(aff, e_idx, axis=1)
    wts = wts / jnp.sum(wts, axis=-1, keepdims=True)

    n_asg = n_tok * TOP_K
    flat_e = e_idx.reshape(-1).astype(jnp.int32)
    flat_t = jnp.repeat(jnp.arange(n_tok, dtype=jnp.int32), TOP_K)
    order = jnp.argsort(flat_e)
    se, st = flat_e[order], flat_t[order]
    counts = jnp.bincount(flat_e, length=N_EXPERTS)
    starts = jnp.cumsum(counts) - counts
    padded = (counts + MOE_BM - 1) // MOE_BM * MOE_BM
    pad_end = jnp.cumsum(padded)
    pad_start = pad_end - padded
    dest = (pad_start[se] + jnp.arange(n_asg, dtype=jnp.int32) - starts[se]).astype(jnp.int32)
    n_blocks = -(-n_asg // MOE_BM) + N_EXPERTS
    slot_tok = jnp.full((n_blocks * MOE_BM,), n_tok, jnp.int32).at[dest].set(st)
    block_e = jnp.minimum(jnp.searchsorted(pad_end, jnp.arange(n_blocks) * MOE_BM, side='right'),
                          N_EXPERTS - 1).astype(jnp.int32)
    n_used = (pad_end[-1] // MOE_BM).astype(jnp.int32).reshape(1)
    x_pad = jnp.concatenate([tokens.astype(BF16), jnp.zeros((1, tokens.shape[1]), BF16)], axis=0)
    xb = x_pad[slot_tok]
    yb = _moe_experts(xb, block_e, n_used, w1, w3, w2)
    dest_asg = jnp.zeros((n_asg,), jnp.int32).at[order].set(dest).reshape(n_tok, TOP_K)
    return wts[:, 0:1] * yb[dest_asg[:, 0]] + wts[:, 1:2] * yb[dest_asg[:, 1]]


def _split(p, layout):
    out, off = {}, 0
    for name, width in layout:
        out[name] = p[..., off:off + width]
        off += width
    return out


def _heads(z, n_heads, head_dim):
    return z.reshape(*z.shape[:2], n_heads, head_dim)


def _shift_prev(z):
    return jnp.pad(z, ((0, 0), (1, 0), (0, 0)))[:, :-1]


def _shift_next(z):
    return jnp.pad(z, ((0, 0), (0, 1), (0, 0)))[:, 1:]


def _layernorm(x, g, b, eps):
    xc = x - jnp.mean(x, axis=-1, keepdims=True)
    return xc * lax.rsqrt(jnp.mean(xc * xc, axis=-1, keepdims=True) + eps) * g + b


def _head_norm(y, g, b, eps):
    yc = y - jnp.mean(y, axis=-1, keepdims=True)
    yn = yc * lax.rsqrt(jnp.mean(yc * yc, axis=-1, keepdims=True) + eps)
    return yn.reshape(*y.shape[:2], -1) * g + b


def _depthwise_conv(z, w, b, pad_lo, pad_hi):
    ch = z.shape[-1]
    y = lax.conv_general_dilated(z, w[:, None, :], (1,), [(pad_lo, pad_hi)],
                                 dimension_numbers=('NWC', 'WIO', 'NWC'), feature_group_count=ch)
    return y + b


def _two_way(scan_f, scan_b, ctx_f, ctx_b, lat_f, lat_b, s_zero):
    flip = lambda tup: tuple(jnp.flip(t, 1) for t in tup)
    yc_f, sc_f = scan_f(ctx_f, s_zero)
    yc_b, sc_b = scan_b(flip(ctx_b), s_zero)
    yl_f, _ = scan_f(lat_f, sc_f)
    yl_b, _ = scan_b(flip(lat_b), sc_b)
    return yc_f + jnp.flip(yc_b, 1), yl_f + jnp.flip(yl_b, 1)


def _rwkv_dir_inputs(ca, lp, d):
    k = ca['k']
    w_raw = lp['rwkv_w0'][d] + jnp.tanh(ca['wlo%d' % d]) @ lp['rwkv_w2'][d]
    log_w = -jnp.exp(-jax.nn.softplus(-w_raw) - 0.5)
    a = jax.nn.sigmoid(lp['rwkv_a0'][d] + ca['alo%d' % d] @ lp['rwkv_a2'][d])
    k_d = k * (1.0 + (a - 1.0) * lp['rwkv_k_a'])
    return log_w, k_d, a


def _rwkv_kk(ca, lp):
    kk = _heads(ca['k'] * lp['rwkv_k_k'], RWKV_H, RWKV_HS)
    kk = kk * lax.rsqrt(jnp.sum(kk * kk, axis=-1, keepdims=True) + L2_EPS)
    return kk.reshape(*ca['k'].shape)


def _rwkv_mix(ca_c, ca_l, lp):
    tc = ca_c['r'].shape[1]
    seqs = {n: [] for n in ('r', 'lw', 'k', 'v', 'kk', 'a')}
    kk_c, kk_l = _rwkv_kk(ca_c, lp), _rwkv_kk(ca_l, lp)
    for d in range(2):
        fl = (lambda z: z) if d == 0 else (lambda z: jnp.flip(z, 1))
        lw_c, k_c, a_c = _rwkv_dir_inputs(ca_c, lp, d)
        lw_l, k_l, a_l = _rwkv_dir_inputs(ca_l, lp, d)
        cat = lambda zc, zl: jnp.concatenate([fl(zc), fl(zl)], axis=1)
        seqs['r'].append(cat(ca_c['r'], ca_l['r']))
        seqs['lw'].append(cat(lw_c, lw_l))
        seqs['k'].append(cat(k_c, k_l))
        seqs['v'].append(cat(ca_c['v'], ca_l['v']))
        seqs['kk'].append(cat(kk_c, kk_l))
        seqs['a'].append(cat(a_c, a_l))
    stk = {n: jnp.stack(v, axis=0) for n, v in seqs.items()}
    seq = stk['r'].shape[2]
    tb = max(t for t in range(RWKV_CHUNK, 256 + 1, RWKV_CHUNK) if seq % t == 0)
    y = _rwkv_scan(stk['r'], stk['lw'], stk['k'], stk['v'], stk['kk'], stk['a'], tb=tb)
    y_c = y[0, :, :tc] + jnp.flip(y[1, :, :tc], 1)
    y_l = y[0, :, tc:] + jnp.flip(y[1, :, tc:], 1)
    return y_c, y_l


def _rwkv_out_act(y, ca, lp):
    r = _heads(ca['r'], RWKV_H, RWKV_HS)
    k = _heads(ca['k'], RWKV_H, RWKV_HS)
    v = _heads(ca['v'], RWKV_H, RWKV_HS)
    g = jax.nn.sigmoid(ca['glo']) @ lp['rwkv_g2']
    bonus = jnp.sum(r * k * lp['rwkv_r_k'].reshape(RWKV_H, RWKV_HS), axis=-1, keepdims=True) * v
    yn = _head_norm(_heads(y, RWKV_H, RWKV_HS), lp['rwkv_gn_g'], lp['rwkv_gn_b'], RWKV_GN_EPS)
    return (yn + bonus.reshape(*bonus.shape[:2], BR_W)) * g


def _conformer_act(cr, lp):
    u = cr['conv_val'] * jax.nn.sigmoid(cr['conv_gate'])
    u = _depthwise_conv(u, lp['conv_w'], lp['conv_b'], CONV_K // 2, CONV_K // 2)
    return jax.nn.silu(_layernorm(u, lp['conv_ln_g'], lp['conv_ln_b'], LN_EPS))


def _lru_scan_inputs(cr, lp):
    dirs = []
    for d in range(2):
        pad = (LRU_CONV - 1, 0) if d == 0 else (0, LRU_CONV - 1)
        xc = _depthwise_conv(cr['lru_x'], lp['lru_conv_w'][d], lp['lru_conv_b'][d], *pad)
        xb = _heads(xc, LRU_BLOCKS, LRU_BS)
        gate_r = jax.nn.sigmoid(jnp.einsum('btgi,gij->btgj', xb, lp['lru_wr'][d]).reshape(xc.shape)
                                + lp['lru_br'][d])
        gate_i = jax.nn.sigmoid(jnp.einsum('btgi,gij->btgj', xb, lp['lru_wi'][d]).reshape(xc.shape)
                                + lp['lru_bi'][d])
        log_a = -LRU_C * gate_r * jax.nn.softplus(-lp['lru_lambda'][d])
        dirs.append((jnp.exp(log_a), jnp.sqrt(-jnp.expm1(2.0 * log_a)) * (gate_i * xc)))
    return dirs


def _linear_scan(inputs, h0):
    a, b = inputs

    def combine(lhs, rhs):
        return lhs[0] * rhs[0], rhs[0] * lhs[1] + rhs[1]
    a_cum, h = lax.associative_scan(combine, (a, b), axis=1)
    h = h + a_cum * h0[:, None, :]
    return h, h[:, -1]


def _rotary_2d(z, rows, cols):
    n_freq = RET_DK // 4
    inv = ROPE_BASE ** (-jnp.arange(n_freq, dtype=F32) / n_freq)
    ang = jnp.concatenate([rows[:, None] * inv, cols[:, None] * inv], axis=-1)[None, :, None, :]
    cos, sin = jnp.cos(ang), jnp.sin(ang)
    z1, z2 = z[..., :RET_DK // 2], z[..., RET_DK // 2:]
    return jnp.concatenate([z1 * cos - z2 * sin, z1 * sin + z2 * cos], axis=-1)


def _ret_scan_inputs(cr, rows, cols):
    q = _heads(cr['ret_q'], RET_H, RET_DK)
    k = _heads(cr['ret_k'], RET_H, RET_DK) * (RET_DK ** -0.5)
    v = _heads(cr['ret_v'], RET_H, RET_DV)
    if rows is not None:
        q, k = _rotary_2d(q, rows, cols), _rotary_2d(k, rows, cols)
    return (q, k, v)


def _retention_chunkwise(inputs, s0, include_diag):
    q, k, v = inputs
    bsz, n_tok = q.shape[:2]
    n_chunk = n_tok // RET_CHUNK
    log_g = jnp.log1p(-jnp.exp2(-5.0 - jnp.arange(RET_H, dtype=F32)))
    idx = jnp.arange(RET_CHUNK, dtype=F32)
    diff = idx[:, None] - idx[None, :]
    keep = diff >= 0 if include_diag else diff > 0
    dmat = jnp.where(keep[None], jnp.exp(log_g[:, None, None] * jnp.maximum(diff, 0.0)[None]), 0.0)
    xi = jnp.exp(log_g[None, :] * (idx[:, None] + 1.0))
    zeta = jnp.exp(log_g[None, :] * (RET_CHUNK - 1.0 - idx)[:, None])
    g_chunk = jnp.exp(log_g * RET_CHUNK)
    chunks = lambda t: jnp.moveaxis(t.reshape(bsz, n_chunk, RET_CHUNK, *t.shape[2:]), 1, 0)

    def step(S, xs):
        qc, kc, vc = xs
        att = jnp.einsum('bihd,bjhd->bhij', qc, kc) * dmat
        o = (jnp.einsum('bhij,bjhv->bihv', att, vc)
             + jnp.einsum('bihd,bhdv->bihv', qc, S) * xi[None, :, :, None])
        S = S * g_chunk[None, :, None, None] + jnp.einsum('bjhd,bjhv->bhdv', kc * zeta[None, :, :, None], vc)
        return S, o
    s_fin, o = lax.scan(step, s0, (chunks(q), chunks(k), chunks(v)))
    return jnp.moveaxis(o, 0, 1).reshape(bsz, n_tok, RET_H, RET_DV), s_fin


def _ret_act(o, cr, lp):
    yn = _head_norm(o, lp['ret_gn_g'], lp['ret_gn_b'], LN_EPS)
    return jax.nn.silu(cr['ret_g']) * yn


def _rows2d(z):
    return z.reshape(-1, z.shape[-1])


def _mixer_acts(p, lp, tc, rows, cols):
    bsz = p.shape[0]
    parts = []
    for seg in (p[:, :tc], p[:, tc:]):
        za = seg[..., :RWKV_WIDTH]
        za = (za + lp['rwkv_mu_prev'] * (_shift_prev(za) - za)
              + lp['rwkv_mu_next'] * (_shift_next(za) - za))
        parts.append((_split(za, RWKV_COLS), _split(seg[..., RWKV_WIDTH:], REST_COLS)))
    (ca_c, cr_c), (ca_l, cr_l) = parts
    ya_c, ya_l = _rwkv_mix(ca_c, ca_l, lp)
    act_a = [_rwkv_out_act(ya_c, ca_c, lp), _rwkv_out_act(ya_l, ca_l, lp)]
    act_b = [_conformer_act(cr_c, lp), _conformer_act(cr_l, lp)]
    lc, ll = _lru_scan_inputs(cr_c, lp), _lru_scan_inputs(cr_l, lp)
    yl_c, yl_l = _two_way(_linear_scan, _linear_scan, lc[0], lc[1], ll[0], ll[1], jnp.zeros((bsz, BR_W), F32))
    act_c = [yl_c * jax.nn.gelu(cr_c['lru_gate']), yl_l * jax.nn.gelu(cr_l['lru_gate'])]
    ret_f = functools.partial(_retention_chunkwise, include_diag=True)
    ret_b = functools.partial(_retention_chunkwise, include_diag=False)
    qc = _ret_scan_inputs(cr_c, None, None)
    ql = _ret_scan_inputs(cr_l, rows, cols)
    yr_c, yr_l = _two_way(ret_f, ret_b, qc, qc, ql, ql, jnp.zeros((bsz, RET_H, RET_DK, RET_DV), F32))
    act_d = [_ret_act(yr_c, cr_c, lp), _ret_act(yr_l, cr_l, lp)]
    cat = lambda pr: jnp.concatenate(pr, axis=1)
    merge = cat([cr_c['merge'], cr_l['merge']])
    return [cat(act_a), cat(act_b), cat(act_c), cat(act_d)], merge


def kernel(x, c, ctx, c_ctx, w_ada, b_ada, norm1_g, norm2_g, w_in, b_merge, rwkv_mu_prev, rwkv_mu_next, rwkv_w0, rwkv_w2, rwkv_a0, rwkv_a2, rwkv_g2, rwkv_k_k, rwkv_k_a, rwkv_r_k, rwkv_gn_g, rwkv_gn_b, rwkv_proj, conv_w, conv_b, conv_ln_g, conv_ln_b, conv_proj, lru_conv_w, lru_conv_b, lru_wr, lru_br, lru_wi, lru_bi, lru_lambda, lru_proj, ret_gn_g, ret_gn_b, ret_proj, w_out, w_router, b_router, e_w1, e_w3, e_w2, final_g):
    bsz, n_tok, d = x.shape
    tc = ctx.shape[1]
    seq = tc + n_tok
    depth = w_in.shape[0]
    n_rows = n_tok // GRID_W
    rows = jnp.repeat(jnp.arange(n_rows, dtype=F32), GRID_W)
    cols = jnp.tile(jnp.arange(GRID_W, dtype=F32), n_rows)
    per_layer = {
        'b_merge': b_merge,
        'rwkv_mu_prev': rwkv_mu_prev, 'rwkv_mu_next': rwkv_mu_next, 'rwkv_w0': rwkv_w0,
        'rwkv_w2': rwkv_w2, 'rwkv_a0': rwkv_a0, 'rwkv_a2': rwkv_a2, 'rwkv_g2': rwkv_g2,
        'rwkv_k_k': rwkv_k_k, 'rwkv_k_a': rwkv_k_a, 'rwkv_r_k': rwkv_r_k,
        'rwkv_gn_g': rwkv_gn_g, 'rwkv_gn_b': rwkv_gn_b,
        'conv_w': conv_w, 'conv_b': conv_b, 'conv_ln_g': conv_ln_g, 'conv_ln_b': conv_ln_b,
        'lru_conv_w': lru_conv_w, 'lru_conv_b': lru_conv_b, 'lru_wr': lru_wr, 'lru_br': lru_br,
        'lru_wi': lru_wi, 'lru_bi': lru_bi, 'lru_lambda': lru_lambda,
        'ret_gn_g': ret_gn_g, 'ret_gn_b': ret_gn_b,
    }
    tm = seq // 2
    assert seq % 2 == 0 and tm % 8 == 0 and tc <= tm
    blocks_per_batch = seq // tm

    def seg_select(i, ctx_val, lat_val):
        row = (i % blocks_per_batch) * tm + lax.broadcasted_iota(jnp.int32, (tm, 1), 0)
        return jnp.where(row < tc, ctx_val, lat_val)

    def norm_mod(i, xb, g, sh_l, sc_l, sh_c, sc_c):
        y = xb * lax.rsqrt(jnp.mean(xb * xb, axis=-1, keepdims=True) + RMS_EPS) * g
        return y * (1.0 + seg_select(i, sc_c, sc_l[0])) + seg_select(i, sh_c, sh_l[0])

    def per_batch(a):
        return (a, (1, 1, a.shape[2]), lambda i, j: (i // blocks_per_batch, 0, 0))

    def per_batch_col(a, tn):
        return (a, (1, 1, tn), lambda i, j: (i // blocks_per_batch, 0, j))

    z = jnp.concatenate([ctx, x], axis=1).reshape(bsz * seq, d)
    for layer in range(depth):
        lp = {name: arr[layer] for name, arr in per_layer.items()}
        last = layer == depth - 1
        cc = jnp.concatenate([c, c_ctx[None, :]], axis=0)
        n_mod = -(-cc.shape[0] // 8) * 8
        cc = jnp.pad(cc, ((0, n_mod - cc.shape[0]), (0, 0)))
        mod = _fused_mm([cc], [], w_ada[layer].astype(BF16), lambda i, cb: cb * jax.nn.sigmoid(cb),
                        tm=n_mod, tn=6 * d // 4,
                        epilogue=lambda i, acc, bb: acc + bb,
                        epi_aux=[(b_ada[layer][None, :], (1, 6 * d // 4), lambda i, j: (0, j))])
        mod_l = mod[:bsz].reshape(bsz, 1, 6 * d)
        mod_c = mod[bsz:bsz + 1]
        sh1, sc1, g1, sh2, sc2, g2 = [mod_l[:, :, n * d:(n + 1) * d] for n in range(6)]
        csh1, csc1, cg1, csh2, csc2, cg2 = [mod_c[:, n * d:(n + 1) * d] for n in range(6)]

        p = _fused_mm([z], [_whole(norm1_g[layer][None, :]), per_batch(sh1), per_batch(sc1),
                            _whole(csh1), _whole(csc1)],
                      w_in[layer].astype(BF16), norm_mod, tm=tm, tn=640)
        acts, merge = _mixer_acts(p.reshape(bsz, seq, -1), lp, tc, rows, cols)
        branches = [
            _plain_mm(_rows2d(act), w.astype(BF16), tm=tm, tn=d)
            for act, w in zip(acts, (rwkv_proj[layer], conv_proj[layer], lru_proj[layer], ret_proj[layer]))
        ]

        z = _merge_out(z, merge, branches, lp, w_out[layer], g1, cg1, seq, tc, d)

        z3 = z.reshape(bsz, seq, d)
        zn = z3 * lax.rsqrt(jnp.mean(z3 * z3, axis=-1, keepdims=True) + RMS_EPS) * norm2_g[layer]
        is_ctx = (jnp.arange(seq) < tc)[None, :, None]
        h2 = zn * (1.0 + jnp.where(is_ctx, csc2[None], sc2)) + jnp.where(is_ctx, csh2[None], sh2)
        gate2 = jnp.where(is_ctx, cg2[None], g2)
        w1b, w3b, w2b = e_w1[layer].astype(BF16), e_w3[layer].astype(BF16), e_w2[layer].astype(BF16)
        if not last:
            y = _moe(h2.reshape(-1, d), w_router, b_router, w1b, w3b, w2b).reshape(bsz, seq, d)
            z = (z3 + gate2 * y).reshape(bsz * seq, d)
        else:
            y = _moe(h2[:, tc:].reshape(-1, d), w_router, b_router, w1b, w3b, w2b).reshape(bsz, n_tok, d)
            xl = z3[:, tc:] + g2 * y
            return xl * lax.rsqrt(jnp.mean(xl * xl, axis=-1, keepdims=True) + RMS_EPS) * final_g
    return None


def _pick_tile(n, cap):
    best = 8
    for t in range(8, cap + 1, 8):
        if n % t == 0:
            best = t
    return best


def _merge_out(z, merge, branches, lp, w_out_l, g1, cg1, seq, tc, d):
    tm2 = _pick_tile(seq, 288)
    bpb = seq // tm2

    def seg_select(i, ctx_val, lat_val):
        row = (i % bpb) * tm2 + lax.broadcasted_iota(jnp.int32, (tm2, 1), 0)
        return jnp.where(row < tc, ctx_val, lat_val)

    def merge_pro(i, mg, b0, b1, b2, b3, bm):
        gates = jax.nn.sigmoid(mg + bm)
        return (gates[:, 0 * d:1 * d] * b0 + gates[:, 1 * d:2 * d] * b1
                + gates[:, 2 * d:3 * d] * b2 + gates[:, 3 * d:4 * d] * b3)

    def resid_epi(i, acc, zb, gl, gc):
        return zb + seg_select(i, gc, gl[0]) * acc

    tn = d // 2
    return _fused_mm([_rows2d(merge)] + branches, [_whole(lp['b_merge'][None, :])],
                     w_out_l.astype(BF16), merge_pro, tm=tm2, tn=tn,
                     epilogue=resid_epi, epi_rows=[z],
                     epi_aux=[(g1, (1, 1, tn), lambda i, j: (i // bpb, 0, j)),
                              (cg1, (1, tn), lambda i, j: (0, j))])
```

```python
import functools
import math

import jax
import jax.numpy as jnp
from jax import lax
from jax.experimental import pallas as pl
from jax.experimental.pallas import tpu as pltpu

F32 = jnp.float32
BF16 = jnp.bfloat16

D_MODEL = 1024
GRID_W = 64
N_BRANCH = 4
BR_W = D_MODEL // 2
RWKV_HS = 64
RWKV_H = BR_W // RWKV_HS
RWKV_W_LORA = 64
RWKV_A_LORA = 64
RWKV_G_LORA = 128
RWKV_GN_EPS = 64e-5
L2_EPS = 1e-12
CONV_K = 31
LRU_BLOCKS = 8
LRU_BS = BR_W // LRU_BLOCKS
LRU_CONV = 4
LRU_C = 8.0
RET_H = 4
RET_DK = 64
RET_DV = BR_W // RET_H
RET_CHUNK = 128
ROPE_BASE = 10000.0
N_EXPERTS = 32
N_GROUPS = 4
EXPERTS_PER_GROUP = N_EXPERTS // N_GROUPS
TOP_K = 2
D_EXPERT = D_MODEL // 2
RMS_EPS = 1e-6
LN_EPS = 1e-5

REST_COLS = (('conv_val', BR_W), ('conv_gate', BR_W), ('lru_x', BR_W), ('lru_gate', BR_W),
             ('ret_q', RET_H * RET_DK), ('ret_k', RET_H * RET_DK), ('ret_v', BR_W), ('ret_g', BR_W))
RWKV_WIDTH = 3 * BR_W + 2 * RWKV_W_LORA + 2 * RWKV_A_LORA + RWKV_G_LORA
LORA_OFF = 3 * BR_W
LORA_W = 2 * RWKV_W_LORA + 2 * RWKV_A_LORA
MID_WIDTH = sum(w for _, w in REST_COLS)
MERGE_WIDTH = N_BRANCH * D_MODEL

VMEM_LIMIT = 48 * 1024 * 1024
RWKV_CHUNK = 64
RWKV_TB = 256
MOE_BM = 256


def _bdot(a, b):
    return jnp.dot(a.astype(BF16), b.astype(BF16), preferred_element_type=F32)


_NT = (((1,), (1,)), ((), ()))
_TN = (((0,), (0,)), ((), ()))


def _pick_tile(n, cap, mult=8):
    best = None
    for t in range(mult, cap + 1, mult):
        if n % t == 0:
            best = t
    assert best is not None, (n, cap, mult)
    return best


def _fused_mm_body(*refs, nr, na, ner, nea, prologue, epilogue, out_dtype):
    rows = refs[:nr]
    auxs = refs[nr:nr + na]
    w_ref = refs[nr + na]
    er = refs[nr + na + 1:nr + na + 1 + ner]
    ea = refs[nr + na + 1 + ner:nr + na + 1 + ner + nea]
    o_ref, act = refs[-2], refs[-1]
    i = pl.program_id(0)

    @pl.when(pl.program_id(1) == 0)
    def _():
        act[...] = prologue(i, *[r[...] for r in rows], *[a[...] for a in auxs]).astype(BF16)

    acc = jnp.dot(act[...], w_ref[...], preferred_element_type=F32)
    if epilogue is not None:
        acc = epilogue(i, acc, *[r[...] for r in er], *[a[...] for a in ea])
    o_ref[...] = acc.astype(out_dtype)


def _fused_mm(row_ins, aux_ins, w, prologue, *, tm, tn, out_dtype=F32, epilogue=None,
              epi_rows=(), epi_aux=()):
    row_ins = [r if isinstance(r, tuple) else (r, r.shape[1], 0) for r in row_ins]
    n = row_ins[0][0].shape[0]
    k, m = w.shape
    assert n % tm == 0 and m % tn == 0, (n, tm, m, tn)
    in_specs = [pl.BlockSpec((tm, bw), functools.partial(lambda i, j, cb: (i, cb), cb=cb))
                for (_, bw, cb) in row_ins]
    in_specs += [pl.BlockSpec(bs, im) for (_, bs, im) in aux_ins]
    in_specs += [pl.BlockSpec((k, tn), lambda i, j: (0, j))]
    in_specs += [pl.BlockSpec((tm, tn), lambda i, j: (i, j)) for _ in epi_rows]
    in_specs += [pl.BlockSpec(bs, im) for (_, bs, im) in epi_aux]
    body = functools.partial(_fused_mm_body, nr=len(row_ins), na=len(aux_ins), ner=len(epi_rows),
                             nea=len(epi_aux), prologue=prologue, epilogue=epilogue, out_dtype=out_dtype)
    return pl.pallas_call(
        body,
        grid=(n // tm, m // tn),
        in_specs=in_specs,
        out_specs=pl.BlockSpec((tm, tn), lambda i, j: (i, j)),
        out_shape=jax.ShapeDtypeStruct((n, m), out_dtype),
        scratch_shapes=[pltpu.VMEM((tm, k), BF16)],
        compiler_params=pltpu.CompilerParams(dimension_semantics=("parallel", "arbitrary"),
                                             vmem_limit_bytes=VMEM_LIMIT),
    )(*[a for (a, _, _) in row_ins], *[a for (a, _, _) in aux_ins], w, *epi_rows,
      *[a for (a, _, _) in epi_aux])


def _whole(a):
    nd = a.ndim
    return (a, a.shape, lambda i, j: (0,) * nd)


def _plain_mm(x, w, *, tm, tn, out_dtype=F32):
    return _fused_mm([x], [], w, lambda i, xb: xb, tm=tm, tn=tn, out_dtype=out_dtype)


def _in_proj_body(*refs, tc, shift):
    if shift:
        z_ref, g_ref, shl_ref, scl_ref, shc_ref, scc_ref, w_ref, mup_ref, mun_ref, o_ref, act = refs
    else:
        z_ref, g_ref, shl_ref, scl_ref, shc_ref, scc_ref, w_ref, o_ref, act = refs
    seq = z_ref.shape[0]
    row = lax.broadcasted_iota(jnp.int32, (seq, 1), 0)

    @pl.when(pl.program_id(1) == 0)
    def _():
        xb = z_ref[...]
        y = xb * lax.rsqrt(jnp.mean(xb * xb, axis=-1, keepdims=True) + RMS_EPS) * g_ref[...]
        is_ctx = row < tc
        sc = jnp.where(is_ctx, scc_ref[...], scl_ref[0])
        sh = jnp.where(is_ctx, shc_ref[...], shl_ref[0])
        act[...] = (y * (1.0 + sc) + sh).astype(BF16)

    acc = jnp.dot(act[...], w_ref[...], preferred_element_type=F32)
    if shift:
        prev = pltpu.roll(acc, 1, 0)
        nxt = pltpu.roll(acc, seq - 1, 0)
        prev = jnp.where((row == 0) | (row == tc), 0.0, prev)
        nxt = jnp.where((row == tc - 1) | (row == seq - 1), 0.0, nxt)
        acc = acc + mup_ref[...] * (prev - acc) + mun_ref[...] * (nxt - acc)
    o_ref[...] = acc


def _in_proj(z, seq, tc, g, sh_l, sc_l, sh_c, sc_c, w, tn, mu_prev=None, mu_next=None):
    n, d = z.shape
    m = w.shape[1]
    shift = mu_prev is not None
    assert m % tn == 0
    vec = lambda width: pl.BlockSpec((1, width), lambda b, j: (0, 0))
    per_b = pl.BlockSpec((1, 1, d), lambda b, j: (b, 0, 0))
    in_specs = [pl.BlockSpec((seq, d), lambda b, j: (b, 0)), vec(d), per_b, per_b, vec(d), vec(d),
                pl.BlockSpec((d, tn), lambda b, j: (0, j))]
    args = [z, g, sh_l, sc_l, sh_c, sc_c, w]
    if shift:
        in_specs += [pl.BlockSpec((1, tn), lambda b, j: (0, j))] * 2
        args += [mu_prev, mu_next]
    return pl.pallas_call(
        functools.partial(_in_proj_body, tc=tc, shift=shift),
        grid=(n // seq, m // tn),
        in_specs=in_specs,
        out_specs=pl.BlockSpec((seq, tn), lambda b, j: (b, j)),
        out_shape=jax.ShapeDtypeStruct((n, m), F32),
        scratch_shapes=[pltpu.VMEM((seq, d), BF16)],
        compiler_params=pltpu.CompilerParams(dimension_semantics=("parallel", "arbitrary"),
                                             vmem_limit_bytes=VMEM_LIMIT),
    )(*args)


def _rwkv_body(rf, kf, vf, lf, rb_, kb, vb_, lb, w0_ref, w2_ref, a0_ref, a2_ref, kkw_ref, ka_ref, bd_ref,
               of_ref, ob_ref, s_ref, *, n_chunks):
    c_len = RWKV_CHUNK

    @pl.when(pl.program_id(1) == 0)
    def _():
        s_ref[...] = jnp.zeros_like(s_ref)

    ri = lax.broadcasted_iota(jnp.int32, (c_len, c_len), 0)
    ci = lax.broadcasted_iota(jnp.int32, (c_len, c_len), 1)
    eye = (ri == ci).astype(F32)
    in_refs = ((rf, kf, vf, lf), (rb_, kb, vb_, lb))
    masks = []
    for d in range(2):
        before = (ri > ci) if d == 0 else (ri < ci)
        strict = before.astype(F32)
        incl = (before | (ri == ci)).astype(F32)
        m8 = jnp.where(((ri // 8) == (ci // 8)) & before, 1.0, 0.0).astype(F32)
        merges = []
        for size in (16, 32, 64):
            same = (ri // size) == (ci // size)
            inner = (ri // (size // 2)) == (ci // (size // 2))
            merges.append(jnp.where(same & jnp.logical_not(inner) & before, 1.0, 0.0).astype(F32))
        masks.append(dict(strict=strict, incl=incl, incl_b=incl.astype(BF16), m8=m8, merges=merges))
    out_refs = (of_ref, ob_ref)
    decay_scale = math.exp(-0.5)

    def chunk(c, carry):
        per_dir = []
        for d in range(2):
            cc = c if d == 0 else n_chunks - 1 - c
            sl = pl.ds(pl.multiple_of(cc * c_len, c_len), c_len)
            r_ref, k_ref, v_ref, lo_ref = in_refs[d]
            r = r_ref[sl, :]
            k = k_ref[sl, :]
            v = v_ref[sl, :]
            lo = lo_ref[sl, :]
            wlo = lo[:, d * RWKV_W_LORA:(d + 1) * RWKV_W_LORA]
            alo = lo[:, 2 * RWKV_W_LORA + d * RWKV_A_LORA:2 * RWKV_W_LORA + (d + 1) * RWKV_A_LORA]
            w_raw = w0_ref[d] + _bdot(jnp.tanh(wlo), w2_ref[d])
            lw = -decay_scale * jax.nn.sigmoid(w_raw)
            a = jax.nn.sigmoid(a0_ref[d] + _bdot(alo, a2_ref[d]))
            k_d = k * (1.0 + (a - 1.0) * ka_ref[...])
            kx = k * kkw_ref[...]
            kk = kx * lax.rsqrt(_bdot(kx * kx, bd_ref[...]) + L2_EPS)
            hi = lw.astype(BF16)
            lo2 = (lw - hi.astype(F32)).astype(BF16)
            tri = masks[d]['incl_b']
            cum = (jnp.dot(tri, hi, preferred_element_type=F32)
                   + jnp.dot(tri, lo2, preferred_element_type=F32))
            eg = jnp.exp(cum)
            ieg = jnp.exp(-cum)
            last = c_len - 1 if d == 0 else 0
            per_dir.append(dict(
                rt=(r * eg).astype(BF16), kt=(k_d * ieg).astype(BF16), at=(-(kk * a) * ieg).astype(BF16),
                bt=(kk * jnp.exp(cum - lw)).astype(BF16), vb=v.astype(BF16), g_last=eg[last:last + 1, :],
                rows=sl))
        chains = [(d, h) for d in range(2) for h in range(RWKV_H)]
        hsl = lambda h: slice(h * RWKV_HS, (h + 1) * RWKV_HS)
        rb = [jnp.concatenate([per_dir[d]['rt'][:, hsl(h)], per_dir[d]['bt'][:, hsl(h)]], axis=0)
              for d, h in chains]
        at_h = [per_dir[d]['at'][:, hsl(h)] for d, h in chains]
        kt_h = [per_dir[d]['kt'][:, hsl(h)] for d, h in chains]
        v_h = [per_dir[d]['vb'][:, hsl(h)] for d, h in chains]
        mk = [masks[d] for d, h in chains]
        n = len(chains)
        m_a = [lax.dot_general(rb[i], at_h[i], _NT, preferred_element_type=F32) for i in range(n)]
        m_k = [lax.dot_general(rb[i], kt_h[i], _NT, preferred_element_type=F32) for i in range(n)]
        ra = [m_a[i][:c_len] * mk[i]['incl'] for i in range(n)]
        nmat = [m_a[i][c_len:] * mk[i]['strict'] for i in range(n)]
        rk = [m_k[i][:c_len] * mk[i]['incl'] for i in range(n)]
        bk = [m_k[i][c_len:] * mk[i]['strict'] for i in range(n)]
        d1 = [nmat[i] * mk[i]['m8'] for i in range(n)]
        d2 = [_bdot(d1[i], d1[i]) for i in range(n)]
        d4 = [_bdot(d2[i], d2[i]) for i in range(n)]
        tinv = [eye + d1[i] for i in range(n)]
        tinv = [tinv[i] + _bdot(tinv[i], d2[i]) for i in range(n)]
        tinv = [tinv[i] + _bdot(tinv[i], d4[i]) for i in range(n)]
        for lvl in range(3):
            e = [_bdot(tinv[i], nmat[i] * mk[i]['merges'][lvl]) for i in range(n)]
            tinv = [tinv[i] + _bdot(e[i], tinv[i]) for i in range(n)]
        s0 = [s_ref[d, h] for d, h in chains]
        rbs = [lax.dot_general(rb[i], s0[i].astype(BF16), _NT, preferred_element_type=F32) for i in range(n)]
        x = [rbs[i][c_len:] + _bdot(bk[i], v_h[i]) for i in range(n)]
        u = [_bdot(tinv[i], x[i]) for i in range(n)]
        y = [rbs[i][:c_len] + _bdot(ra[i], u[i]) + _bdot(rk[i], v_h[i]) for i in range(n)]
        for i, (d, h) in enumerate(chains):
            uv = jnp.concatenate([u[i].astype(BF16), v_h[i]], axis=0)
            ak = jnp.concatenate([at_h[i], kt_h[i]], axis=0)
            ds = lax.dot_general(uv, ak, _TN, preferred_element_type=F32)
            s_ref[d, h] = (s0[i] + ds) * per_dir[d]['g_last'][:, hsl(h)]
        for d in range(2):
            out_refs[d][per_dir[d]['rows'], :] = jnp.concatenate(
                [y[d * RWKV_H + h] for h in range(RWKV_H)], axis=1)
        return carry

    lax.fori_loop(0, n_chunks, chunk, 0)


def _rwkv_mix(p_rwkv, seq, tc, lp):
    n = p_rwkv.shape[0]
    bsz = n // seq
    tb = _pick_tile(math.gcd(seq, tc), RWKV_TB, RWKV_CHUNK)
    nblk, cblk = seq // tb, tc // tb
    assert LORA_OFF % LORA_W == 0

    def fwd_blk(t):
        return t

    def bwd_blk(t):
        return jnp.where(t < cblk, cblk - 1 - t, nblk - 1 - (t - cblk))

    def col_spec(width, cb, blk_fn):
        return pl.BlockSpec((tb, width), lambda b, t: (b * nblk + blk_fn(t), cb))

    in_specs = []
    for blk_fn in (fwd_blk, bwd_blk):
        in_specs += [col_spec(BR_W, 0, blk_fn), col_spec(BR_W, 1, blk_fn), col_spec(BR_W, 2, blk_fn),
                     col_spec(LORA_W, LORA_OFF // LORA_W, blk_fn)]
    whole = lambda a: pl.BlockSpec(a.shape, lambda b, t: (0,) * a.ndim)
    head_id = jnp.arange(BR_W) // RWKV_HS
    bd = (head_id[:, None] == head_id[None, :]).astype(BF16)
    consts = [lp['rwkv_w0'][:, None, :], lp['rwkv_w2'].astype(BF16), lp['rwkv_a0'][:, None, :],
              lp['rwkv_a2'].astype(BF16), lp['rwkv_k_k'][None, :], lp['rwkv_k_a'][None, :], bd]
    in_specs += [whole(a) for a in consts]
    return pl.pallas_call(
        functools.partial(_rwkv_body, n_chunks=tb // RWKV_CHUNK),
        grid=(bsz, nblk),
        in_specs=in_specs,
        out_specs=[col_spec(BR_W, 0, fwd_blk), col_spec(BR_W, 0, bwd_blk)],
        out_shape=[jax.ShapeDtypeStruct((n, BR_W), F32)] * 2,
        scratch_shapes=[pltpu.VMEM((2, RWKV_H, RWKV_HS, RWKV_HS), F32)],
        compiler_params=pltpu.CompilerParams(
            dimension_semantics=("parallel", "arbitrary"), vmem_limit_bytes=VMEM_LIMIT),
    )(*([p_rwkv] * 8), *consts)


def _rwkv_out_proj(y_f, y_b, p_rwkv, lp, w_proj, tm):
    head_id = jnp.arange(BR_W) // RWKV_HS
    bd = (head_id[:, None] == head_id[None, :]).astype(BF16)

    def hsum(xv, bdm):
        hi = xv.astype(BF16)
        lo = (xv - hi.astype(F32)).astype(BF16)
        return (jnp.dot(hi, bdm, preferred_element_type=F32) + jnp.dot(lo, bdm, preferred_element_type=F32))

    def act(i, yf, yb, r, k, v, glo, gn_g, gn_b, r_k, g2, bdm):
        inv = 1.0 / RWKV_HS
        yb = yf + yb
        yc = yb - hsum(yb, bdm) * inv
        var = hsum(yc * yc, bdm) * inv
        yn = yc * lax.rsqrt(var + RWKV_GN_EPS) * gn_g + gn_b
        bonus = hsum(r * k * r_k, bdm) * v
        g = _bdot(jax.nn.sigmoid(glo), g2)
        return (yn + bonus) * g

    glo_cb = (LORA_OFF + LORA_W) // RWKV_G_LORA
    assert (LORA_OFF + LORA_W) % RWKV_G_LORA == 0
    return _fused_mm(
        [y_f, y_b, (p_rwkv, BR_W, 0), (p_rwkv, BR_W, 1), (p_rwkv, BR_W, 2), (p_rwkv, RWKV_G_LORA, glo_cb)],
        [_whole(lp['rwkv_gn_g'][None, :]), _whole(lp['rwkv_gn_b'][None, :]), _whole(lp['rwkv_r_k'][None, :]),
         _whole(lp['rwkv_g2'].astype(BF16)), _whole(bd)],
        w_proj.astype(BF16), act, tm=tm, tn=w_proj.shape[1])


LRU_TB = 256
LRU_HALO = 8


def _lru_body(x_ref, g_ref, cw_ref, cb_ref, wr_ref, br_ref, wi_ref, bi_ref, sp_ref, o_ref, h_ref, *,
              tb, nblk, cblk):
    seq = x_ref.shape[0]
    row = lax.broadcasted_iota(jnp.int32, (tb, 1), 0)
    n_ext = tb + LRU_HALO
    for d in range(2):
        h_ref[...] = jnp.zeros_like(h_ref)

        def blk_body(i, carry, d=d):
            if d == 0:
                blk = i
                halo_ok = (blk != 0) & (blk != cblk)
                halo0 = jnp.maximum(blk * tb - LRU_HALO, 0)
            else:
                blk = jnp.where(i < cblk, cblk - 1 - i, nblk - 1 - (i - cblk))
                halo_ok = (blk != cblk - 1) & (blk != nblk - 1)
                halo0 = jnp.minimum(blk * tb + tb, seq - LRU_HALO)
            rows = pl.ds(pl.multiple_of(blk * tb, tb), tb)
            cur = x_ref[rows, :]
            halo = x_ref[pl.ds(pl.multiple_of(halo0, LRU_HALO), LRU_HALO), :] * jnp.where(halo_ok, 1.0, 0.0)
            xc = cb_ref[d]
            if d == 0:
                ext = jnp.concatenate([halo, cur], axis=0)
                for j in range(LRU_CONV):
                    sh = LRU_CONV - 1 - j
                    tap = ext if sh == 0 else pltpu.roll(ext, sh, 0)
                    xc = xc + cw_ref[d, j:j + 1, :] * tap[LRU_HALO:, :]
            else:
                ext = jnp.concatenate([cur, halo], axis=0)
                for j in range(LRU_CONV):
                    tap = ext if j == 0 else pltpu.roll(ext, n_ext - j, 0)
                    xc = xc + cw_ref[d, j:j + 1, :] * tap[:tb, :]
            gate_r = jax.nn.sigmoid(_bdot(xc, wr_ref[d]) + br_ref[d])
            gate_i = jax.nn.sigmoid(_bdot(xc, wi_ref[d]) + bi_ref[d])
            log_a = -LRU_C * gate_r * sp_ref[d]
            a_cum = jnp.exp(log_a)
            b_cum = jnp.sqrt(1.0 - jnp.exp(2.0 * log_a)) * (gate_i * xc)
            s = 1
            while s < tb:
                if d == 0:
                    ok = row >= s
                    a_sh = jnp.where(ok, pltpu.roll(a_cum, s, 0), 1.0)
                    b_sh = jnp.where(ok, pltpu.roll(b_cum, s, 0), 0.0)
                else:
                    ok = row < tb - s
                    a_sh = jnp.where(ok, pltpu.roll(a_cum, tb - s, 0), 1.0)
                    b_sh = jnp.where(ok, pltpu.roll(b_cum, tb - s, 0), 0.0)
                b_cum = a_cum * b_sh + b_cum
                a_cum = a_cum * a_sh
                s *= 2
            h = b_cum + a_cum * h_ref[...]
            if d == 0:
                h_ref[...] = h[tb - 1:tb, :]
                o_ref[rows, :] = h
            else:
                h_ref[...] = h[0:1, :]
                o_ref[rows, :] = (o_ref[rows, :] + h) * jax.nn.gelu(g_ref[rows, :])
            return carry

        lax.fori_loop(0, nblk, blk_body, 0)


def _lru_mix(p_mid, seq, tc, lp):
    n = p_mid.shape[0]
    tb = _pick_tile(math.gcd(seq, tc), LRU_TB)
    nblk, cblk = seq // tb, tc // tb
    eye = jnp.eye(LRU_BLOCKS, dtype=F32)
    blockdiag = lambda w: jnp.einsum('dgij,gh->dgihj', w, eye).reshape(2, BR_W, BR_W).astype(BF16)
    consts = [lp['lru_conv_w'], lp['lru_conv_b'][:, None, :], blockdiag(lp['lru_wr']), lp['lru_br'][:, None, :],
              blockdiag(lp['lru_wi']), lp['lru_bi'][:, None, :], jax.nn.softplus(-lp['lru_lambda'])[:, None, :]]
    whole = lambda a: pl.BlockSpec(a.shape, lambda b: (0,) * a.ndim)
    x_cb = [name for name, _ in REST_COLS].index('lru_x')
    g_cb = [name for name, _ in REST_COLS].index('lru_gate')
    assert all(w == BR_W for _, w in REST_COLS[:max(x_cb, g_cb) + 1])
    return pl.pallas_call(
        functools.partial(_lru_body, tb=tb, nblk=nblk, cblk=cblk),
        grid=(n // seq,),
        in_specs=[pl.BlockSpec((seq, BR_W), lambda b: (b, x_cb)), pl.BlockSpec((seq, BR_W), lambda b: (b, g_cb))]
        + [whole(a) for a in consts],
        out_specs=pl.BlockSpec((seq, BR_W), lambda b: (b, 0)),
        out_shape=jax.ShapeDtypeStruct((n, BR_W), F32),
        scratch_shapes=[pltpu.VMEM((1, BR_W), F32)],
        compiler_params=pltpu.CompilerParams(dimension_semantics=("parallel",), vmem_limit_bytes=VMEM_LIMIT),
    )(p_mid, p_mid, *consts)


def _moe_body(be_ref, nu_ref, x_ref, w1_ref, w3_ref, w2_ref, o_ref):
    i = pl.program_id(0)

    @pl.when(i < nu_ref[0])
    def _():
        xb = x_ref[...]
        h1 = jnp.dot(xb, w1_ref[0], preferred_element_type=F32)
        h3 = jnp.dot(xb, w3_ref[0], preferred_element_type=F32)
        hid = (h1 * jax.nn.sigmoid(h1) * h3).astype(BF16)
        o_ref[...] = jnp.dot(hid, w2_ref[0], preferred_element_type=F32)

    @pl.when(i >= nu_ref[0])
    def _():
        o_ref[...] = jnp.zeros_like(o_ref)


def _moe_experts(xb, block_e, n_used, w1, w3, w2):
    n_rows, d = xb.shape
    n_blocks = n_rows // MOE_BM
    de = w1.shape[2]
    grid_spec = pltpu.PrefetchScalarGridSpec(
        num_scalar_prefetch=2,
        grid=(n_blocks,),
        in_specs=[
            pl.BlockSpec((MOE_BM, d), lambda i, be, nu: (i, 0)),
            pl.BlockSpec((1, d, de), lambda i, be, nu: (be[i], 0, 0)),
            pl.BlockSpec((1, d, de), lambda i, be, nu: (be[i], 0, 0)),
            pl.BlockSpec((1, de, d), lambda i, be, nu: (be[i], 0, 0)),
        ],
        out_specs=pl.BlockSpec((MOE_BM, d), lambda i, be, nu: (i, 0)),
    )
    return pl.pallas_call(
        _moe_body,
        grid_spec=grid_spec,
        out_shape=jax.ShapeDtypeStruct((n_rows, d), F32),
        compiler_params=pltpu.CompilerParams(dimension_semantics=("arbitrary",),
                                             vmem_limit_bytes=VMEM_LIMIT),
    )(block_e, n_used, xb, w1, w3, w2)


def _moe(tokens, w_router, b_router, w1, w3, w2):
    n_tok = tokens.shape[0]
    aff = jax.nn.sigmoid(jnp.dot(tokens, w_router, precision=lax.Precision.HIGHEST))
    sel = (aff + b_router).reshape(-1, N_GROUPS, EXPERTS_PER_GROUP)
    grp_score = jnp.sum(lax.top_k(sel, TOP_K)[0], axis=-1)
    grp = jnp.argmax(grp_score, axis=-1)
    grp_mask = jnp.arange(N_GROUPS)[None, :] == grp[:, None]
    masked = jnp.where(grp_mask[:, :, None], sel, -jnp.inf).reshape(-1, N_EXPERTS)
    _, e_idx = lax.top_k(masked, TOP_K)
    wts = jnp.take_along_axis(aff, e_idx, axis=1)
    wts = wts / jnp.sum(wts, axis=-1, keepdims=True)

    n_asg = n_tok * TOP_K
    flat_e = e_idx.reshape(-1).astype(jnp.int32)
    flat_t = jnp.repeat(jnp.arange(n_tok, dtype=jnp.int32), TOP_K)
    order = jnp.argsort(flat_e)
    se, st = flat_e[order], flat_t[order]
    counts = jnp.bincount(flat_e, length=N_EXPERTS)
    starts = jnp.cumsum(counts) - counts
    padded = (counts + MOE_BM - 1) // MOE_BM * MOE_BM
    pad_end = jnp.cumsum(padded)
    pad_start = pad_end - padded
    dest = (pad_start[se] + jnp.arange(n_asg, dtype=jnp.int32) - starts[se]).astype(jnp.int32)
    n_blocks = -(-n_asg // MOE_BM) + N_EXPERTS
    slot_tok = jnp.full((n_blocks * MOE_BM,), n_tok, jnp.int32).at[dest].set(st)
    block_e = jnp.minimum(jnp.searchsorted(pad_end, jnp.arange(n_blocks) * MOE_BM, side='right'),
                          N_EXPERTS - 1).astype(jnp.int32)
    n_used = (pad_end[-1] // MOE_BM).astype(jnp.int32).reshape(1)
    x_pad = jnp.concatenate([tokens.astype(BF16), jnp.zeros((1, tokens.shape[1]), BF16)], axis=0)
    xb = x_pad[slot_tok]
    yb = _moe_experts(xb, block_e, n_used, w1, w3, w2)
    dest_asg = jnp.zeros((n_asg,), jnp.int32).at[order].set(dest).reshape(n_tok, TOP_K)
    return wts[:, 0:1] * yb[dest_asg[:, 0]] + wts[:, 1:2] * yb[dest_asg[:, 1]]


def _split(p, layout):
    out, off = {}, 0
    for name, width in layout:
        out[name] = p[..., off:off + width]
        off += width
    return out


def _heads(z, n_heads, head_dim):
    return z.reshape(*z.shape[:2], n_heads, head_dim)


def _layernorm(x, g, b, eps):
    xc = x - jnp.mean(x, axis=-1, keepdims=True)
    return xc * lax.rsqrt(jnp.mean(xc * xc, axis=-1, keepdims=True) + eps) * g + b


def _head_norm(y, g, b, eps):
    yc = y - jnp.mean(y, axis=-1, keepdims=True)
    yn = yc * lax.rsqrt(jnp.mean(yc * yc, axis=-1, keepdims=True) + eps)
    return yn.reshape(*y.shape[:2], -1) * g + b


def _depthwise_conv(z, w, b, pad_lo, pad_hi):
    ch = z.shape[-1]
    y = lax.conv_general_dilated(z, w[:, None, :], (1,), [(pad_lo, pad_hi)],
                                 dimension_numbers=('NWC', 'WIO', 'NWC'), feature_group_count=ch)
    return y + b


def _two_way(scan_f, scan_b, ctx_f, ctx_b, lat_f, lat_b, s_zero):
    flip = lambda tup: tuple(jnp.flip(t, 1) for t in tup)
    yc_f, sc_f = scan_f(ctx_f, s_zero)
    yc_b, sc_b = scan_b(flip(ctx_b), s_zero)
    yl_f, _ = scan_f(lat_f, sc_f)
    yl_b, _ = scan_b(flip(lat_b), sc_b)
    return yc_f + jnp.flip(yc_b, 1), yl_f + jnp.flip(yl_b, 1)


def _conformer_act(cr, lp):
    u = cr['conv_val'] * jax.nn.sigmoid(cr['conv_gate'])
    u = _depthwise_conv(u, lp['conv_w'], lp['conv_b'], CONV_K // 2, CONV_K // 2)
    return jax.nn.silu(_layernorm(u, lp['conv_ln_g'], lp['conv_ln_b'], LN_EPS))


def _lru_scan_inputs(cr, lp):
    dirs = []
    for d in range(2):
        pad = (LRU_CONV - 1, 0) if d == 0 else (0, LRU_CONV - 1)
        xc = _depthwise_conv(cr['lru_x'], lp['lru_conv_w'][d], lp['lru_conv_b'][d], *pad)
        xb = _heads(xc, LRU_BLOCKS, LRU_BS)
        gate_r = jax.nn.sigmoid(jnp.einsum('btgi,gij->btgj', xb, lp['lru_wr'][d]).reshape(xc.shape)
                                + lp['lru_br'][d])
        gate_i = jax.nn.sigmoid(jnp.einsum('btgi,gij->btgj', xb, lp['lru_wi'][d]).reshape(xc.shape)
                                + lp['lru_bi'][d])
        log_a = -LRU_C * gate_r * jax.nn.softplus(-lp['lru_lambda'][d])
        dirs.append((jnp.exp(log_a), jnp.sqrt(-jnp.expm1(2.0 * log_a)) * (gate_i * xc)))
    return dirs


def _linear_scan(inputs, h0):
    a, b = inputs

    def combine(lhs, rhs):
        return lhs[0] * rhs[0], rhs[0] * lhs[1] + rhs[1]
    a_cum, h = lax.associative_scan(combine, (a, b), axis=1)
    h = h + a_cum * h0[:, None, :]
    return h, h[:, -1]


def _rotary_2d(z, rows, cols):
    n_freq = RET_DK // 4
    inv = ROPE_BASE ** (-jnp.arange(n_freq, dtype=F32) / n_freq)
    ang = jnp.concatenate([rows[:, None] * inv, cols[:, None] * inv], axis=-1)[None, :, None, :]
    cos, sin = jnp.cos(ang), jnp.sin(ang)
    z1, z2 = z[..., :RET_DK // 2], z[..., RET_DK // 2:]
    return jnp.concatenate([z1 * cos - z2 * sin, z1 * sin + z2 * cos], axis=-1)


def _ret_scan_inputs(cr, rows, cols):
    q = _heads(cr['ret_q'], RET_H, RET_DK)
    k = _heads(cr['ret_k'], RET_H, RET_DK) * (RET_DK ** -0.5)
    v = _heads(cr['ret_v'], RET_H, RET_DV)
    if rows is not None:
        q, k = _rotary_2d(q, rows, cols), _rotary_2d(k, rows, cols)
    return (q, k, v)


def _retention_chunkwise(inputs, s0, include_diag):
    q, k, v = inputs
    bsz, n_tok = q.shape[:2]
    n_chunk = n_tok // RET_CHUNK
    log_g = jnp.log1p(-jnp.exp2(-5.0 - jnp.arange(RET_H, dtype=F32)))
    idx = jnp.arange(RET_CHUNK, dtype=F32)
    diff = idx[:, None] - idx[None, :]
    keep = diff >= 0 if include_diag else diff > 0
    dmat = jnp.where(keep[None], jnp.exp(log_g[:, None, None] * jnp.maximum(diff, 0.0)[None]), 0.0)
    xi = jnp.exp(log_g[None, :] * (idx[:, None] + 1.0))
    zeta = jnp.exp(log_g[None, :] * (RET_CHUNK - 1.0 - idx)[:, None])
    g_chunk = jnp.exp(log_g * RET_CHUNK)
    chunks = lambda t: jnp.moveaxis(t.reshape(bsz, n_chunk, RET_CHUNK, *t.shape[2:]), 1, 0)

    def step(S, xs):
        qc, kc, vc = xs
        att = jnp.einsum('bihd,bjhd->bhij', qc, kc) * dmat
        o = (jnp.einsum('bhij,bjhv->bihv', att, vc)
             + jnp.einsum('bihd,bhdv->bihv', qc, S) * xi[None, :, :, None])
        S = S * g_chunk[None, :, None, None] + jnp.einsum('bjhd,bjhv->bhdv', kc * zeta[None, :, :, None], vc)
        return S, o
    s_fin, o = lax.scan(step, s0, (chunks(q), chunks(k), chunks(v)))
    return jnp.moveaxis(o, 0, 1).reshape(bsz, n_tok, RET_H, RET_DV), s_fin


def _ret_act(o, cr, lp):
    yn = _head_norm(o, lp['ret_gn_g'], lp['ret_gn_b'], LN_EPS)
    return jax.nn.silu(cr['ret_g']) * yn


def _rows2d(z):
    return z.reshape(-1, z.shape[-1])


def _mid_acts(p_mid, lp, tc, rows, cols):
    bsz = p_mid.shape[0]
    cr_c, cr_l = _split(p_mid[:, :tc], REST_COLS), _split(p_mid[:, tc:], REST_COLS)
    act_b = [_conformer_act(cr_c, lp), _conformer_act(cr_l, lp)]
    ret_f = functools.partial(_retention_chunkwise, include_diag=True)
    ret_b = functools.partial(_retention_chunkwise, include_diag=False)
    qc = _ret_scan_inputs(cr_c, None, None)
    ql = _ret_scan_inputs(cr_l, rows, cols)
    yr_c, yr_l = _two_way(ret_f, ret_b, qc, qc, ql, ql, jnp.zeros((bsz, RET_H, RET_DK, RET_DV), F32))
    act_d = [_ret_act(yr_c, cr_c, lp), _ret_act(yr_l, cr_l, lp)]
    cat = lambda pr: jnp.concatenate(pr, axis=1)
    return cat(act_b), cat(act_d)


def _merge_out(z, merge, branches, b_merge, w_out_l, g1, cg1, seq, tc, d):
    tm2 = _pick_tile(seq, 288)
    bpb = seq // tm2

    def seg_select(i, ctx_val, lat_val):
        row = (i % bpb) * tm2 + lax.broadcasted_iota(jnp.int32, (tm2, 1), 0)
        return jnp.where(row < tc, ctx_val, lat_val)

    def merge_pro(i, mg, b0, b1, b2, b3, bm):
        gates = jax.nn.sigmoid(mg + bm)
        return (gates[:, 0 * d:1 * d] * b0 + gates[:, 1 * d:2 * d] * b1
                + gates[:, 2 * d:3 * d] * b2 + gates[:, 3 * d:4 * d] * b3)

    def resid_epi(i, acc, zb, gl, gc):
        return zb + seg_select(i, gc, gl[0]) * acc

    tn = d // 2
    return _fused_mm([merge] + branches, [_whole(b_merge[None, :])],
                     w_out_l.astype(BF16), merge_pro, tm=tm2, tn=tn,
                     epilogue=resid_epi, epi_rows=[z],
                     epi_aux=[(g1, (1, 1, tn), lambda i, j: (i // bpb, 0, j)),
                              (cg1, (1, tn), lambda i, j: (0, j))])


def kernel(x, c, ctx, c_ctx, w_ada, b_ada, norm1_g, norm2_g, w_in, b_merge, rwkv_mu_prev, rwkv_mu_next, rwkv_w0, rwkv_w2, rwkv_a0, rwkv_a2, rwkv_g2, rwkv_k_k, rwkv_k_a, rwkv_r_k, rwkv_gn_g, rwkv_gn_b, rwkv_proj, conv_w, conv_b, conv_ln_g, conv_ln_b, conv_proj, lru_conv_w, lru_conv_b, lru_wr, lru_br, lru_wi, lru_bi, lru_lambda, lru_proj, ret_gn_g, ret_gn_b, ret_proj, w_out, w_router, b_router, e_w1, e_w3, e_w2, final_g):
    bsz, n_tok, d = x.shape
    tc = ctx.shape[1]
    seq = tc + n_tok
    depth = w_in.shape[0]
    n_rows = n_tok // GRID_W
    rows = jnp.repeat(jnp.arange(n_rows, dtype=F32), GRID_W)
    cols = jnp.tile(jnp.arange(GRID_W, dtype=F32), n_rows)
    per_layer = {
        'rwkv_w0': rwkv_w0, 'rwkv_w2': rwkv_w2, 'rwkv_a0': rwkv_a0, 'rwkv_a2': rwkv_a2, 'rwkv_g2': rwkv_g2,
        'rwkv_k_k': rwkv_k_k, 'rwkv_k_a': rwkv_k_a, 'rwkv_r_k': rwkv_r_k,
        'rwkv_gn_g': rwkv_gn_g, 'rwkv_gn_b': rwkv_gn_b,
        'conv_w': conv_w, 'conv_b': conv_b, 'conv_ln_g': conv_ln_g, 'conv_ln_b': conv_ln_b,
        'lru_conv_w': lru_conv_w, 'lru_conv_b': lru_conv_b, 'lru_wr': lru_wr, 'lru_br': lru_br,
        'lru_wi': lru_wi, 'lru_bi': lru_bi, 'lru_lambda': lru_lambda,
        'ret_gn_g': ret_gn_g, 'ret_gn_b': ret_gn_b,
    }
    tm = _pick_tile(seq, 576)

    z = jnp.concatenate([ctx, x], axis=1).reshape(bsz * seq, d)
    for layer in range(depth):
        lp = {name: arr[layer] for name, arr in per_layer.items()}
        last = layer == depth - 1
        cc = jnp.concatenate([c, c_ctx[None, :]], axis=0)
        n_mod = -(-cc.shape[0] // 8) * 8
        cc = jnp.pad(cc, ((0, n_mod - cc.shape[0]), (0, 0)))
        mod = _fused_mm([cc], [], w_ada[layer].astype(BF16), lambda i, cb: cb * jax.nn.sigmoid(cb),
                        tm=n_mod, tn=6 * d // 4,
                        epilogue=lambda i, acc, bb: acc + bb,
                        epi_aux=[(b_ada[layer][None, :], (1, 6 * d // 4), lambda i, j: (0, j))])
        mod_l = mod[:bsz].reshape(bsz, 1, 6 * d)
        mod_c = mod[bsz:bsz + 1]
        sh1, sc1, g1, sh2, sc2, g2 = [mod_l[:, :, n * d:(n + 1) * d] for n in range(6)]
        csh1, csc1, cg1, csh2, csc2, cg2 = [mod_c[:, n * d:(n + 1) * d] for n in range(6)]

        w_l = w_in[layer].astype(BF16)
        norm_args = (norm1_g[layer][None, :], sh1, sc1, csh1, csc1)
        p_rwkv = _in_proj(z, seq, tc, *norm_args, w_l[:, :RWKV_WIDTH], 640,
                          rwkv_mu_prev[layer][None, :], rwkv_mu_next[layer][None, :])
        p_mid = _in_proj(z, seq, tc, *norm_args, w_l[:, RWKV_WIDTH:RWKV_WIDTH + MID_WIDTH], 512)
        p_merge = _in_proj(z, seq, tc, *norm_args, w_l[:, RWKV_WIDTH + MID_WIDTH:], 512)

        y_f, y_b = _rwkv_mix(p_rwkv, seq, tc, lp)
        act_b, act_d = _mid_acts(p_mid.reshape(bsz, seq, MID_WIDTH), lp, tc, rows, cols)
        act_c = _lru_mix(p_mid, seq, tc, lp)
        branches = [_rwkv_out_proj(y_f, y_b, p_rwkv, lp, rwkv_proj[layer], tm)]
        branches += [
            _plain_mm(_rows2d(act), w.astype(BF16), tm=tm, tn=d)
            for act, w in zip((act_b, act_c, act_d), (conv_proj[layer], lru_proj[layer], ret_proj[layer]))
        ]
        z = _merge_out(z, p_merge, branches, b_merge[layer], w_out[layer], g1, cg1, seq, tc, d)

        z3 = z.reshape(bsz, seq, d)
        zn = z3 * lax.rsqrt(jnp.mean(z3 * z3, axis=-1, keepdims=True) + RMS_EPS) * norm2_g[layer]
        is_ctx = (jnp.arange(seq) < tc)[None, :, None]
        h2 = zn * (1.0 + jnp.where(is_ctx, csc2[None], sc2)) + jnp.where(is_ctx, csh2[None], sh2)
        gate2 = jnp.where(is_ctx, cg2[None], g2)
        w1b, w3b, w2b = e_w1[layer].astype(BF16), e_w3[layer].astype(BF16), e_w2[layer].astype(BF16)
        if not last:
            y = _moe(h2.reshape(-1, d), w_router, b_router, w1b, w3b, w2b).reshape(bsz, seq, d)
            z = (z3 + gate2 * y).reshape(bsz * seq, d)
        else:
            y = _moe(h2[:, tc:].reshape(-1, d), w_router, b_router, w1b, w3b, w2b).reshape(bsz, n_tok, d)
            xl = z3[:, tc:] + g2 * y
            return xl * lax.rsqrt(jnp.mean(xl * xl, axis=-1, keepdims=True) + RMS_EPS) * final_g
    return None
```

```python
import functools
import math

import jax
import jax.numpy as jnp
from jax import lax
from jax.experimental import pallas as pl
from jax.experimental.pallas import tpu as pltpu

F32 = jnp.float32
BF16 = jnp.bfloat16

D_MODEL = 1024
GRID_W = 64
N_BRANCH = 4
BR_W = D_MODEL // 2
RWKV_HS = 64
RWKV_H = BR_W // RWKV_HS
RWKV_W_LORA = 64
RWKV_A_LORA = 64
RWKV_G_LORA = 128
RWKV_GN_EPS = 64e-5
L2_EPS = 1e-12
CONV_K = 31
LRU_BLOCKS = 8
LRU_BS = BR_W // LRU_BLOCKS
LRU_CONV = 4
LRU_C = 8.0
RET_H = 4
RET_DK = 64
RET_DV = BR_W // RET_H
RET_CHUNK = 128
ROPE_BASE = 10000.0
N_EXPERTS = 32
N_GROUPS = 4
EXPERTS_PER_GROUP = N_EXPERTS // N_GROUPS
TOP_K = 2
D_EXPERT = D_MODEL // 2
RMS_EPS = 1e-6
LN_EPS = 1e-5

REST_COLS = (('conv_val', BR_W), ('conv_gate', BR_W), ('lru_x', BR_W), ('lru_gate', BR_W),
             ('ret_q', RET_H * RET_DK), ('ret_k', RET_H * RET_DK), ('ret_v', BR_W), ('ret_g', BR_W))
RWKV_WIDTH = 3 * BR_W + 2 * RWKV_W_LORA + 2 * RWKV_A_LORA + RWKV_G_LORA
LORA_OFF = 3 * BR_W
LORA_W = 2 * RWKV_W_LORA + 2 * RWKV_A_LORA
MID_WIDTH = sum(w for _, w in REST_COLS)
MERGE_WIDTH = N_BRANCH * D_MODEL

VMEM_LIMIT = 48 * 1024 * 1024
RWKV_CHUNK = 64
RWKV_TB = 256
MOE_BM = 256


def _bdot(a, b):
    return jnp.dot(a.astype(BF16), b.astype(BF16), preferred_element_type=F32)


_NT = (((1,), (1,)), ((), ()))
_TN = (((0,), (0,)), ((), ()))


def _pick_tile(n, cap, mult=8):
    best = None
    for t in range(mult, cap + 1, mult):
        if n % t == 0:
            best = t
    assert best is not None, (n, cap, mult)
    return best


def _fused_mm_body(*refs, nr, na, ner, nea, prologue, epilogue, out_dtype):
    rows = refs[:nr]
    auxs = refs[nr:nr + na]
    w_ref = refs[nr + na]
    er = refs[nr + na + 1:nr + na + 1 + ner]
    ea = refs[nr + na + 1 + ner:nr + na + 1 + ner + nea]
    o_ref, act = refs[-2], refs[-1]
    i = pl.program_id(0)

    @pl.when(pl.program_id(1) == 0)
    def _():
        act[...] = prologue(i, *[r[...] for r in rows], *[a[...] for a in auxs]).astype(BF16)

    acc = jnp.dot(act[...], w_ref[...], preferred_element_type=F32)
    if epilogue is not None:
        acc = epilogue(i, acc, *[r[...] for r in er], *[a[...] for a in ea])
    o_ref[...] = acc.astype(out_dtype)


def _fused_mm(row_ins, aux_ins, w, prologue, *, tm, tn, out_dtype=F32, epilogue=None,
              epi_rows=(), epi_aux=()):
    row_ins = [r if isinstance(r, tuple) else (r, r.shape[1], 0) for r in row_ins]
    n = row_ins[0][0].shape[0]
    k, m = w.shape
    assert n % tm == 0 and m % tn == 0, (n, tm, m, tn)
    in_specs = [pl.BlockSpec((tm, bw), functools.partial(lambda i, j, cb: (i, cb), cb=cb))
                for (_, bw, cb) in row_ins]
    in_specs += [pl.BlockSpec(bs, im) for (_, bs, im) in aux_ins]
    in_specs += [pl.BlockSpec((k, tn), lambda i, j: (0, j))]
    in_specs += [pl.BlockSpec((tm, tn), lambda i, j: (i, j)) for _ in epi_rows]
    in_specs += [pl.BlockSpec(bs, im) for (_, bs, im) in epi_aux]
    body = functools.partial(_fused_mm_body, nr=len(row_ins), na=len(aux_ins), ner=len(epi_rows),
                             nea=len(epi_aux), prologue=prologue, epilogue=epilogue, out_dtype=out_dtype)
    return pl.pallas_call(
        body,
        grid=(n // tm, m // tn),
        in_specs=in_specs,
        out_specs=pl.BlockSpec((tm, tn), lambda i, j: (i, j)),
        out_shape=jax.ShapeDtypeStruct((n, m), out_dtype),
        scratch_shapes=[pltpu.VMEM((tm, k), BF16)],
        compiler_params=pltpu.CompilerParams(dimension_semantics=("parallel", "arbitrary"),
                                             vmem_limit_bytes=VMEM_LIMIT),
    )(*[a for (a, _, _) in row_ins], *[a for (a, _, _) in aux_ins], w, *epi_rows,
      *[a for (a, _, _) in epi_aux])


def _whole(a):
    nd = a.ndim
    return (a, a.shape, lambda i, j: (0,) * nd)


def _plain_mm(x, w, *, tm, tn, out_dtype=F32):
    return _fused_mm([x], [], w, lambda i, xb: xb, tm=tm, tn=tn, out_dtype=out_dtype)


def _in_proj_body(*refs, tc, shift):
    if shift:
        z_ref, g_ref, shl_ref, scl_ref, shc_ref, scc_ref, w_ref, mup_ref, mun_ref, o_ref, act = refs
    else:
        z_ref, g_ref, shl_ref, scl_ref, shc_ref, scc_ref, w_ref, o_ref, act = refs
    seq = z_ref.shape[0]
    row = lax.broadcasted_iota(jnp.int32, (seq, 1), 0)

    @pl.when(pl.program_id(1) == 0)
    def _():
        xb = z_ref[...]
        y = xb * lax.rsqrt(jnp.mean(xb * xb, axis=-1, keepdims=True) + RMS_EPS) * g_ref[...]
        is_ctx = row < tc
        sc = jnp.where(is_ctx, scc_ref[...], scl_ref[0])
        sh = jnp.where(is_ctx, shc_ref[...], shl_ref[0])
        act[...] = (y * (1.0 + sc) + sh).astype(BF16)

    acc = jnp.dot(act[...], w_ref[...], preferred_element_type=F32)
    if shift:
        prev = pltpu.roll(acc, 1, 0)
        nxt = pltpu.roll(acc, seq - 1, 0)
        prev = jnp.where((row == 0) | (row == tc), 0.0, prev)
        nxt = jnp.where((row == tc - 1) | (row == seq - 1), 0.0, nxt)
        acc = acc + mup_ref[...] * (prev - acc) + mun_ref[...] * (nxt - acc)
    o_ref[...] = acc


def _in_proj(z, seq, tc, g, sh_l, sc_l, sh_c, sc_c, w, tn, mu_prev=None, mu_next=None):
    n, d = z.shape
    m = w.shape[1]
    shift = mu_prev is not None
    assert m % tn == 0
    vec = lambda width: pl.BlockSpec((1, width), lambda b, j: (0, 0))
    per_b = pl.BlockSpec((1, 1, d), lambda b, j: (b, 0, 0))
    in_specs = [pl.BlockSpec((seq, d), lambda b, j: (b, 0)), vec(d), per_b, per_b, vec(d), vec(d),
                pl.BlockSpec((d, tn), lambda b, j: (0, j))]
    args = [z, g, sh_l, sc_l, sh_c, sc_c, w]
    if shift:
        in_specs += [pl.BlockSpec((1, tn), lambda b, j: (0, j))] * 2
        args += [mu_prev, mu_next]
    return pl.pallas_call(
        functools.partial(_in_proj_body, tc=tc, shift=shift),
        grid=(n // seq, m // tn),
        in_specs=in_specs,
        out_specs=pl.BlockSpec((seq, tn), lambda b, j: (b, j)),
        out_shape=jax.ShapeDtypeStruct((n, m), F32),
        scratch_shapes=[pltpu.VMEM((seq, d), BF16)],
        compiler_params=pltpu.CompilerParams(dimension_semantics=("parallel", "arbitrary"),
                                             vmem_limit_bytes=VMEM_LIMIT),
    )(*args)


def _rwkv_body(rf, kf, vf, lf, rb_, kb, vb_, lb, w0_ref, w2_ref, a0_ref, a2_ref, kkw_ref, ka_ref, bd_ref,
               of_ref, ob_ref, s_ref, *, n_chunks):
    c_len = RWKV_CHUNK

    @pl.when(pl.program_id(1) == 0)
    def _():
        s_ref[...] = jnp.zeros_like(s_ref)

    ri = lax.broadcasted_iota(jnp.int32, (c_len, c_len), 0)
    ci = lax.broadcasted_iota(jnp.int32, (c_len, c_len), 1)
    eye = (ri == ci).astype(F32)
    in_refs = ((rf, kf, vf, lf), (rb_, kb, vb_, lb))
    masks = []
    for d in range(2):
        before = (ri > ci) if d == 0 else (ri < ci)
        strict = before.astype(F32)
        incl = (before | (ri == ci)).astype(F32)
        m8 = jnp.where(((ri // 8) == (ci // 8)) & before, 1.0, 0.0).astype(F32)
        merges = []
        for size in (16, 32, 64):
            same = (ri // size) == (ci // size)
            inner = (ri // (size // 2)) == (ci // (size // 2))
            merges.append(jnp.where(same & jnp.logical_not(inner) & before, 1.0, 0.0).astype(F32))
        masks.append(dict(strict=strict, incl=incl, incl_b=incl.astype(BF16), m8=m8, merges=merges))
    out_refs = (of_ref, ob_ref)
    decay_scale = math.exp(-0.5)

    def chunk(c, carry):
        per_dir = []
        for d in range(2):
            cc = c if d == 0 else n_chunks - 1 - c
            sl = pl.ds(pl.multiple_of(cc * c_len, c_len), c_len)
            r_ref, k_ref, v_ref, lo_ref = in_refs[d]
            r = r_ref[sl, :]
            k = k_ref[sl, :]
            v = v_ref[sl, :]
            lo = lo_ref[sl, :]
            wlo = lo[:, d * RWKV_W_LORA:(d + 1) * RWKV_W_LORA]
            alo = lo[:, 2 * RWKV_W_LORA + d * RWKV_A_LORA:2 * RWKV_W_LORA + (d + 1) * RWKV_A_LORA]
            w_raw = w0_ref[d] + _bdot(jnp.tanh(wlo), w2_ref[d])
            lw = -decay_scale * jax.nn.sigmoid(w_raw)
            a = jax.nn.sigmoid(a0_ref[d] + _bdot(alo, a2_ref[d]))
            k_d = k * (1.0 + (a - 1.0) * ka_ref[...])
            kx = k * kkw_ref[...]
            kk = kx * lax.rsqrt(_bdot(kx * kx, bd_ref[...]) + L2_EPS)
            hi = lw.astype(BF16)
            lo2 = (lw - hi.astype(F32)).astype(BF16)
            tri = masks[d]['incl_b']
            cum = (jnp.dot(tri, hi, preferred_element_type=F32)
                   + jnp.dot(tri, lo2, preferred_element_type=F32))
            eg = jnp.exp(cum)
            ieg = jnp.exp(-cum)
            last = c_len - 1 if d == 0 else 0
            per_dir.append(dict(
                rt=(r * eg).astype(BF16), kt=(k_d * ieg).astype(BF16), at=(-(kk * a) * ieg).astype(BF16),
                bt=(kk * jnp.exp(cum - lw)).astype(BF16), vb=v.astype(BF16), g_last=eg[last:last + 1, :],
                rows=sl))
        chains = [(d, h) for d in range(2) for h in range(RWKV_H)]
        hsl = lambda h: slice(h * RWKV_HS, (h + 1) * RWKV_HS)
        rb = [jnp.concatenate([per_dir[d]['rt'][:, hsl(h)], per_dir[d]['bt'][:, hsl(h)]], axis=0)
              for d, h in chains]
        at_h = [per_dir[d]['at'][:, hsl(h)] for d, h in chains]
        kt_h = [per_dir[d]['kt'][:, hsl(h)] for d, h in chains]
        v_h = [per_dir[d]['vb'][:, hsl(h)] for d, h in chains]
        mk = [masks[d] for d, h in chains]
        n = len(chains)
        m_a = [lax.dot_general(rb[i], at_h[i], _NT, preferred_element_type=F32) for i in range(n)]
        m_k = [lax.dot_general(rb[i], kt_h[i], _NT, preferred_element_type=F32) for i in range(n)]
        ra = [m_a[i][:c_len] * mk[i]['incl'] for i in range(n)]
        nmat = [m_a[i][c_len:] * mk[i]['strict'] for i in range(n)]
        rk = [m_k[i][:c_len] * mk[i]['incl'] for i in range(n)]
        bk = [m_k[i][c_len:] * mk[i]['strict'] for i in range(n)]
        d1 = [nmat[i] * mk[i]['m8'] for i in range(n)]
        d2 = [_bdot(d1[i], d1[i]) for i in range(n)]
        d4 = [_bdot(d2[i], d2[i]) for i in range(n)]
        tinv = [eye + d1[i] for i in range(n)]
        tinv = [tinv[i] + _bdot(tinv[i], d2[i]) for i in range(n)]
        tinv = [tinv[i] + _bdot(tinv[i], d4[i]) for i in range(n)]
        for lvl in range(3):
            e = [_bdot(tinv[i], nmat[i] * mk[i]['merges'][lvl]) for i in range(n)]
            tinv = [tinv[i] + _bdot(e[i], tinv[i]) for i in range(n)]
        s0 = [s_ref[d, h] for d, h in chains]
        rbs = [lax.dot_general(rb[i], s0[i].astype(BF16), _NT, preferred_element_type=F32) for i in range(n)]
        x = [rbs[i][c_len:] + _bdot(bk[i], v_h[i]) for i in range(n)]
        u = [_bdot(tinv[i], x[i]) for i in range(n)]
        y = [rbs[i][:c_len] + _bdot(ra[i], u[i]) + _bdot(rk[i], v_h[i]) for i in range(n)]
        for i, (d, h) in enumerate(chains):
            uv = jnp.concatenate([u[i].astype(BF16), v_h[i]], axis=0)
            ak = jnp.concatenate([at_h[i], kt_h[i]], axis=0)
            ds = lax.dot_general(uv, ak, _TN, preferred_element_type=F32)
            s_ref[d, h] = (s0[i] + ds) * per_dir[d]['g_last'][:, hsl(h)]
        for d in range(2):
            out_refs[d][per_dir[d]['rows'], :] = jnp.concatenate(
                [y[d * RWKV_H + h] for h in range(RWKV_H)], axis=1)
        return carry

    lax.fori_loop(0, n_chunks, chunk, 0)


def _rwkv_mix(p_rwkv, seq, tc, lp):
    n = p_rwkv.shape[0]
    bsz = n // seq
    tb = _pick_tile(math.gcd(seq, tc), RWKV_TB, RWKV_CHUNK)
    nblk, cblk = seq // tb, tc // tb
    assert LORA_OFF % LORA_W == 0

    def fwd_blk(t):
        return t

    def bwd_blk(t):
        return jnp.where(t < cblk, cblk - 1 - t, nblk - 1 - (t - cblk))

    def col_spec(width, cb, blk_fn):
        return pl.BlockSpec((tb, width), lambda b, t: (b * nblk + blk_fn(t), cb))

    in_specs = []
    for blk_fn in (fwd_blk, bwd_blk):
        in_specs += [col_spec(BR_W, 0, blk_fn), col_spec(BR_W, 1, blk_fn), col_spec(BR_W, 2, blk_fn),
                     col_spec(LORA_W, LORA_OFF // LORA_W, blk_fn)]
    whole = lambda a: pl.BlockSpec(a.shape, lambda b, t: (0,) * a.ndim)
    head_id = jnp.arange(BR_W) // RWKV_HS
    bd = (head_id[:, None] == head_id[None, :]).astype(BF16)
    consts = [lp['rwkv_w0'][:, None, :], lp['rwkv_w2'].astype(BF16), lp['rwkv_a0'][:, None, :],
              lp['rwkv_a2'].astype(BF16), lp['rwkv_k_k'][None, :], lp['rwkv_k_a'][None, :], bd]
    in_specs += [whole(a) for a in consts]
    return pl.pallas_call(
        functools.partial(_rwkv_body, n_chunks=tb // RWKV_CHUNK),
        grid=(bsz, nblk),
        in_specs=in_specs,
        out_specs=[col_spec(BR_W, 0, fwd_blk), col_spec(BR_W, 0, bwd_blk)],
        out_shape=[jax.ShapeDtypeStruct((n, BR_W), F32)] * 2,
        scratch_shapes=[pltpu.VMEM((2, RWKV_H, RWKV_HS, RWKV_HS), F32)],
        compiler_params=pltpu.CompilerParams(
            dimension_semantics=("parallel", "arbitrary"), vmem_limit_bytes=VMEM_LIMIT),
    )(*([p_rwkv] * 8), *consts)


def _rwkv_out_proj(y_f, y_b, p_rwkv, lp, w_proj, tm):
    head_id = jnp.arange(BR_W) // RWKV_HS
    bd = (head_id[:, None] == head_id[None, :]).astype(BF16)

    def hsum(xv, bdm):
        hi = xv.astype(BF16)
        lo = (xv - hi.astype(F32)).astype(BF16)
        return (jnp.dot(hi, bdm, preferred_element_type=F32) + jnp.dot(lo, bdm, preferred_element_type=F32))

    def act(i, yf, yb, r, k, v, glo, gn_g, gn_b, r_k, g2, bdm):
        inv = 1.0 / RWKV_HS
        yb = yf + yb
        yc = yb - hsum(yb, bdm) * inv
        var = hsum(yc * yc, bdm) * inv
        yn = yc * lax.rsqrt(var + RWKV_GN_EPS) * gn_g + gn_b
        bonus = hsum(r * k * r_k, bdm) * v
        g = _bdot(jax.nn.sigmoid(glo), g2)
        return (yn + bonus) * g

    glo_cb = (LORA_OFF + LORA_W) // RWKV_G_LORA
    assert (LORA_OFF + LORA_W) % RWKV_G_LORA == 0
    return _fused_mm(
        [y_f, y_b, (p_rwkv, BR_W, 0), (p_rwkv, BR_W, 1), (p_rwkv, BR_W, 2), (p_rwkv, RWKV_G_LORA, glo_cb)],
        [_whole(lp['rwkv_gn_g'][None, :]), _whole(lp['rwkv_gn_b'][None, :]), _whole(lp['rwkv_r_k'][None, :]),
         _whole(lp['rwkv_g2'].astype(BF16)), _whole(bd)],
        w_proj.astype(BF16), act, tm=tm, tn=w_proj.shape[1])


LRU_TB = 256
LRU_HALO = 8


def _lru_body(x_ref, g_ref, cw_ref, cb_ref, wr_ref, br_ref, wi_ref, bi_ref, sp_ref, o_ref, h_ref, *,
              tb, nblk, cblk):
    seq = x_ref.shape[0]
    row = lax.broadcasted_iota(jnp.int32, (tb, 1), 0)
    n_ext = tb + LRU_HALO
    for d in range(2):
        h_ref[...] = jnp.zeros_like(h_ref)

        def blk_body(i, carry, d=d):
            if d == 0:
                blk = i
                halo_ok = (blk != 0) & (blk != cblk)
                halo0 = jnp.maximum(blk * tb - LRU_HALO, 0)
            else:
                blk = jnp.where(i < cblk, cblk - 1 - i, nblk - 1 - (i - cblk))
                halo_ok = (blk != cblk - 1) & (blk != nblk - 1)
                halo0 = jnp.minimum(blk * tb + tb, seq - LRU_HALO)
            rows = pl.ds(pl.multiple_of(blk * tb, tb), tb)
            cur = x_ref[rows, :]
            halo = x_ref[pl.ds(pl.multiple_of(halo0, LRU_HALO), LRU_HALO), :] * jnp.where(halo_ok, 1.0, 0.0)
            xc = cb_ref[d]
            if d == 0:
                ext = jnp.concatenate([halo, cur], axis=0)
                for j in range(LRU_CONV):
                    sh = LRU_CONV - 1 - j
                    tap = ext if sh == 0 else pltpu.roll(ext, sh, 0)
                    xc = xc + cw_ref[d, j:j + 1, :] * tap[LRU_HALO:, :]
            else:
                ext = jnp.concatenate([cur, halo], axis=0)
                for j in range(LRU_CONV):
                    tap = ext if j == 0 else pltpu.roll(ext, n_ext - j, 0)
                    xc = xc + cw_ref[d, j:j + 1, :] * tap[:tb, :]
            gate_r = jax.nn.sigmoid(_bdot(xc, wr_ref[d]) + br_ref[d])
            gate_i = jax.nn.sigmoid(_bdot(xc, wi_ref[d]) + bi_ref[d])
            log_a = -LRU_C * gate_r * sp_ref[d]
            a_cum = jnp.exp(log_a)
            b_cum = jnp.sqrt(1.0 - jnp.exp(2.0 * log_a)) * (gate_i * xc)
            s = 1
            while s < tb:
                if d == 0:
                    ok = row >= s
                    a_sh = jnp.where(ok, pltpu.roll(a_cum, s, 0), 1.0)
                    b_sh = jnp.where(ok, pltpu.roll(b_cum, s, 0), 0.0)
                else:
                    ok = row < tb - s
                    a_sh = jnp.where(ok, pltpu.roll(a_cum, tb - s, 0), 1.0)
                    b_sh = jnp.where(ok, pltpu.roll(b_cum, tb - s, 0), 0.0)
                b_cum = a_cum * b_sh + b_cum
                a_cum = a_cum * a_sh
                s *= 2
            h = b_cum + a_cum * h_ref[...]
            if d == 0:
                h_ref[...] = h[tb - 1:tb, :]
                o_ref[rows, :] = h
            else:
                h_ref[...] = h[0:1, :]
                o_ref[rows, :] = (o_ref[rows, :] + h) * jax.nn.gelu(g_ref[rows, :])
            return carry

        lax.fori_loop(0, nblk, blk_body, 0)


def _lru_mix(p_mid, seq, tc, lp):
    n = p_mid.shape[0]
    tb = _pick_tile(math.gcd(seq, tc), LRU_TB)
    nblk, cblk = seq // tb, tc // tb
    eye = jnp.eye(LRU_BLOCKS, dtype=F32)
    blockdiag = lambda w: jnp.einsum('dgij,gh->dgihj', w, eye).reshape(2, BR_W, BR_W).astype(BF16)
    consts = [lp['lru_conv_w'], lp['lru_conv_b'][:, None, :], blockdiag(lp['lru_wr']), lp['lru_br'][:, None, :],
              blockdiag(lp['lru_wi']), lp['lru_bi'][:, None, :], jax.nn.softplus(-lp['lru_lambda'])[:, None, :]]
    whole = lambda a: pl.BlockSpec(a.shape, lambda b: (0,) * a.ndim)
    x_cb = [name for name, _ in REST_COLS].index('lru_x')
    g_cb = [name for name, _ in REST_COLS].index('lru_gate')
    assert all(w == BR_W for _, w in REST_COLS[:max(x_cb, g_cb) + 1])
    return pl.pallas_call(
        functools.partial(_lru_body, tb=tb, nblk=nblk, cblk=cblk),
        grid=(n // seq,),
        in_specs=[pl.BlockSpec((seq, BR_W), lambda b: (b, x_cb)), pl.BlockSpec((seq, BR_W), lambda b: (b, g_cb))]
        + [whole(a) for a in consts],
        out_specs=pl.BlockSpec((seq, BR_W), lambda b: (b, 0)),
        out_shape=jax.ShapeDtypeStruct((n, BR_W), F32),
        scratch_shapes=[pltpu.VMEM((1, BR_W), F32)],
        compiler_params=pltpu.CompilerParams(dimension_semantics=("parallel",), vmem_limit_bytes=VMEM_LIMIT),
    )(p_mid, p_mid, *consts)


CONV_TB = 256
CONV_HALO = 16


def _conv_body(val_ref, gate_ref, w_ref, b_ref, lng_ref, lnb_ref, o_ref, *, tb, nblk, cblk):
    seq = val_ref.shape[0]
    n_ext = tb + 2 * CONV_HALO
    pad = CONV_K // 2

    def glu(rows):
        return val_ref[rows, :] * jax.nn.sigmoid(gate_ref[rows, :])

    def blk_body(blk, carry):
        t0 = blk * tb
        rows = pl.ds(pl.multiple_of(t0, tb), tb)
        lo_ok = (blk != 0) & (blk != cblk)
        hi_ok = (blk != cblk - 1) & (blk != nblk - 1)
        lo0 = jnp.maximum(t0 - CONV_HALO, 0)
        hi0 = jnp.minimum(t0 + tb, seq - CONV_HALO)
        lo = glu(pl.ds(pl.multiple_of(lo0, CONV_HALO), CONV_HALO)) * jnp.where(lo_ok, 1.0, 0.0)
        hi = glu(pl.ds(pl.multiple_of(hi0, CONV_HALO), CONV_HALO)) * jnp.where(hi_ok, 1.0, 0.0)
        ext = jnp.concatenate([lo, glu(rows), hi], axis=0)
        acc = jnp.zeros((tb, ext.shape[1]), F32) + b_ref[...]
        for r in range(8):
            rolled = ext if r == 0 else pltpu.roll(ext, n_ext - r, 0)
            for j in range(CONV_K):
                off = CONV_HALO - pad + j
                if off % 8 == r:
                    acc = acc + w_ref[j:j + 1, :] * rolled[off - r:off - r + tb, :]
        mu = jnp.mean(acc, axis=-1, keepdims=True)
        xc = acc - mu
        yn = xc * lax.rsqrt(jnp.mean(xc * xc, axis=-1, keepdims=True) + LN_EPS) * lng_ref[...] + lnb_ref[...]
        o_ref[rows, :] = yn * jax.nn.sigmoid(yn)
        return carry

    lax.fori_loop(0, nblk, blk_body, 0)


def _conv_mix(p_mid, seq, tc, lp):
    n = p_mid.shape[0]
    tb = _pick_tile(math.gcd(seq, tc), CONV_TB, CONV_HALO)
    names = [name for name, _ in REST_COLS]
    v_cb, g_cb = names.index('conv_val'), names.index('conv_gate')
    assert all(w == BR_W for _, w in REST_COLS[:max(v_cb, g_cb) + 1]) and CONV_HALO >= CONV_K // 2
    consts = [lp['conv_w'], lp['conv_b'][None, :], lp['conv_ln_g'][None, :], lp['conv_ln_b'][None, :]]
    whole = lambda a: pl.BlockSpec(a.shape, lambda b: (0,) * a.ndim)
    return pl.pallas_call(
        functools.partial(_conv_body, tb=tb, nblk=seq // tb, cblk=tc // tb),
        grid=(n // seq,),
        in_specs=[pl.BlockSpec((seq, BR_W), lambda b: (b, v_cb)), pl.BlockSpec((seq, BR_W), lambda b: (b, g_cb))]
        + [whole(a) for a in consts],
        out_specs=pl.BlockSpec((seq, BR_W), lambda b: (b, 0)),
        out_shape=jax.ShapeDtypeStruct((n, BR_W), F32),
        compiler_params=pltpu.CompilerParams(dimension_semantics=("parallel",), vmem_limit_bytes=VMEM_LIMIT),
    )(p_mid, p_mid, *consts)


RET_TB = 256
_RET_LOG_G = [math.log1p(-2.0 ** (-5.0 - h)) for h in range(RET_H)]


def _ret_body(qf, kf, vf, cf, sf, qb, kb, vb, cb, sb, dm_ref, xi_ref, zt_ref, of_ref, ob_ref, s_ref, *, n_chunks):
    c_len = RET_CHUNK
    qk_w = RET_H * RET_DK

    @pl.when(pl.program_id(1) == 0)
    def _():
        s_ref[...] = jnp.zeros_like(s_ref)

    lane = lax.broadcasted_iota(jnp.int32, (c_len, qk_w), 1)
    first_half = (lane % RET_DK) < (RET_DK // 2)

    def rotary(z, cos, sin):
        swapped = jnp.where(first_half, pltpu.roll(z, qk_w - RET_DK // 2, 1), pltpu.roll(z, RET_DK // 2, 1))
        return z * cos + swapped * sin

    in_refs = ((qf, kf, vf, cf, sf), (qb, kb, vb, cb, sb))
    out_refs = (of_ref, ob_ref)

    def chunk(c, carry):
        per_dir = []
        for d in range(2):
            cc = c if d == 0 else n_chunks - 1 - c
            sl = pl.ds(pl.multiple_of(cc * c_len, c_len), c_len)
            q_ref, k_ref, v_ref, c_ref, sn_ref = in_refs[d]
            cos, sin = c_ref[sl, :], sn_ref[sl, :]
            q = rotary(q_ref[sl, :], cos, sin)
            k = rotary(k_ref[sl, :], cos, sin) * (RET_DK ** -0.5)
            per_dir.append(dict(q=q.astype(BF16), k=k.astype(BF16), kz=(k * zt_ref[d]).astype(BF16),
                                v=v_ref[sl, :].astype(BF16), sl=sl))
        groups = [(d, h) for d in range(2) for h in range(RET_H)]
        ksl = lambda h: slice(h * RET_DK, (h + 1) * RET_DK)
        vsl = lambda h: slice(h * RET_DV, (h + 1) * RET_DV)
        q_h = [per_dir[d]['q'][:, ksl(h)] for d, h in groups]
        v_h = [per_dir[d]['v'][:, vsl(h)] for d, h in groups]
        att = [lax.dot_general(q_h[i], per_dir[d]['k'][:, ksl(h)], _NT, preferred_element_type=F32) * dm_ref[d, h]
               for i, (d, h) in enumerate(groups)]
        s0 = [s_ref[d, h] for d, h in groups]
        o = [_bdot(att[i], v_h[i]) + _bdot(q_h[i], s0[i]) * xi_ref[d][:, vsl(h)]
             for i, (d, h) in enumerate(groups)]
        for i, (d, h) in enumerate(groups):
            kv = lax.dot_general(per_dir[d]['kz'][:, ksl(h)], v_h[i], _TN, preferred_element_type=F32)
            s_ref[d, h] = s0[i] * math.exp(_RET_LOG_G[h] * c_len) + kv
        for d in range(2):
            out_refs[d][per_dir[d]['sl'], :] = jnp.concatenate([o[d * RET_H + h] for h in range(RET_H)], axis=1)
        return carry

    lax.fori_loop(0, n_chunks, chunk, 0)


def _ret_mix(p_mid, seq, tc, n_tok):
    n = p_mid.shape[0]
    tb = _pick_tile(math.gcd(seq, tc), RET_TB, RET_CHUNK)
    nblk, cblk = seq // tb, tc // tb
    c_len = RET_CHUNK
    n_rows = n_tok // GRID_W
    rows = jnp.repeat(jnp.arange(n_rows, dtype=F32), GRID_W)
    cols = jnp.tile(jnp.arange(GRID_W, dtype=F32), n_rows)
    n_freq = RET_DK // 4
    inv = ROPE_BASE ** (-jnp.arange(n_freq, dtype=F32) / n_freq)
    ang = jnp.concatenate([rows[:, None] * inv, cols[:, None] * inv], axis=-1)
    cos_h = jnp.concatenate([jnp.cos(ang), jnp.cos(ang)], axis=-1)
    sin_h = jnp.concatenate([-jnp.sin(ang), jnp.sin(ang)], axis=-1)
    pad_ctx = lambda tbl, fill: jnp.concatenate(
        [jnp.full((tc, RET_H * RET_DK), fill, F32), jnp.tile(tbl, (1, RET_H))], axis=0)
    cos_t, sin_t = pad_ctx(cos_h, 1.0), pad_ctx(sin_h, 0.0)
    log_g = jnp.asarray(_RET_LOG_G, F32)
    idx = jnp.arange(c_len, dtype=F32)
    dm, xi, zt = [], [], []
    for d in range(2):
        pos = idx if d == 0 else c_len - 1.0 - idx
        diff = pos[:, None] - pos[None, :]
        keep = diff >= 0 if d == 0 else diff > 0
        dm.append(jnp.where(keep[None], jnp.exp(log_g[:, None, None] * jnp.maximum(diff, 0.0)[None]), 0.0))
        xi.append(jnp.repeat(jnp.exp(log_g[None, :] * (pos[:, None] + 1.0)), RET_DV, axis=1))
        zt.append(jnp.repeat(jnp.exp(log_g[None, :] * (c_len - 1.0 - pos)[:, None]), RET_DK, axis=1))
    consts = [jnp.stack(dm), jnp.stack(xi), jnp.stack(zt)]

    def fwd_blk(t):
        return t

    def bwd_blk(t):
        return jnp.where(t < cblk, cblk - 1 - t, nblk - 1 - (t - cblk))

    names = [name for name, _ in REST_COLS]
    offs = {name: sum(w for _, w in REST_COLS[:i]) for i, (name, _) in enumerate(REST_COLS)}
    qk_w = RET_H * RET_DK
    in_specs = []
    for blk_fn in (fwd_blk, bwd_blk):
        spec = lambda width, cb, blk_fn=blk_fn: pl.BlockSpec((tb, width), lambda b, t: (b * nblk + blk_fn(t), cb))
        tbl = lambda blk_fn=blk_fn: pl.BlockSpec((tb, qk_w), lambda b, t: (blk_fn(t), 0))
        assert offs['ret_q'] % qk_w == 0 and offs['ret_k'] % qk_w == 0 and offs['ret_v'] % BR_W == 0
        in_specs += [spec(qk_w, offs['ret_q'] // qk_w), spec(qk_w, offs['ret_k'] // qk_w),
                     spec(BR_W, offs['ret_v'] // BR_W), tbl(), tbl()]
    whole = lambda a: pl.BlockSpec(a.shape, lambda b, t: (0,) * a.ndim)
    in_specs += [whole(a) for a in consts]
    out_spec = lambda blk_fn: pl.BlockSpec((tb, BR_W), lambda b, t: (b * nblk + blk_fn(t), 0))
    return pl.pallas_call(
        functools.partial(_ret_body, n_chunks=tb // c_len),
        grid=(n // seq, nblk),
        in_specs=in_specs,
        out_specs=[out_spec(fwd_blk), out_spec(bwd_blk)],
        out_shape=[jax.ShapeDtypeStruct((n, BR_W), F32)] * 2,
        scratch_shapes=[pltpu.VMEM((2, RET_H, RET_DK, RET_DV), F32)],
        compiler_params=pltpu.CompilerParams(
            dimension_semantics=("parallel", "arbitrary"), vmem_limit_bytes=VMEM_LIMIT),
    )(*([p_mid, p_mid, p_mid, cos_t, sin_t] * 2), *consts)


def _ret_out_proj(o_f, o_b, p_mid, lp, w_proj, tm):
    def act(i, of, ob, g, gn_g, gn_b):
        o = of + ob
        normed = []
        for h in range(RET_H):
            oh = o[:, h * RET_DV:(h + 1) * RET_DV]
            oc = oh - jnp.mean(oh, axis=-1, keepdims=True)
            normed.append(oc * lax.rsqrt(jnp.mean(oc * oc, axis=-1, keepdims=True) + LN_EPS))
        yn = jnp.concatenate(normed, axis=1) * gn_g + gn_b
        return g * jax.nn.sigmoid(g) * yn

    off_g = sum(w for _, w in REST_COLS[:[name for name, _ in REST_COLS].index('ret_g')])
    assert off_g % BR_W == 0
    return _fused_mm([o_f, o_b, (p_mid, BR_W, off_g // BR_W)],
                     [_whole(lp['ret_gn_g'][None, :]), _whole(lp['ret_gn_b'][None, :])],
                     w_proj.astype(BF16), act, tm=tm, tn=w_proj.shape[1])


def _moe_body(be_ref, nu_ref, x_ref, w1_ref, w3_ref, w2_ref, o_ref):
    i = pl.program_id(0)

    @pl.when(i < nu_ref[0])
    def _():
        xb = x_ref[...]
        h1 = jnp.dot(xb, w1_ref[0], preferred_element_type=F32)
        h3 = jnp.dot(xb, w3_ref[0], preferred_element_type=F32)
        hid = (h1 * jax.nn.sigmoid(h1) * h3).astype(BF16)
        o_ref[...] = jnp.dot(hid, w2_ref[0], preferred_element_type=F32)

    @pl.when(i >= nu_ref[0])
    def _():
        o_ref[...] = jnp.zeros_like(o_ref)


def _moe_experts(xb, block_e, n_used, w1, w3, w2):
    n_rows, d = xb.shape
    n_blocks = n_rows // MOE_BM
    de = w1.shape[2]
    grid_spec = pltpu.PrefetchScalarGridSpec(
        num_scalar_prefetch=2,
        grid=(n_blocks,),
        in_specs=[
            pl.BlockSpec((MOE_BM, d), lambda i, be, nu: (i, 0)),
            pl.BlockSpec((1, d, de), lambda i, be, nu: (be[i], 0, 0)),
            pl.BlockSpec((1, d, de), lambda i, be, nu: (be[i], 0, 0)),
            pl.BlockSpec((1, de, d), lambda i, be, nu: (be[i], 0, 0)),
        ],
        out_specs=pl.BlockSpec((MOE_BM, d), lambda i, be, nu: (i, 0)),
    )
    return pl.pallas_call(
        _moe_body,
        grid_spec=grid_spec,
        out_shape=jax.ShapeDtypeStruct((n_rows, d), F32),
        compiler_params=pltpu.CompilerParams(dimension_semantics=("arbitrary",),
                                             vmem_limit_bytes=VMEM_LIMIT),
    )(block_e, n_used, xb, w1, w3, w2)


def _moe(tokens, w_router, b_router, w1, w3, w2):
    n_tok = tokens.shape[0]
    aff = jax.nn.sigmoid(jnp.dot(tokens, w_router, precision=lax.Precision.HIGHEST))
    sel = (aff + b_router).reshape(-1, N_GROUPS, EXPERTS_PER_GROUP)
    grp_score = jnp.sum(lax.top_k(sel, TOP_K)[0], axis=-1)
    grp = jnp.argmax(grp_score, axis=-1)
    grp_mask = jnp.arange(N_GROUPS)[None, :] == grp[:, None]
    masked = jnp.where(grp_mask[:, :, None], sel, -jnp.inf).reshape(-1, N_EXPERTS)
    _, e_idx = lax.top_k(masked, TOP_K)
    wts = jnp.take_along_axis(aff, e_idx, axis=1)
    wts = wts / jnp.sum(wts, axis=-1, keepdims=True)

    n_asg = n_tok * TOP_K
    flat_e = e_idx.reshape(-1).astype(jnp.int32)
    flat_t = jnp.repeat(jnp.arange(n_tok, dtype=jnp.int32), TOP_K)
    order = jnp.argsort(flat_e)
    se, st = flat_e[order], flat_t[order]
    counts = jnp.bincount(flat_e, length=N_EXPERTS)
    starts = jnp.cumsum(counts) - counts
    padded = (counts + MOE_BM - 1) // MOE_BM * MOE_BM
    pad_end = jnp.cumsum(padded)
    pad_start = pad_end - padded
    dest = (pad_start[se] + jnp.arange(n_asg, dtype=jnp.int32) - starts[se]).astype(jnp.int32)
    n_blocks = -(-n_asg // MOE_BM) + N_EXPERTS
    slot_tok = jnp.full((n_blocks * MOE_BM,), n_tok, jnp.int32).at[dest].set(st)
    block_e = jnp.minimum(jnp.searchsorted(pad_end, jnp.arange(n_blocks) * MOE_BM, side='right'),
                          N_EXPERTS - 1).astype(jnp.int32)
    n_used = (pad_end[-1] // MOE_BM).astype(jnp.int32).reshape(1)
    x_pad = jnp.concatenate([tokens.astype(BF16), jnp.zeros((1, tokens.shape[1]), BF16)], axis=0)
    xb = x_pad[slot_tok]
    yb = _moe_experts(xb, block_e, n_used, w1, w3, w2)
    dest_asg = jnp.zeros((n_asg,), jnp.int32).at[order].set(dest).reshape(n_tok, TOP_K)
    return wts[:, 0:1] * yb[dest_asg[:, 0]] + wts[:, 1:2] * yb[dest_asg[:, 1]]


def _merge_out(z, merge, branches, b_merge, w_out_l, g1, cg1, seq, tc, d):
    tm2 = _pick_tile(seq, 288)
    bpb = seq // tm2

    def seg_select(i, ctx_val, lat_val):
        row = (i % bpb) * tm2 + lax.broadcasted_iota(jnp.int32, (tm2, 1), 0)
        return jnp.where(row < tc, ctx_val, lat_val)

    def merge_pro(i, mg, b0, b1, b2, b3, bm):
        gates = jax.nn.sigmoid(mg + bm)
        return (gates[:, 0 * d:1 * d] * b0 + gates[:, 1 * d:2 * d] * b1
                + gates[:, 2 * d:3 * d] * b2 + gates[:, 3 * d:4 * d] * b3)

    def resid_epi(i, acc, zb, gl, gc):
        return zb + seg_select(i, gc, gl[0]) * acc

    tn = d // 2
    return _fused_mm([merge] + branches, [_whole(b_merge[None, :])],
                     w_out_l.astype(BF16), merge_pro, tm=tm2, tn=tn,
                     epilogue=resid_epi, epi_rows=[z],
                     epi_aux=[(g1, (1, 1, tn), lambda i, j: (i // bpb, 0, j)),
                              (cg1, (1, tn), lambda i, j: (0, j))])


def kernel(x, c, ctx, c_ctx, w_ada, b_ada, norm1_g, norm2_g, w_in, b_merge, rwkv_mu_prev, rwkv_mu_next, rwkv_w0, rwkv_w2, rwkv_a0, rwkv_a2, rwkv_g2, rwkv_k_k, rwkv_k_a, rwkv_r_k, rwkv_gn_g, rwkv_gn_b, rwkv_proj, conv_w, conv_b, conv_ln_g, conv_ln_b, conv_proj, lru_conv_w, lru_conv_b, lru_wr, lru_br, lru_wi, lru_bi, lru_lambda, lru_proj, ret_gn_g, ret_gn_b, ret_proj, w_out, w_router, b_router, e_w1, e_w3, e_w2, final_g):
    bsz, n_tok, d = x.shape
    tc = ctx.shape[1]
    seq = tc + n_tok
    depth = w_in.shape[0]
    per_layer = {
        'rwkv_w0': rwkv_w0, 'rwkv_w2': rwkv_w2, 'rwkv_a0': rwkv_a0, 'rwkv_a2': rwkv_a2, 'rwkv_g2': rwkv_g2,
        'rwkv_k_k': rwkv_k_k, 'rwkv_k_a': rwkv_k_a, 'rwkv_r_k': rwkv_r_k,
        'rwkv_gn_g': rwkv_gn_g, 'rwkv_gn_b': rwkv_gn_b,
        'conv_w': conv_w, 'conv_b': conv_b, 'conv_ln_g': conv_ln_g, 'conv_ln_b': conv_ln_b,
        'lru_conv_w': lru_conv_w, 'lru_conv_b': lru_conv_b, 'lru_wr': lru_wr, 'lru_br': lru_br,
        'lru_wi': lru_wi, 'lru_bi': lru_bi, 'lru_lambda': lru_lambda,
        'ret_gn_g': ret_gn_g, 'ret_gn_b': ret_gn_b,
    }
    tm = _pick_tile(seq, 576)

    z = jnp.concatenate([ctx, x], axis=1).reshape(bsz * seq, d)
    for layer in range(depth):
        lp = {name: arr[layer] for name, arr in per_layer.items()}
        last = layer == depth - 1
        cc = jnp.concatenate([c, c_ctx[None, :]], axis=0)
        n_mod = -(-cc.shape[0] // 8) * 8
        cc = jnp.pad(cc, ((0, n_mod - cc.shape[0]), (0, 0)))
        mod = _fused_mm([cc], [], w_ada[layer].astype(BF16), lambda i, cb: cb * jax.nn.sigmoid(cb),
                        tm=n_mod, tn=6 * d // 4,
                        epilogue=lambda i, acc, bb: acc + bb,
                        epi_aux=[(b_ada[layer][None, :], (1, 6 * d // 4), lambda i, j: (0, j))])
        mod_l = mod[:bsz].reshape(bsz, 1, 6 * d)
        mod_c = mod[bsz:bsz + 1]
        sh1, sc1, g1, sh2, sc2, g2 = [mod_l[:, :, n * d:(n + 1) * d] for n in range(6)]
        csh1, csc1, cg1, csh2, csc2, cg2 = [mod_c[:, n * d:(n + 1) * d] for n in range(6)]

        w_l = w_in[layer].astype(BF16)
        norm_args = (norm1_g[layer][None, :], sh1, sc1, csh1, csc1)
        p_rwkv = _in_proj(z, seq, tc, *norm_args, w_l[:, :RWKV_WIDTH], 640,
                          rwkv_mu_prev[layer][None, :], rwkv_mu_next[layer][None, :])
        p_mid = _in_proj(z, seq, tc, *norm_args, w_l[:, RWKV_WIDTH:RWKV_WIDTH + MID_WIDTH], 512)
        p_merge = _in_proj(z, seq, tc, *norm_args, w_l[:, RWKV_WIDTH + MID_WIDTH:], 512)

        y_f, y_b = _rwkv_mix(p_rwkv, seq, tc, lp)
        o_f, o_b = _ret_mix(p_mid, seq, tc, n_tok)
        branches = [
            _rwkv_out_proj(y_f, y_b, p_rwkv, lp, rwkv_proj[layer], tm),
            _plain_mm(_conv_mix(p_mid, seq, tc, lp), conv_proj[layer].astype(BF16), tm=tm, tn=d),
            _plain_mm(_lru_mix(p_mid, seq, tc, lp), lru_proj[layer].astype(BF16), tm=tm, tn=d),
            _ret_out_proj(o_f, o_b, p_mid, lp, ret_proj[layer], tm),
        ]
        z = _merge_out(z, p_merge, branches, b_merge[layer], w_out[layer], g1, cg1, seq, tc, d)

        z3 = z.reshape(bsz, seq, d)
        zn = z3 * lax.rsqrt(jnp.mean(z3 * z3, axis=-1, keepdims=True) + RMS_EPS) * norm2_g[layer]
        is_ctx = (jnp.arange(seq) < tc)[None, :, None]
        h2 = zn * (1.0 + jnp.where(is_ctx, csc2[None], sc2)) + jnp.where(is_ctx, csh2[None], sh2)
        gate2 = jnp.where(is_ctx, cg2[None], g2)
        w1b, w3b, w2b = e_w1[layer].astype(BF16), e_w3[layer].astype(BF16), e_w2[layer].astype(BF16)
        if not last:
            y = _moe(h2.reshape(-1, d), w_router, b_router, w1b, w3b, w2b).reshape(bsz, seq, d)
            z = (z3 + gate2 * y).reshape(bsz * seq, d)
        else:
            y = _moe(h2[:, tc:].reshape(-1, d), w_router, b_router, w1b, w3b, w2b).reshape(bsz, n_tok, d)
            xl = z3[:, tc:] + g2 * y
            return xl * lax.rsqrt(jnp.mean(xl * xl, axis=-1, keepdims=True) + RMS_EPS) * final_g
    return None
```

```python
import functools
import math

import jax
import jax.numpy as jnp
from jax import lax
from jax.experimental import pallas as pl
from jax.experimental.pallas import tpu as pltpu

F32 = jnp.float32
BF16 = jnp.bfloat16

D_MODEL = 1024
GRID_W = 64
N_BRANCH = 4
BR_W = D_MODEL // 2
RWKV_HS = 64
RWKV_H = BR_W // RWKV_HS
RWKV_W_LORA = 64
RWKV_A_LORA = 64
RWKV_G_LORA = 128
RWKV_GN_EPS = 64e-5
L2_EPS = 1e-12
CONV_K = 31
LRU_BLOCKS = 8
LRU_BS = BR_W // LRU_BLOCKS
LRU_CONV = 4
LRU_C = 8.0
RET_H = 4
RET_DK = 64
RET_DV = BR_W // RET_H
RET_CHUNK = 128
ROPE_BASE = 10000.0
N_EXPERTS = 32
N_GROUPS = 4
EXPERTS_PER_GROUP = N_EXPERTS // N_GROUPS
TOP_K = 2
D_EXPERT = D_MODEL // 2
RMS_EPS = 1e-6
LN_EPS = 1e-5

REST_COLS = (('conv_val', BR_W), ('conv_gate', BR_W), ('lru_x', BR_W), ('lru_gate', BR_W),
             ('ret_q', RET_H * RET_DK), ('ret_k', RET_H * RET_DK), ('ret_v', BR_W), ('ret_g', BR_W))
RWKV_WIDTH = 3 * BR_W + 2 * RWKV_W_LORA + 2 * RWKV_A_LORA + RWKV_G_LORA
LORA_OFF = 3 * BR_W
LORA_W = 2 * RWKV_W_LORA + 2 * RWKV_A_LORA
MID_WIDTH = sum(w for _, w in REST_COLS)
MERGE_WIDTH = N_BRANCH * D_MODEL

VMEM_LIMIT = 48 * 1024 * 1024
RWKV_CHUNK = 64
RWKV_TB = 256
MOE_BM = 256


def _bdot(a, b):
    return jnp.dot(a.astype(BF16), b.astype(BF16), preferred_element_type=F32)


_NT = (((1,), (1,)), ((), ()))
_TN = (((0,), (0,)), ((), ()))


def _pick_tile(n, cap, mult=8):
    best = None
    for t in range(mult, cap + 1, mult):
        if n % t == 0:
            best = t
    assert best is not None, (n, cap, mult)
    return best


def _fused_mm_body(*refs, nr, na, ner, nea, prologue, epilogue, out_dtype):
    rows = refs[:nr]
    auxs = refs[nr:nr + na]
    w_ref = refs[nr + na]
    er = refs[nr + na + 1:nr + na + 1 + ner]
    ea = refs[nr + na + 1 + ner:nr + na + 1 + ner + nea]
    o_ref, act = refs[-2], refs[-1]
    i = pl.program_id(0)

    @pl.when(pl.program_id(1) == 0)
    def _():
        act[...] = prologue(i, *[r[...] for r in rows], *[a[...] for a in auxs]).astype(BF16)

    acc = jnp.dot(act[...], w_ref[...], preferred_element_type=F32)
    if epilogue is not None:
        acc = epilogue(i, acc, *[r[...] for r in er], *[a[...] for a in ea])
    o_ref[...] = acc.astype(out_dtype)


def _fused_mm(row_ins, aux_ins, w, prologue, *, tm, tn, out_dtype=F32, epilogue=None,
              epi_rows=(), epi_aux=()):
    row_ins = [r if isinstance(r, tuple) else (r, r.shape[1], 0) for r in row_ins]
    n = row_ins[0][0].shape[0]
    k, m = w.shape
    assert n % tm == 0 and m % tn == 0, (n, tm, m, tn)
    in_specs = [pl.BlockSpec((tm, bw), functools.partial(lambda i, j, cb: (i, cb), cb=cb))
                for (_, bw, cb) in row_ins]
    in_specs += [pl.BlockSpec(bs, im) for (_, bs, im) in aux_ins]
    in_specs += [pl.BlockSpec((k, tn), lambda i, j: (0, j))]
    in_specs += [pl.BlockSpec((tm, tn), lambda i, j: (i, j)) for _ in epi_rows]
    in_specs += [pl.BlockSpec(bs, im) for (_, bs, im) in epi_aux]
    body = functools.partial(_fused_mm_body, nr=len(row_ins), na=len(aux_ins), ner=len(epi_rows),
                             nea=len(epi_aux), prologue=prologue, epilogue=epilogue, out_dtype=out_dtype)
    return pl.pallas_call(
        body,
        grid=(n // tm, m // tn),
        in_specs=in_specs,
        out_specs=pl.BlockSpec((tm, tn), lambda i, j: (i, j)),
        out_shape=jax.ShapeDtypeStruct((n, m), out_dtype),
        scratch_shapes=[pltpu.VMEM((tm, k), BF16)],
        compiler_params=pltpu.CompilerParams(dimension_semantics=("parallel", "arbitrary"),
                                             vmem_limit_bytes=VMEM_LIMIT),
    )(*[a for (a, _, _) in row_ins], *[a for (a, _, _) in aux_ins], w, *epi_rows,
      *[a for (a, _, _) in epi_aux])


def _whole(a):
    nd = a.ndim
    return (a, a.shape, lambda i, j: (0,) * nd)


def _plain_mm(x, w, *, tm, tn, out_dtype=F32):
    return _fused_mm([x], [], w, lambda i, xb: xb, tm=tm, tn=tn, out_dtype=out_dtype)


def _in_proj_body(*refs, tc, shift):
    if shift:
        z_ref, g_ref, shl_ref, scl_ref, shc_ref, scc_ref, w_ref, mup_ref, mun_ref, o_ref, act = refs
    else:
        z_ref, g_ref, shl_ref, scl_ref, shc_ref, scc_ref, w_ref, o_ref, act = refs
    seq = z_ref.shape[0]
    row = lax.broadcasted_iota(jnp.int32, (seq, 1), 0)

    @pl.when(pl.program_id(1) == 0)
    def _():
        xb = z_ref[...]
        y = xb * lax.rsqrt(jnp.mean(xb * xb, axis=-1, keepdims=True) + RMS_EPS) * g_ref[...]
        is_ctx = row < tc
        sc = jnp.where(is_ctx, scc_ref[...], scl_ref[0])
        sh = jnp.where(is_ctx, shc_ref[...], shl_ref[0])
        act[...] = (y * (1.0 + sc) + sh).astype(BF16)

    acc = jnp.dot(act[...], w_ref[...], preferred_element_type=F32)
    if shift:
        prev = pltpu.roll(acc, 1, 0)
        nxt = pltpu.roll(acc, seq - 1, 0)
        prev = jnp.where((row == 0) | (row == tc), 0.0, prev)
        nxt = jnp.where((row == tc - 1) | (row == seq - 1), 0.0, nxt)
        acc = acc + mup_ref[...] * (prev - acc) + mun_ref[...] * (nxt - acc)
    o_ref[...] = acc.astype(o_ref.dtype)


def _in_proj(z, seq, tc, g, sh_l, sc_l, sh_c, sc_c, w, tn, mu_prev=None, mu_next=None, out_dtype=F32):
    n, d = z.shape
    m = w.shape[1]
    shift = mu_prev is not None
    assert m % tn == 0
    vec = lambda width: pl.BlockSpec((1, width), lambda b, j: (0, 0))
    per_b = pl.BlockSpec((1, 1, d), lambda b, j: (b, 0, 0))
    in_specs = [pl.BlockSpec((seq, d), lambda b, j: (b, 0)), vec(d), per_b, per_b, vec(d), vec(d),
                pl.BlockSpec((d, tn), lambda b, j: (0, j))]
    args = [z, g, sh_l, sc_l, sh_c, sc_c, w]
    if shift:
        in_specs += [pl.BlockSpec((1, tn), lambda b, j: (0, j))] * 2
        args += [mu_prev, mu_next]
    return pl.pallas_call(
        functools.partial(_in_proj_body, tc=tc, shift=shift),
        grid=(n // seq, m // tn),
        in_specs=in_specs,
        out_specs=pl.BlockSpec((seq, tn), lambda b, j: (b, j)),
        out_shape=jax.ShapeDtypeStruct((n, m), out_dtype),
        scratch_shapes=[pltpu.VMEM((seq, d), BF16)],
        compiler_params=pltpu.CompilerParams(dimension_semantics=("parallel", "arbitrary"),
                                             vmem_limit_bytes=VMEM_LIMIT),
    )(*args)


def _rwkv_body(rf, kf, vf, lf, rb_, kb, vb_, lb, w0_ref, w2_ref, a0_ref, a2_ref, kkw_ref, ka_ref, bd_ref,
               of_ref, ob_ref, s_ref, kk_ref, *, n_chunks):
    c_len = RWKV_CHUNK

    @pl.when(pl.program_id(1) == 0)
    def _():
        s_ref[...] = jnp.zeros_like(s_ref)

    for d, k_blk in enumerate((kf, kb)):
        kx = k_blk[...] * kkw_ref[...]
        kk_ref[d] = kx * lax.rsqrt(_bdot(kx * kx, bd_ref[...]) + L2_EPS)

    ri = lax.broadcasted_iota(jnp.int32, (c_len, c_len), 0)
    ci = lax.broadcasted_iota(jnp.int32, (c_len, c_len), 1)
    eye = (ri == ci).astype(F32)
    in_refs = ((rf, kf, vf, lf), (rb_, kb, vb_, lb))
    masks = []
    for d in range(2):
        before = (ri > ci) if d == 0 else (ri < ci)
        strict = before.astype(F32)
        incl = (before | (ri == ci)).astype(F32)
        m8 = jnp.where(((ri // 8) == (ci // 8)) & before, 1.0, 0.0).astype(F32)
        merges = []
        for size in (16, 32, 64):
            same = (ri // size) == (ci // size)
            inner = (ri // (size // 2)) == (ci // (size // 2))
            merges.append(jnp.where(same & jnp.logical_not(inner) & before, 1.0, 0.0).astype(F32))
        masks.append(dict(strict=strict, incl=incl, incl_b=incl.astype(BF16), m8=m8, merges=merges))
    out_refs = (of_ref, ob_ref)
    decay_scale = math.exp(-0.5)

    def chunk(c, carry):
        per_dir = []
        for d in range(2):
            cc = c if d == 0 else n_chunks - 1 - c
            sl = pl.ds(pl.multiple_of(cc * c_len, c_len), c_len)
            r_ref, k_ref, v_ref, lo_ref = in_refs[d]
            r = r_ref[sl, :]
            k = k_ref[sl, :]
            v = v_ref[sl, :]
            lo = lo_ref[sl, :]
            wlo = lo[:, d * RWKV_W_LORA:(d + 1) * RWKV_W_LORA]
            alo = lo[:, 2 * RWKV_W_LORA + d * RWKV_A_LORA:2 * RWKV_W_LORA + (d + 1) * RWKV_A_LORA]
            w_raw = w0_ref[d] + _bdot(jnp.tanh(wlo), w2_ref[d])
            lw = -decay_scale * jax.nn.sigmoid(w_raw)
            a = jax.nn.sigmoid(a0_ref[d] + _bdot(alo, a2_ref[d]))
            k_d = k * (1.0 + (a - 1.0) * ka_ref[...])
            kk = kk_ref[d, sl, :]
            hi = lw.astype(BF16)
            lo2 = (lw - hi.astype(F32)).astype(BF16)
            tri = masks[d]['incl_b']
            cum = (jnp.dot(tri, hi, preferred_element_type=F32)
                   + jnp.dot(tri, lo2, preferred_element_type=F32))
            eg = jnp.exp(cum)
            ieg = jnp.exp(-cum)
            last = c_len - 1 if d == 0 else 0
            per_dir.append(dict(
                rt=(r * eg).astype(BF16), kt=(k_d * ieg).astype(BF16), at=(-(kk * a) * ieg).astype(BF16),
                bt=(kk * jnp.exp(cum - lw)).astype(BF16), vb=v.astype(BF16), g_last=eg[last:last + 1, :],
                rows=sl))
        chains = [(d, h) for d in range(2) for h in range(RWKV_H)]
        hsl = lambda h: slice(h * RWKV_HS, (h + 1) * RWKV_HS)
        rb = [jnp.concatenate([per_dir[d]['rt'][:, hsl(h)], per_dir[d]['bt'][:, hsl(h)]], axis=0)
              for d, h in chains]
        at_h = [per_dir[d]['at'][:, hsl(h)] for d, h in chains]
        kt_h = [per_dir[d]['kt'][:, hsl(h)] for d, h in chains]
        v_h = [per_dir[d]['vb'][:, hsl(h)] for d, h in chains]
        mk = [masks[d] for d, h in chains]
        n = len(chains)
        m_a = [lax.dot_general(rb[i], at_h[i], _NT, preferred_element_type=F32) for i in range(n)]
        m_k = [lax.dot_general(rb[i], kt_h[i], _NT, preferred_element_type=F32) for i in range(n)]
        ra = [m_a[i][:c_len] * mk[i]['incl'] for i in range(n)]
        nmat = [m_a[i][c_len:] * mk[i]['strict'] for i in range(n)]
        rk = [m_k[i][:c_len] * mk[i]['incl'] for i in range(n)]
        bk = [m_k[i][c_len:] * mk[i]['strict'] for i in range(n)]
        d1 = [nmat[i] * mk[i]['m8'] for i in range(n)]
        d2 = [_bdot(d1[i], d1[i]) for i in range(n)]
        d4 = [_bdot(d2[i], d2[i]) for i in range(n)]
        tinv = [eye + d1[i] for i in range(n)]
        tinv = [tinv[i] + _bdot(tinv[i], d2[i]) for i in range(n)]
        tinv = [tinv[i] + _bdot(tinv[i], d4[i]) for i in range(n)]
        for lvl in range(3):
            e = [_bdot(tinv[i], nmat[i] * mk[i]['merges'][lvl]) for i in range(n)]
            tinv = [tinv[i] + _bdot(e[i], tinv[i]) for i in range(n)]
        s0 = [s_ref[d, h] for d, h in chains]
        rbs = [lax.dot_general(rb[i], s0[i].astype(BF16), _NT, preferred_element_type=F32) for i in range(n)]
        x = [rbs[i][c_len:] + _bdot(bk[i], v_h[i]) for i in range(n)]
        u = [_bdot(tinv[i], x[i]) for i in range(n)]
        y = [rbs[i][:c_len] + _bdot(ra[i], u[i]) + _bdot(rk[i], v_h[i]) for i in range(n)]
        for i, (d, h) in enumerate(chains):
            uv = jnp.concatenate([u[i].astype(BF16), v_h[i]], axis=0)
            ak = jnp.concatenate([at_h[i], kt_h[i]], axis=0)
            ds = lax.dot_general(uv, ak, _TN, preferred_element_type=F32)
            s_ref[d, h] = (s0[i] + ds) * per_dir[d]['g_last'][:, hsl(h)]
        for d in range(2):
            out_refs[d][per_dir[d]['rows'], :] = jnp.concatenate(
                [y[d * RWKV_H + h] for h in range(RWKV_H)], axis=1)
        return carry

    lax.fori_loop(0, n_chunks, chunk, 0)


def _rwkv_mix(p_rwkv, seq, tc, lp):
    n = p_rwkv.shape[0]
    bsz = n // seq
    tb = _pick_tile(math.gcd(seq, tc), RWKV_TB, RWKV_CHUNK)
    nblk, cblk = seq // tb, tc // tb
    assert LORA_OFF % LORA_W == 0

    def fwd_blk(t):
        return t

    def bwd_blk(t):
        return jnp.where(t < cblk, cblk - 1 - t, nblk - 1 - (t - cblk))

    def col_spec(width, cb, blk_fn):
        return pl.BlockSpec((tb, width), lambda b, t: (b * nblk + blk_fn(t), cb))

    in_specs = []
    for blk_fn in (fwd_blk, bwd_blk):
        in_specs += [col_spec(BR_W, 0, blk_fn), col_spec(BR_W, 1, blk_fn), col_spec(BR_W, 2, blk_fn),
                     col_spec(LORA_W, LORA_OFF // LORA_W, blk_fn)]
    whole = lambda a: pl.BlockSpec(a.shape, lambda b, t: (0,) * a.ndim)
    head_id = jnp.arange(BR_W) // RWKV_HS
    bd = (head_id[:, None] == head_id[None, :]).astype(BF16)
    consts = [lp['rwkv_w0'][:, None, :], lp['rwkv_w2'].astype(BF16), lp['rwkv_a0'][:, None, :],
              lp['rwkv_a2'].astype(BF16), lp['rwkv_k_k'][None, :], lp['rwkv_k_a'][None, :], bd]
    in_specs += [whole(a) for a in consts]
    return pl.pallas_call(
        functools.partial(_rwkv_body, n_chunks=tb // RWKV_CHUNK),
        grid=(bsz, nblk),
        in_specs=in_specs,
        out_specs=[col_spec(BR_W, 0, fwd_blk), col_spec(BR_W, 0, bwd_blk)],
        out_shape=[jax.ShapeDtypeStruct((n, BR_W), F32)] * 2,
        scratch_shapes=[pltpu.VMEM((2, RWKV_H, RWKV_HS, RWKV_HS), F32), pltpu.VMEM((2, tb, BR_W), F32)],
        compiler_params=pltpu.CompilerParams(
            dimension_semantics=("parallel", "arbitrary"), vmem_limit_bytes=VMEM_LIMIT),
    )(*([p_rwkv] * 8), *consts)


def _rwkv_out_proj(y_f, y_b, p_rwkv, lp, w_proj, tm):
    head_id = jnp.arange(BR_W) // RWKV_HS
    bd = (head_id[:, None] == head_id[None, :]).astype(BF16)

    def hsum(xv, bdm):
        hi = xv.astype(BF16)
        lo = (xv - hi.astype(F32)).astype(BF16)
        return (jnp.dot(hi, bdm, preferred_element_type=F32) + jnp.dot(lo, bdm, preferred_element_type=F32))

    def act(i, yf, yb, r, k, v, glo, gn_g, gn_b, r_k, g2, bdm):
        inv = 1.0 / RWKV_HS
        yb = yf + yb
        yc = yb - hsum(yb, bdm) * inv
        var = hsum(yc * yc, bdm) * inv
        yn = yc * lax.rsqrt(var + RWKV_GN_EPS) * gn_g + gn_b
        bonus = hsum(r * k * r_k, bdm) * v
        g = _bdot(jax.nn.sigmoid(glo), g2)
        return (yn + bonus) * g

    glo_cb = (LORA_OFF + LORA_W) // RWKV_G_LORA
    assert (LORA_OFF + LORA_W) % RWKV_G_LORA == 0
    return _fused_mm(
        [y_f, y_b, (p_rwkv, BR_W, 0), (p_rwkv, BR_W, 1), (p_rwkv, BR_W, 2), (p_rwkv, RWKV_G_LORA, glo_cb)],
        [_whole(lp['rwkv_gn_g'][None, :]), _whole(lp['rwkv_gn_b'][None, :]), _whole(lp['rwkv_r_k'][None, :]),
         _whole(lp['rwkv_g2'].astype(BF16)), _whole(bd)],
        w_proj.astype(BF16), act, tm=tm, tn=w_proj.shape[1], out_dtype=BF16)


LRU_TB = 256
LRU_HALO = 8


def _lru_body(x_ref, g_ref, cw_ref, cb_ref, wr_ref, br_ref, wi_ref, bi_ref, sp_ref, o_ref, h_ref, *,
              tb, nblk, cblk):
    seq = x_ref.shape[0]
    row = lax.broadcasted_iota(jnp.int32, (tb, 1), 0)
    n_ext = tb + LRU_HALO
    for d in range(2):
        h_ref[...] = jnp.zeros_like(h_ref)

        def blk_body(i, carry, d=d):
            if d == 0:
                blk = i
                halo_ok = (blk != 0) & (blk != cblk)
                halo0 = jnp.maximum(blk * tb - LRU_HALO, 0)
            else:
                blk = jnp.where(i < cblk, cblk - 1 - i, nblk - 1 - (i - cblk))
                halo_ok = (blk != cblk - 1) & (blk != nblk - 1)
                halo0 = jnp.minimum(blk * tb + tb, seq - LRU_HALO)
            rows = pl.ds(pl.multiple_of(blk * tb, tb), tb)
            cur = x_ref[rows, :]
            halo = x_ref[pl.ds(pl.multiple_of(halo0, LRU_HALO), LRU_HALO), :] * jnp.where(halo_ok, 1.0, 0.0)
            xc = cb_ref[d]
            if d == 0:
                ext = jnp.concatenate([halo, cur], axis=0)
                for j in range(LRU_CONV):
                    sh = LRU_CONV - 1 - j
                    tap = ext if sh == 0 else pltpu.roll(ext, sh, 0)
                    xc = xc + cw_ref[d, j:j + 1, :] * tap[LRU_HALO:, :]
            else:
                ext = jnp.concatenate([cur, halo], axis=0)
                for j in range(LRU_CONV):
                    tap = ext if j == 0 else pltpu.roll(ext, n_ext - j, 0)
                    xc = xc + cw_ref[d, j:j + 1, :] * tap[:tb, :]
            gate_r = jax.nn.sigmoid(_bdot(xc, wr_ref[d]) + br_ref[d])
            gate_i = jax.nn.sigmoid(_bdot(xc, wi_ref[d]) + bi_ref[d])
            log_a = -LRU_C * gate_r * sp_ref[d]
            a_cum = jnp.exp(log_a)
            b_cum = jnp.sqrt(1.0 - jnp.exp(2.0 * log_a)) * (gate_i * xc)
            s = 1
            while s < tb:
                if d == 0:
                    ok = row >= s
                    a_sh = jnp.where(ok, pltpu.roll(a_cum, s, 0), 1.0)
                    b_sh = jnp.where(ok, pltpu.roll(b_cum, s, 0), 0.0)
                else:
                    ok = row < tb - s
                    a_sh = jnp.where(ok, pltpu.roll(a_cum, tb - s, 0), 1.0)
                    b_sh = jnp.where(ok, pltpu.roll(b_cum, tb - s, 0), 0.0)
                b_cum = a_cum * b_sh + b_cum
                a_cum = a_cum * a_sh
                s *= 2
            h = b_cum + a_cum * h_ref[...]
            if d == 0:
                h_ref[...] = h[tb - 1:tb, :]
                o_ref[rows, :] = h
            else:
                h_ref[...] = h[0:1, :]
                o_ref[rows, :] = (o_ref[rows, :] + h) * jax.nn.gelu(g_ref[rows, :])
            return carry

        lax.fori_loop(0, nblk, blk_body, 0)


def _lru_mix(p_mid, seq, tc, lp):
    n = p_mid.shape[0]
    tb = _pick_tile(math.gcd(seq, tc), LRU_TB)
    nblk, cblk = seq // tb, tc // tb
    eye = jnp.eye(LRU_BLOCKS, dtype=F32)
    blockdiag = lambda w: jnp.einsum('dgij,gh->dgihj', w, eye).reshape(2, BR_W, BR_W).astype(BF16)
    consts = [lp['lru_conv_w'], lp['lru_conv_b'][:, None, :], blockdiag(lp['lru_wr']), lp['lru_br'][:, None, :],
              blockdiag(lp['lru_wi']), lp['lru_bi'][:, None, :], jax.nn.softplus(-lp['lru_lambda'])[:, None, :]]
    whole = lambda a: pl.BlockSpec(a.shape, lambda b: (0,) * a.ndim)
    x_cb = [name for name, _ in REST_COLS].index('lru_x')
    g_cb = [name for name, _ in REST_COLS].index('lru_gate')
    assert all(w == BR_W for _, w in REST_COLS[:max(x_cb, g_cb) + 1])
    return pl.pallas_call(
        functools.partial(_lru_body, tb=tb, nblk=nblk, cblk=cblk),
        grid=(n // seq,),
        in_specs=[pl.BlockSpec((seq, BR_W), lambda b: (b, x_cb)), pl.BlockSpec((seq, BR_W), lambda b: (b, g_cb))]
        + [whole(a) for a in consts],
        out_specs=pl.BlockSpec((seq, BR_W), lambda b: (b, 0)),
        out_shape=jax.ShapeDtypeStruct((n, BR_W), F32),
        scratch_shapes=[pltpu.VMEM((1, BR_W), F32)],
        compiler_params=pltpu.CompilerParams(dimension_semantics=("parallel",), vmem_limit_bytes=VMEM_LIMIT),
    )(p_mid, p_mid, *consts)


CONV_TB = 256
CONV_HALO = 16


def _conv_body(val_ref, gate_ref, w_ref, b_ref, lng_ref, lnb_ref, o_ref, *, tb, nblk, cblk):
    seq = val_ref.shape[0]
    n_ext = tb + 2 * CONV_HALO
    pad = CONV_K // 2

    def glu(rows):
        return val_ref[rows, :] * jax.nn.sigmoid(gate_ref[rows, :])

    def blk_body(blk, carry):
        t0 = blk * tb
        rows = pl.ds(pl.multiple_of(t0, tb), tb)
        lo_ok = (blk != 0) & (blk != cblk)
        hi_ok = (blk != cblk - 1) & (blk != nblk - 1)
        lo0 = jnp.maximum(t0 - CONV_HALO, 0)
        hi0 = jnp.minimum(t0 + tb, seq - CONV_HALO)
        lo = glu(pl.ds(pl.multiple_of(lo0, CONV_HALO), CONV_HALO)) * jnp.where(lo_ok, 1.0, 0.0)
        hi = glu(pl.ds(pl.multiple_of(hi0, CONV_HALO), CONV_HALO)) * jnp.where(hi_ok, 1.0, 0.0)
        ext = jnp.concatenate([lo, glu(rows), hi], axis=0)
        acc = jnp.zeros((tb, ext.shape[1]), F32) + b_ref[...]
        for r in range(8):
            rolled = ext if r == 0 else pltpu.roll(ext, n_ext - r, 0)
            for j in range(CONV_K):
                off = CONV_HALO - pad + j
                if off % 8 == r:
                    acc = acc + w_ref[j:j + 1, :] * rolled[off - r:off - r + tb, :]
        mu = jnp.mean(acc, axis=-1, keepdims=True)
        xc = acc - mu
        yn = xc * lax.rsqrt(jnp.mean(xc * xc, axis=-1, keepdims=True) + LN_EPS) * lng_ref[...] + lnb_ref[...]
        o_ref[rows, :] = yn * jax.nn.sigmoid(yn)
        return carry

    lax.fori_loop(0, nblk, blk_body, 0)


def _conv_mix(p_mid, seq, tc, lp):
    n = p_mid.shape[0]
    tb = _pick_tile(math.gcd(seq, tc), CONV_TB, CONV_HALO)
    names = [name for name, _ in REST_COLS]
    v_cb, g_cb = names.index('conv_val'), names.index('conv_gate')
    assert all(w == BR_W for _, w in REST_COLS[:max(v_cb, g_cb) + 1]) and CONV_HALO >= CONV_K // 2
    consts = [lp['conv_w'], lp['conv_b'][None, :], lp['conv_ln_g'][None, :], lp['conv_ln_b'][None, :]]
    whole = lambda a: pl.BlockSpec(a.shape, lambda b: (0,) * a.ndim)
    return pl.pallas_call(
        functools.partial(_conv_body, tb=tb, nblk=seq // tb, cblk=tc // tb),
        grid=(n // seq,),
        in_specs=[pl.BlockSpec((seq, BR_W), lambda b: (b, v_cb)), pl.BlockSpec((seq, BR_W), lambda b: (b, g_cb))]
        + [whole(a) for a in consts],
        out_specs=pl.BlockSpec((seq, BR_W), lambda b: (b, 0)),
        out_shape=jax.ShapeDtypeStruct((n, BR_W), F32),
        compiler_params=pltpu.CompilerParams(dimension_semantics=("parallel",), vmem_limit_bytes=VMEM_LIMIT),
    )(p_mid, p_mid, *consts)


RET_TB = 256
_RET_LOG_G = [math.log1p(-2.0 ** (-5.0 - h)) for h in range(RET_H)]


def _ret_body(qf, kf, vf, cf, sf, qb, kb, vb, cb, sb, dm_ref, xi_ref, zt_ref, of_ref, ob_ref, s_ref, *, n_chunks):
    c_len = RET_CHUNK
    qk_w = RET_H * RET_DK

    @pl.when(pl.program_id(1) == 0)
    def _():
        s_ref[...] = jnp.zeros_like(s_ref)

    lane = lax.broadcasted_iota(jnp.int32, (c_len, qk_w), 1)
    first_half = (lane % RET_DK) < (RET_DK // 2)

    def rotary(z, cos, sin):
        swapped = jnp.where(first_half, pltpu.roll(z, qk_w - RET_DK // 2, 1), pltpu.roll(z, RET_DK // 2, 1))
        return z * cos + swapped * sin

    in_refs = ((qf, kf, vf, cf, sf), (qb, kb, vb, cb, sb))
    out_refs = (of_ref, ob_ref)

    def chunk(c, carry):
        per_dir = []
        for d in range(2):
            cc = c if d == 0 else n_chunks - 1 - c
            sl = pl.ds(pl.multiple_of(cc * c_len, c_len), c_len)
            q_ref, k_ref, v_ref, c_ref, sn_ref = in_refs[d]
            cos, sin = c_ref[sl, :], sn_ref[sl, :]
            q = rotary(q_ref[sl, :], cos, sin)
            k = rotary(k_ref[sl, :], cos, sin) * (RET_DK ** -0.5)
            per_dir.append(dict(q=q.astype(BF16), k=k.astype(BF16), kz=(k * zt_ref[d]).astype(BF16),
                                v=v_ref[sl, :].astype(BF16), sl=sl))
        groups = [(d, h) for d in range(2) for h in range(RET_H)]
        ksl = lambda h: slice(h * RET_DK, (h + 1) * RET_DK)
        vsl = lambda h: slice(h * RET_DV, (h + 1) * RET_DV)
        q_h = [per_dir[d]['q'][:, ksl(h)] for d, h in groups]
        v_h = [per_dir[d]['v'][:, vsl(h)] for d, h in groups]
        att = [lax.dot_general(q_h[i], per_dir[d]['k'][:, ksl(h)], _NT, preferred_element_type=F32) * dm_ref[d, h]
               for i, (d, h) in enumerate(groups)]
        s0 = [s_ref[d, h] for d, h in groups]
        o = [_bdot(att[i], v_h[i]) + _bdot(q_h[i], s0[i]) * xi_ref[d][:, vsl(h)]
             for i, (d, h) in enumerate(groups)]
        for i, (d, h) in enumerate(groups):
            kv = lax.dot_general(per_dir[d]['kz'][:, ksl(h)], v_h[i], _TN, preferred_element_type=F32)
            s_ref[d, h] = s0[i] * math.exp(_RET_LOG_G[h] * c_len) + kv
        for d in range(2):
            out_refs[d][per_dir[d]['sl'], :] = jnp.concatenate([o[d * RET_H + h] for h in range(RET_H)], axis=1)
        return carry

    lax.fori_loop(0, n_chunks, chunk, 0)


def _ret_mix(p_mid, seq, tc, n_tok):
    n = p_mid.shape[0]
    tb = _pick_tile(math.gcd(seq, tc), RET_TB, RET_CHUNK)
    nblk, cblk = seq // tb, tc // tb
    c_len = RET_CHUNK
    n_rows = n_tok // GRID_W
    rows = jnp.repeat(jnp.arange(n_rows, dtype=F32), GRID_W)
    cols = jnp.tile(jnp.arange(GRID_W, dtype=F32), n_rows)
    n_freq = RET_DK // 4
    inv = ROPE_BASE ** (-jnp.arange(n_freq, dtype=F32) / n_freq)
    ang = jnp.concatenate([rows[:, None] * inv, cols[:, None] * inv], axis=-1)
    cos_h = jnp.concatenate([jnp.cos(ang), jnp.cos(ang)], axis=-1)
    sin_h = jnp.concatenate([-jnp.sin(ang), jnp.sin(ang)], axis=-1)
    pad_ctx = lambda tbl, fill: jnp.concatenate(
        [jnp.full((tc, RET_H * RET_DK), fill, F32), jnp.tile(tbl, (1, RET_H))], axis=0)
    cos_t, sin_t = pad_ctx(cos_h, 1.0), pad_ctx(sin_h, 0.0)
    log_g = jnp.asarray(_RET_LOG_G, F32)
    idx = jnp.arange(c_len, dtype=F32)
    dm, xi, zt = [], [], []
    for d in range(2):
        pos = idx if d == 0 else c_len - 1.0 - idx
        diff = pos[:, None] - pos[None, :]
        keep = diff >= 0 if d == 0 else diff > 0
        dm.append(jnp.where(keep[None], jnp.exp(log_g[:, None, None] * jnp.maximum(diff, 0.0)[None]), 0.0))
        xi.append(jnp.repeat(jnp.exp(log_g[None, :] * (pos[:, None] + 1.0)), RET_DV, axis=1))
        zt.append(jnp.repeat(jnp.exp(log_g[None, :] * (c_len - 1.0 - pos)[:, None]), RET_DK, axis=1))
    consts = [jnp.stack(dm), jnp.stack(xi), jnp.stack(zt)]

    def fwd_blk(t):
        return t

    def bwd_blk(t):
        return jnp.where(t < cblk, cblk - 1 - t, nblk - 1 - (t - cblk))

    names = [name for name, _ in REST_COLS]
    offs = {name: sum(w for _, w in REST_COLS[:i]) for i, (name, _) in enumerate(REST_COLS)}
    qk_w = RET_H * RET_DK
    in_specs = []
    for blk_fn in (fwd_blk, bwd_blk):
        spec = lambda width, cb, blk_fn=blk_fn: pl.BlockSpec((tb, width), lambda b, t: (b * nblk + blk_fn(t), cb))
        tbl = lambda blk_fn=blk_fn: pl.BlockSpec((tb, qk_w), lambda b, t: (blk_fn(t), 0))
        assert offs['ret_q'] % qk_w == 0 and offs['ret_k'] % qk_w == 0 and offs['ret_v'] % BR_W == 0
        in_specs += [spec(qk_w, offs['ret_q'] // qk_w), spec(qk_w, offs['ret_k'] // qk_w),
                     spec(BR_W, offs['ret_v'] // BR_W), tbl(), tbl()]
    whole = lambda a: pl.BlockSpec(a.shape, lambda b, t: (0,) * a.ndim)
    in_specs += [whole(a) for a in consts]
    out_spec = lambda blk_fn: pl.BlockSpec((tb, BR_W), lambda b, t: (b * nblk + blk_fn(t), 0))
    return pl.pallas_call(
        functools.partial(_ret_body, n_chunks=tb // c_len),
        grid=(n // seq, nblk),
        in_specs=in_specs,
        out_specs=[out_spec(fwd_blk), out_spec(bwd_blk)],
        out_shape=[jax.ShapeDtypeStruct((n, BR_W), F32)] * 2,
        scratch_shapes=[pltpu.VMEM((2, RET_H, RET_DK, RET_DV), F32)],
        compiler_params=pltpu.CompilerParams(
            dimension_semantics=("parallel", "arbitrary"), vmem_limit_bytes=VMEM_LIMIT),
    )(*([p_mid, p_mid, p_mid, cos_t, sin_t] * 2), *consts)


def _ret_out_proj(o_f, o_b, p_mid, lp, w_proj, tm):
    def act(i, of, ob, g, gn_g, gn_b):
        o = of + ob
        normed = []
        for h in range(RET_H):
            oh = o[:, h * RET_DV:(h + 1) * RET_DV]
            oc = oh - jnp.mean(oh, axis=-1, keepdims=True)
            normed.append(oc * lax.rsqrt(jnp.mean(oc * oc, axis=-1, keepdims=True) + LN_EPS))
        yn = jnp.concatenate(normed, axis=1) * gn_g + gn_b
        return g * jax.nn.sigmoid(g) * yn

    off_g = sum(w for _, w in REST_COLS[:[name for name, _ in REST_COLS].index('ret_g')])
    assert off_g % BR_W == 0
    return _fused_mm([o_f, o_b, (p_mid, BR_W, off_g // BR_W)],
                     [_whole(lp['ret_gn_g'][None, :]), _whole(lp['ret_gn_b'][None, :])],
                     w_proj.astype(BF16), act, tm=tm, tn=w_proj.shape[1], out_dtype=BF16)


def _moe_body(be_ref, nu_ref, x_ref, w1_ref, w3_ref, w2_ref, o_ref):
    i = pl.program_id(0)

    @pl.when(i < nu_ref[0])
    def _():
        xb = x_ref[...]
        h1 = jnp.dot(xb, w1_ref[0].astype(BF16), preferred_element_type=F32)
        h3 = jnp.dot(xb, w3_ref[0].astype(BF16), preferred_element_type=F32)
        hid = (h1 * jax.nn.sigmoid(h1) * h3).astype(BF16)
        o_ref[...] = jnp.dot(hid, w2_ref[0].astype(BF16), preferred_element_type=F32)

    @pl.when(i >= nu_ref[0])
    def _():
        o_ref[...] = jnp.zeros_like(o_ref)


def _moe_experts(xb, block_e, n_used, w1, w3, w2):
    n_rows, d = xb.shape
    n_blocks = n_rows // MOE_BM
    de = w1.shape[2]
    grid_spec = pltpu.PrefetchScalarGridSpec(
        num_scalar_prefetch=2,
        grid=(n_blocks,),
        in_specs=[
            pl.BlockSpec((MOE_BM, d), lambda i, be, nu: (i, 0)),
            pl.BlockSpec((1, d, de), lambda i, be, nu: (be[i], 0, 0)),
            pl.BlockSpec((1, d, de), lambda i, be, nu: (be[i], 0, 0)),
            pl.BlockSpec((1, de, d), lambda i, be, nu: (be[i], 0, 0)),
        ],
        out_specs=pl.BlockSpec((MOE_BM, d), lambda i, be, nu: (i, 0)),
    )
    return pl.pallas_call(
        _moe_body,
        grid_spec=grid_spec,
        out_shape=jax.ShapeDtypeStruct((n_rows, d), F32),
        compiler_params=pltpu.CompilerParams(dimension_semantics=("arbitrary",),
                                             vmem_limit_bytes=VMEM_LIMIT),
    )(block_e, n_used, xb, w1, w3, w2)


def _moe(tokens, w_router, b_router, w1, w3, w2):
    n_tok = tokens.shape[0]
    aff = jax.nn.sigmoid(jnp.dot(tokens, w_router, precision=lax.Precision.HIGHEST))
    sel = (aff + b_router).reshape(-1, N_GROUPS, EXPERTS_PER_GROUP)
    grp_score = jnp.sum(lax.top_k(sel, TOP_K)[0], axis=-1)
    grp = jnp.argmax(grp_score, axis=-1)
    grp_mask = jnp.arange(N_GROUPS)[None, :] == grp[:, None]
    masked = jnp.where(grp_mask[:, :, None], sel, -jnp.inf).reshape(-1, N_EXPERTS)
    _, e_idx = lax.top_k(masked, TOP_K)
    wts = jnp.take_along_axis(aff, e_idx, axis=1)
    wts = wts / jnp.sum(wts, axis=-1, keepdims=True)

    n_asg = n_tok * TOP_K
    flat_e = e_idx.reshape(-1).astype(jnp.int32)
    order = jnp.argsort(flat_e).astype(jnp.int32)
    se, st = flat_e[order], order // TOP_K
    counts = jnp.sum(flat_e[:, None] == jnp.arange(N_EXPERTS, dtype=jnp.int32)[None, :], axis=0, dtype=jnp.int32)
    starts = jnp.cumsum(counts) - counts
    padded = (counts + MOE_BM - 1) // MOE_BM * MOE_BM
    pad_end = jnp.cumsum(padded)
    pad_start = pad_end - padded
    dest = pad_start[se] + jnp.arange(n_asg, dtype=jnp.int32) - starts[se]
    dest_asg = dest[jnp.argsort(order)].reshape(n_tok, TOP_K)
    n_blocks = -(-n_asg // MOE_BM) + N_EXPERTS
    block_e = jnp.minimum(jnp.searchsorted(pad_end, jnp.arange(n_blocks) * MOE_BM, side='right'),
                          N_EXPERTS - 1).astype(jnp.int32)
    n_used = (pad_end[-1] // MOE_BM).astype(jnp.int32).reshape(1)
    slot_e = jnp.repeat(block_e, MOE_BM)
    rank = jnp.arange(n_blocks * MOE_BM, dtype=jnp.int32) - pad_start[slot_e]
    slot_tok = jnp.where(rank < counts[slot_e], st[jnp.clip(starts[slot_e] + rank, 0, n_asg - 1)], n_tok)
    x_pad = jnp.concatenate([tokens.astype(BF16), jnp.zeros((1, tokens.shape[1]), BF16)], axis=0)
    xb = x_pad[slot_tok]
    yb = _moe_experts(xb, block_e, n_used, w1, w3, w2)
    return wts[:, 0:1] * yb[dest_asg[:, 0]] + wts[:, 1:2] * yb[dest_asg[:, 1]]


def _merge_out(z, merge, branches, b_merge, w_out_l, g1, cg1, seq, tc, d):
    tm2 = _pick_tile(seq, 576, 16)
    bpb = seq // tm2

    def seg_select(i, ctx_val, lat_val):
        row = (i % bpb) * tm2 + lax.broadcasted_iota(jnp.int32, (tm2, 1), 0)
        return jnp.where(row < tc, ctx_val, lat_val)

    def merge_pro(i, mg, b0, b1, b2, b3, bm):
        gates = jax.nn.sigmoid(mg.astype(F32) + bm)
        b0, b1, b2, b3 = [b.astype(F32) for b in (b0, b1, b2, b3)]
        return (gates[:, 0 * d:1 * d] * b0 + gates[:, 1 * d:2 * d] * b1
                + gates[:, 2 * d:3 * d] * b2 + gates[:, 3 * d:4 * d] * b3)

    def resid_epi(i, acc, zb, gl, gc):
        return zb + seg_select(i, gc, gl[0]) * acc

    tn = d // 2
    return _fused_mm([merge] + branches, [_whole(b_merge[None, :])],
                     w_out_l.astype(BF16), merge_pro, tm=tm2, tn=tn,
                     epilogue=resid_epi, epi_rows=[z],
                     epi_aux=[(g1, (1, 1, tn), lambda i, j: (i // bpb, 0, j)),
                              (cg1, (1, tn), lambda i, j: (0, j))])


def kernel(x, c, ctx, c_ctx, w_ada, b_ada, norm1_g, norm2_g, w_in, b_merge, rwkv_mu_prev, rwkv_mu_next, rwkv_w0, rwkv_w2, rwkv_a0, rwkv_a2, rwkv_g2, rwkv_k_k, rwkv_k_a, rwkv_r_k, rwkv_gn_g, rwkv_gn_b, rwkv_proj, conv_w, conv_b, conv_ln_g, conv_ln_b, conv_proj, lru_conv_w, lru_conv_b, lru_wr, lru_br, lru_wi, lru_bi, lru_lambda, lru_proj, ret_gn_g, ret_gn_b, ret_proj, w_out, w_router, b_router, e_w1, e_w3, e_w2, final_g):
    bsz, n_tok, d = x.shape
    tc = ctx.shape[1]
    seq = tc + n_tok
    depth = w_in.shape[0]
    per_layer = {
        'rwkv_w0': rwkv_w0, 'rwkv_w2': rwkv_w2, 'rwkv_a0': rwkv_a0, 'rwkv_a2': rwkv_a2, 'rwkv_g2': rwkv_g2,
        'rwkv_k_k': rwkv_k_k, 'rwkv_k_a': rwkv_k_a, 'rwkv_r_k': rwkv_r_k,
        'rwkv_gn_g': rwkv_gn_g, 'rwkv_gn_b': rwkv_gn_b,
        'conv_w': conv_w, 'conv_b': conv_b, 'conv_ln_g': conv_ln_g, 'conv_ln_b': conv_ln_b,
        'lru_conv_w': lru_conv_w, 'lru_conv_b': lru_conv_b, 'lru_wr': lru_wr, 'lru_br': lru_br,
        'lru_wi': lru_wi, 'lru_bi': lru_bi, 'lru_lambda': lru_lambda,
        'ret_gn_g': ret_gn_g, 'ret_gn_b': ret_gn_b,
    }
    tm = _pick_tile(seq, 576)

    z = jnp.concatenate([ctx, x], axis=1).reshape(bsz * seq, d)
    for layer in range(depth):
        lp = {name: arr[layer] for name, arr in per_layer.items()}
        last = layer == depth - 1
        cc = jnp.concatenate([c, c_ctx[None, :]], axis=0)
        n_mod = -(-cc.shape[0] // 8) * 8
        cc = jnp.pad(cc, ((0, n_mod - cc.shape[0]), (0, 0)))
        mod = _fused_mm([cc], [], w_ada[layer].astype(BF16), lambda i, cb: cb * jax.nn.sigmoid(cb),
                        tm=n_mod, tn=6 * d // 4,
                        epilogue=lambda i, acc, bb: acc + bb,
                        epi_aux=[(b_ada[layer][None, :], (1, 6 * d // 4), lambda i, j: (0, j))])
        mod_l = mod[:bsz].reshape(bsz, 1, 6 * d)
        mod_c = mod[bsz:bsz + 1]
        sh1, sc1, g1, sh2, sc2, g2 = [mod_l[:, :, n * d:(n + 1) * d] for n in range(6)]
        csh1, csc1, cg1, csh2, csc2, cg2 = [mod_c[:, n * d:(n + 1) * d] for n in range(6)]

        w_l = w_in[layer].astype(BF16)
        norm_args = (norm1_g[layer][None, :], sh1, sc1, csh1, csc1)
        p_rwkv = _in_proj(z, seq, tc, *norm_args, w_l[:, :RWKV_WIDTH], 640,
                          rwkv_mu_prev[layer][None, :], rwkv_mu_next[layer][None, :])
        p_mid = _in_proj(z, seq, tc, *norm_args, w_l[:, RWKV_WIDTH:RWKV_WIDTH + MID_WIDTH], 512)
        p_merge = _in_proj(z, seq, tc, *norm_args, w_l[:, RWKV_WIDTH + MID_WIDTH:], 512, out_dtype=BF16)

        y_f, y_b = _rwkv_mix(p_rwkv, seq, tc, lp)
        o_f, o_b = _ret_mix(p_mid, seq, tc, n_tok)
        branches = [
            _rwkv_out_proj(y_f, y_b, p_rwkv, lp, rwkv_proj[layer], tm),
            _plain_mm(_conv_mix(p_mid, seq, tc, lp), conv_proj[layer].astype(BF16), tm=tm, tn=d, out_dtype=BF16),
            _plain_mm(_lru_mix(p_mid, seq, tc, lp), lru_proj[layer].astype(BF16), tm=tm, tn=d, out_dtype=BF16),
            _ret_out_proj(o_f, o_b, p_mid, lp, ret_proj[layer], tm),
        ]
        z = _merge_out(z, p_merge, branches, b_merge[layer], w_out[layer], g1, cg1, seq, tc, d)

        z3 = z.reshape(bsz, seq, d)
        zn = z3 * lax.rsqrt(jnp.mean(z3 * z3, axis=-1, keepdims=True) + RMS_EPS) * norm2_g[layer]
        is_ctx = (jnp.arange(seq) < tc)[None, :, None]
        h2 = zn * (1.0 + jnp.where(is_ctx, csc2[None], sc2)) + jnp.where(is_ctx, csh2[None], sh2)
        gate2 = jnp.where(is_ctx, cg2[None], g2)
        w1b, w3b, w2b = e_w1[layer], e_w3[layer], e_w2[layer]
        if not last:
            y = _moe(h2.reshape(-1, d), w_router, b_router, w1b, w3b, w2b).reshape(bsz, seq, d)
            z = (z3 + gate2 * y).reshape(bsz * seq, d)
        else:
            y = _moe(h2[:, tc:].reshape(-1, d), w_router, b_router, w1b, w3b, w2b).reshape(bsz, n_tok, d)
            xl = z3[:, tc:] + g2 * y
            return xl * lax.rsqrt(jnp.mean(xl * xl, axis=-1, keepdims=True) + RMS_EPS) * final_g
    return None
```

```python
import functools
import math

import jax
import jax.numpy as jnp
from jax import lax
from jax.experimental import pallas as pl
from jax.experimental.pallas import tpu as pltpu

F32 = jnp.float32
BF16 = jnp.bfloat16

D_MODEL = 1024
GRID_W = 64
N_BRANCH = 4
BR_W = D_MODEL // 2
RWKV_HS = 64
RWKV_H = BR_W // RWKV_HS
RWKV_W_LORA = 64
RWKV_A_LORA = 64
RWKV_G_LORA = 128
RWKV_GN_EPS = 64e-5
L2_EPS = 1e-12
CONV_K = 31
LRU_BLOCKS = 8
LRU_BS = BR_W // LRU_BLOCKS
LRU_CONV = 4
LRU_C = 8.0
RET_H = 4
RET_DK = 64
RET_DV = BR_W // RET_H
RET_CHUNK = 128
ROPE_BASE = 10000.0
N_EXPERTS = 32
N_GROUPS = 4
EXPERTS_PER_GROUP = N_EXPERTS // N_GROUPS
TOP_K = 2
D_EXPERT = D_MODEL // 2
RMS_EPS = 1e-6
LN_EPS = 1e-5

REST_COLS = (('conv_val', BR_W), ('conv_gate', BR_W), ('lru_x', BR_W), ('lru_gate', BR_W),
             ('ret_q', RET_H * RET_DK), ('ret_k', RET_H * RET_DK), ('ret_v', BR_W), ('ret_g', BR_W))
RWKV_WIDTH = 3 * BR_W + 2 * RWKV_W_LORA + 2 * RWKV_A_LORA + RWKV_G_LORA
LORA_OFF = 3 * BR_W
LORA_W = 2 * RWKV_W_LORA + 2 * RWKV_A_LORA
MID_WIDTH = sum(w for _, w in REST_COLS)
MERGE_WIDTH = N_BRANCH * D_MODEL

VMEM_LIMIT = 48 * 1024 * 1024
RWKV_CHUNK = 64
RWKV_TB = 256
MOE_BM = 256


def _bdot(a, b):
    return jnp.dot(a.astype(BF16), b.astype(BF16), preferred_element_type=F32)


_NT = (((1,), (1,)), ((), ()))
_TN = (((0,), (0,)), ((), ()))


def _pick_tile(n, cap, mult=8):
    best = None
    for t in range(mult, cap + 1, mult):
        if n % t == 0:
            best = t
    assert best is not None, (n, cap, mult)
    return best


def _fused_mm_body(*refs, nr, na, ner, nea, prologue, epilogue, out_dtype):
    rows = refs[:nr]
    auxs = refs[nr:nr + na]
    w_ref = refs[nr + na]
    er = refs[nr + na + 1:nr + na + 1 + ner]
    ea = refs[nr + na + 1 + ner:nr + na + 1 + ner + nea]
    o_ref, act = refs[-2], refs[-1]
    i = pl.program_id(0)

    @pl.when(pl.program_id(1) == 0)
    def _():
        act[...] = prologue(i, *[r[...] for r in rows], *[a[...] for a in auxs]).astype(BF16)

    acc = jnp.dot(act[...], w_ref[...], preferred_element_type=F32)
    if epilogue is not None:
        acc = epilogue(i, acc, *[r[...] for r in er], *[a[...] for a in ea])
    o_ref[...] = acc.astype(out_dtype)


def _fused_mm(row_ins, aux_ins, w, prologue, *, tm, tn, out_dtype=F32, epilogue=None,
              epi_rows=(), epi_aux=()):
    row_ins = [r if isinstance(r, tuple) else (r, r.shape[1], 0) for r in row_ins]
    n = row_ins[0][0].shape[0]
    k, m = w.shape
    assert n % tm == 0 and m % tn == 0, (n, tm, m, tn)
    in_specs = [pl.BlockSpec((tm, bw), functools.partial(lambda i, j, cb: (i, cb), cb=cb))
                for (_, bw, cb) in row_ins]
    in_specs += [pl.BlockSpec(bs, im) for (_, bs, im) in aux_ins]
    in_specs += [pl.BlockSpec((k, tn), lambda i, j: (0, j))]
    in_specs += [pl.BlockSpec((tm, tn), lambda i, j: (i, j)) for _ in epi_rows]
    in_specs += [pl.BlockSpec(bs, im) for (_, bs, im) in epi_aux]
    body = functools.partial(_fused_mm_body, nr=len(row_ins), na=len(aux_ins), ner=len(epi_rows),
                             nea=len(epi_aux), prologue=prologue, epilogue=epilogue, out_dtype=out_dtype)
    return pl.pallas_call(
        body,
        grid=(n // tm, m // tn),
        in_specs=in_specs,
        out_specs=pl.BlockSpec((tm, tn), lambda i, j: (i, j)),
        out_shape=jax.ShapeDtypeStruct((n, m), out_dtype),
        scratch_shapes=[pltpu.VMEM((tm, k), BF16)],
        compiler_params=pltpu.CompilerParams(dimension_semantics=("parallel", "arbitrary"),
                                             vmem_limit_bytes=VMEM_LIMIT),
    )(*[a for (a, _, _) in row_ins], *[a for (a, _, _) in aux_ins], w, *epi_rows,
      *[a for (a, _, _) in epi_aux])


def _whole(a):
    nd = a.ndim
    return (a, a.shape, lambda i, j: (0,) * nd)


def _plain_mm(x, w, *, tm, tn, out_dtype=F32):
    return _fused_mm([x], [], w, lambda i, xb: xb, tm=tm, tn=tn, out_dtype=out_dtype)


def _in_proj_body(*refs, tc, shift):
    if shift:
        z_ref, g_ref, shl_ref, scl_ref, shc_ref, scc_ref, w_ref, mup_ref, mun_ref, o_ref, act = refs
    else:
        z_ref, g_ref, shl_ref, scl_ref, shc_ref, scc_ref, w_ref, o_ref, act = refs
    seq = z_ref.shape[0]
    row = lax.broadcasted_iota(jnp.int32, (seq, 1), 0)

    @pl.when(pl.program_id(1) == 0)
    def _():
        xb = z_ref[...]
        y = xb * lax.rsqrt(jnp.mean(xb * xb, axis=-1, keepdims=True) + RMS_EPS) * g_ref[...]
        is_ctx = row < tc
        sc = jnp.where(is_ctx, scc_ref[...], scl_ref[0])
        sh = jnp.where(is_ctx, shc_ref[...], shl_ref[0])
        act[...] = (y * (1.0 + sc) + sh).astype(BF16)

    acc = jnp.dot(act[...], w_ref[...], preferred_element_type=F32)
    if shift:
        prev = pltpu.roll(acc, 1, 0)
        nxt = pltpu.roll(acc, seq - 1, 0)
        prev = jnp.where((row == 0) | (row == tc), 0.0, prev)
        nxt = jnp.where((row == tc - 1) | (row == seq - 1), 0.0, nxt)
        acc = acc + mup_ref[...] * (prev - acc) + mun_ref[...] * (nxt - acc)
    o_ref[...] = acc.astype(o_ref.dtype)


def _in_proj(z, seq, tc, g, sh_l, sc_l, sh_c, sc_c, w, tn, mu_prev=None, mu_next=None, out_dtype=F32):
    n, d = z.shape
    m = w.shape[1]
    shift = mu_prev is not None
    assert m % tn == 0
    vec = lambda width: pl.BlockSpec((1, width), lambda b, j: (0, 0))
    per_b = pl.BlockSpec((1, 1, d), lambda b, j: (b, 0, 0))
    in_specs = [pl.BlockSpec((seq, d), lambda b, j: (b, 0)), vec(d), per_b, per_b, vec(d), vec(d),
                pl.BlockSpec((d, tn), lambda b, j: (0, j))]
    args = [z, g, sh_l, sc_l, sh_c, sc_c, w]
    if shift:
        in_specs += [pl.BlockSpec((1, tn), lambda b, j: (0, j))] * 2
        args += [mu_prev, mu_next]
    return pl.pallas_call(
        functools.partial(_in_proj_body, tc=tc, shift=shift),
        grid=(n // seq, m // tn),
        in_specs=in_specs,
        out_specs=pl.BlockSpec((seq, tn), lambda b, j: (b, j)),
        out_shape=jax.ShapeDtypeStruct((n, m), out_dtype),
        scratch_shapes=[pltpu.VMEM((seq, d), BF16)],
        compiler_params=pltpu.CompilerParams(dimension_semantics=("parallel", "arbitrary"),
                                             vmem_limit_bytes=VMEM_LIMIT),
    )(*args)


def _rwkv_body(rf, kf, vf, lf, rb_, kb, vb_, lb, w0_ref, w2_ref, a0_ref, a2_ref, kkw_ref, ka_ref, bd_ref,
               of_ref, ob_ref, s_ref, kk_ref, *, n_chunks):
    c_len = RWKV_CHUNK

    @pl.when(pl.program_id(1) == 0)
    def _():
        s_ref[...] = jnp.zeros_like(s_ref)

    for d, k_blk in enumerate((kf, kb)):
        kx = k_blk[...].astype(F32) * kkw_ref[...]
        kk_ref[d] = kx * lax.rsqrt(_bdot(kx * kx, bd_ref[...]) + L2_EPS)

    ri = lax.broadcasted_iota(jnp.int32, (c_len, c_len), 0)
    ci = lax.broadcasted_iota(jnp.int32, (c_len, c_len), 1)
    eye = (ri == ci).astype(F32)
    in_refs = ((rf, kf, vf, lf), (rb_, kb, vb_, lb))
    masks = []
    for d in range(2):
        before = (ri > ci) if d == 0 else (ri < ci)
        strict = before.astype(F32)
        incl = (before | (ri == ci)).astype(F32)
        m8 = jnp.where(((ri // 8) == (ci // 8)) & before, 1.0, 0.0).astype(F32)
        merges = []
        for size in (16, 32, 64):
            same = (ri // size) == (ci // size)
            inner = (ri // (size // 2)) == (ci // (size // 2))
            merges.append(jnp.where(same & jnp.logical_not(inner) & before, 1.0, 0.0).astype(F32))
        masks.append(dict(strict=strict, incl=incl, incl_b=incl.astype(BF16), m8=m8, merges=merges))
    out_refs = (of_ref, ob_ref)
    decay_scale = math.exp(-0.5)

    def chunk(c, carry):
        per_dir = []
        for d in range(2):
            cc = c if d == 0 else n_chunks - 1 - c
            sl = pl.ds(pl.multiple_of(cc * c_len, c_len), c_len)
            r_ref, k_ref, v_ref, lo_ref = in_refs[d]
            r = r_ref[sl, :].astype(F32)
            k = k_ref[sl, :].astype(F32)
            v = v_ref[sl, :].astype(F32)
            lo = lo_ref[sl, :].astype(F32)
            wlo = lo[:, d * RWKV_W_LORA:(d + 1) * RWKV_W_LORA]
            alo = lo[:, 2 * RWKV_W_LORA + d * RWKV_A_LORA:2 * RWKV_W_LORA + (d + 1) * RWKV_A_LORA]
            w_raw = w0_ref[d] + _bdot(jnp.tanh(wlo), w2_ref[d])
            lw = -decay_scale * jax.nn.sigmoid(w_raw)
            a = jax.nn.sigmoid(a0_ref[d] + _bdot(alo, a2_ref[d]))
            k_d = k * (1.0 + (a - 1.0) * ka_ref[...])
            kk = kk_ref[d, sl, :]
            hi = lw.astype(BF16)
            lo2 = (lw - hi.astype(F32)).astype(BF16)
            tri = masks[d]['incl_b']
            cum = (jnp.dot(tri, hi, preferred_element_type=F32)
                   + jnp.dot(tri, lo2, preferred_element_type=F32))
            eg = jnp.exp(cum)
            ieg = jnp.exp(-cum)
            last = c_len - 1 if d == 0 else 0
            per_dir.append(dict(
                rt=(r * eg).astype(BF16), kt=(k_d * ieg).astype(BF16), at=(-(kk * a) * ieg).astype(BF16),
                bt=(kk * jnp.exp(cum - lw)).astype(BF16), vb=v.astype(BF16), g_last=eg[last:last + 1, :],
                rows=sl))
        chains = [(d, h) for d in range(2) for h in range(RWKV_H)]
        hsl = lambda h: slice(h * RWKV_HS, (h + 1) * RWKV_HS)
        rb = [jnp.concatenate([per_dir[d]['rt'][:, hsl(h)], per_dir[d]['bt'][:, hsl(h)]], axis=0)
              for d, h in chains]
        at_h = [per_dir[d]['at'][:, hsl(h)] for d, h in chains]
        kt_h = [per_dir[d]['kt'][:, hsl(h)] for d, h in chains]
        v_h = [per_dir[d]['vb'][:, hsl(h)] for d, h in chains]
        mk = [masks[d] for d, h in chains]
        n = len(chains)
        m_a = [lax.dot_general(rb[i], at_h[i], _NT, preferred_element_type=F32) for i in range(n)]
        m_k = [lax.dot_general(rb[i], kt_h[i], _NT, preferred_element_type=F32) for i in range(n)]
        ra = [m_a[i][:c_len] * mk[i]['incl'] for i in range(n)]
        nmat = [m_a[i][c_len:] * mk[i]['strict'] for i in range(n)]
        rk = [m_k[i][:c_len] * mk[i]['incl'] for i in range(n)]
        bk = [m_k[i][c_len:] * mk[i]['strict'] for i in range(n)]
        d1 = [nmat[i] * mk[i]['m8'] for i in range(n)]
        d2 = [_bdot(d1[i], d1[i]) for i in range(n)]
        d4 = [_bdot(d2[i], d2[i]) for i in range(n)]
        tinv = [eye + d1[i] for i in range(n)]
        tinv = [tinv[i] + _bdot(tinv[i], d2[i]) for i in range(n)]
        tinv = [tinv[i] + _bdot(tinv[i], d4[i]) for i in range(n)]
        for lvl in range(3):
            e = [_bdot(tinv[i], nmat[i] * mk[i]['merges'][lvl]) for i in range(n)]
            tinv = [tinv[i] + _bdot(e[i], tinv[i]) for i in range(n)]
        s0 = [s_ref[d, h] for d, h in chains]
        rbs = [lax.dot_general(rb[i], s0[i].astype(BF16), _NT, preferred_element_type=F32) for i in range(n)]
        x = [rbs[i][c_len:] + _bdot(bk[i], v_h[i]) for i in range(n)]
        u = [_bdot(tinv[i], x[i]) for i in range(n)]
        y = [rbs[i][:c_len] + _bdot(ra[i], u[i]) + _bdot(rk[i], v_h[i]) for i in range(n)]
        for i, (d, h) in enumerate(chains):
            uv = jnp.concatenate([u[i].astype(BF16), v_h[i]], axis=0)
            ak = jnp.concatenate([at_h[i], kt_h[i]], axis=0)
            ds = lax.dot_general(uv, ak, _TN, preferred_element_type=F32)
            s_ref[d, h] = (s0[i] + ds) * per_dir[d]['g_last'][:, hsl(h)]
        for d in range(2):
            out_refs[d][per_dir[d]['rows'], :] = jnp.concatenate(
                [y[d * RWKV_H + h] for h in range(RWKV_H)], axis=1)
        return carry

    lax.fori_loop(0, n_chunks, chunk, 0)


def _rwkv_mix(p_rwkv, seq, tc, lp):
    n = p_rwkv.shape[0]
    bsz = n // seq
    tb = _pick_tile(math.gcd(seq, tc), RWKV_TB, RWKV_CHUNK)
    nblk, cblk = seq // tb, tc // tb
    assert LORA_OFF % LORA_W == 0

    def fwd_blk(t):
        return t

    def bwd_blk(t):
        return jnp.where(t < cblk, cblk - 1 - t, nblk - 1 - (t - cblk))

    def col_spec(width, cb, blk_fn):
        return pl.BlockSpec((tb, width), lambda b, t: (b * nblk + blk_fn(t), cb))

    in_specs = []
    for blk_fn in (fwd_blk, bwd_blk):
        in_specs += [col_spec(BR_W, 0, blk_fn), col_spec(BR_W, 1, blk_fn), col_spec(BR_W, 2, blk_fn),
                     col_spec(LORA_W, LORA_OFF // LORA_W, blk_fn)]
    whole = lambda a: pl.BlockSpec(a.shape, lambda b, t: (0,) * a.ndim)
    head_id = jnp.arange(BR_W) // RWKV_HS
    bd = (head_id[:, None] == head_id[None, :]).astype(BF16)
    consts = [lp['rwkv_w0'][:, None, :], lp['rwkv_w2'].astype(BF16), lp['rwkv_a0'][:, None, :],
              lp['rwkv_a2'].astype(BF16), lp['rwkv_k_k'][None, :], lp['rwkv_k_a'][None, :], bd]
    in_specs += [whole(a) for a in consts]
    return pl.pallas_call(
        functools.partial(_rwkv_body, n_chunks=tb // RWKV_CHUNK),
        grid=(bsz, nblk),
        in_specs=in_specs,
        out_specs=[col_spec(BR_W, 0, fwd_blk), col_spec(BR_W, 0, bwd_blk)],
        out_shape=[jax.ShapeDtypeStruct((n, BR_W), F32)] * 2,
        scratch_shapes=[pltpu.VMEM((2, RWKV_H, RWKV_HS, RWKV_HS), F32), pltpu.VMEM((2, tb, BR_W), F32)],
        compiler_params=pltpu.CompilerParams(
            dimension_semantics=("parallel", "arbitrary"), vmem_limit_bytes=VMEM_LIMIT),
    )(*([p_rwkv] * 8), *consts)


def _rwkv_out_proj(y_f, y_b, p_rwkv, lp, w_proj, tm):
    head_id = jnp.arange(BR_W) // RWKV_HS
    bd = (head_id[:, None] == head_id[None, :]).astype(BF16)

    def hsum(xv, bdm):
        hi = xv.astype(BF16)
        lo = (xv - hi.astype(F32)).astype(BF16)
        return (jnp.dot(hi, bdm, preferred_element_type=F32) + jnp.dot(lo, bdm, preferred_element_type=F32))

    def act(i, yf, yb, r, k, v, glo, gn_g, gn_b, r_k, g2, bdm):
        inv = 1.0 / RWKV_HS
        r, k, v, glo = [t.astype(F32) for t in (r, k, v, glo)]
        yb = yf + yb
        yc = yb - hsum(yb, bdm) * inv
        var = hsum(yc * yc, bdm) * inv
        yn = yc * lax.rsqrt(var + RWKV_GN_EPS) * gn_g + gn_b
        bonus = hsum(r * k * r_k, bdm) * v
        g = _bdot(jax.nn.sigmoid(glo), g2)
        return (yn + bonus) * g

    glo_cb = (LORA_OFF + LORA_W) // RWKV_G_LORA
    assert (LORA_OFF + LORA_W) % RWKV_G_LORA == 0
    return _fused_mm(
        [y_f, y_b, (p_rwkv, BR_W, 0), (p_rwkv, BR_W, 1), (p_rwkv, BR_W, 2), (p_rwkv, RWKV_G_LORA, glo_cb)],
        [_whole(lp['rwkv_gn_g'][None, :]), _whole(lp['rwkv_gn_b'][None, :]), _whole(lp['rwkv_r_k'][None, :]),
         _whole(lp['rwkv_g2'].astype(BF16)), _whole(bd)],
        w_proj.astype(BF16), act, tm=tm, tn=w_proj.shape[1], out_dtype=BF16)


LRU_TB = 256
LRU_HALO = 16


def _lru_body(x_ref, g_ref, cw_ref, cb_ref, wr_ref, br_ref, wi_ref, bi_ref, sp_ref, o_ref, h_ref, *,
              tb, nblk, cblk):
    seq = x_ref.shape[0]
    row = lax.broadcasted_iota(jnp.int32, (tb, 1), 0)
    n_ext = tb + LRU_HALO
    for d in range(2):
        h_ref[...] = jnp.zeros_like(h_ref)

        def blk_body(i, carry, d=d):
            if d == 0:
                blk = i
                halo_ok = (blk != 0) & (blk != cblk)
                halo0 = jnp.maximum(blk * tb - LRU_HALO, 0)
            else:
                blk = jnp.where(i < cblk, cblk - 1 - i, nblk - 1 - (i - cblk))
                halo_ok = (blk != cblk - 1) & (blk != nblk - 1)
                halo0 = jnp.minimum(blk * tb + tb, seq - LRU_HALO)
            rows = pl.ds(pl.multiple_of(blk * tb, tb), tb)
            cur = x_ref[rows, :].astype(F32)
            halo = (x_ref[pl.ds(pl.multiple_of(halo0, LRU_HALO), LRU_HALO), :].astype(F32)
                    * jnp.where(halo_ok, 1.0, 0.0))
            xc = cb_ref[d]
            if d == 0:
                ext = jnp.concatenate([halo, cur], axis=0)
                for j in range(LRU_CONV):
                    sh = LRU_CONV - 1 - j
                    tap = ext if sh == 0 else pltpu.roll(ext, sh, 0)
                    xc = xc + cw_ref[d, j:j + 1, :] * tap[LRU_HALO:, :]
            else:
                ext = jnp.concatenate([cur, halo], axis=0)
                for j in range(LRU_CONV):
                    tap = ext if j == 0 else pltpu.roll(ext, n_ext - j, 0)
                    xc = xc + cw_ref[d, j:j + 1, :] * tap[:tb, :]
            gate_r = jax.nn.sigmoid(_bdot(xc, wr_ref[d]) + br_ref[d])
            gate_i = jax.nn.sigmoid(_bdot(xc, wi_ref[d]) + bi_ref[d])
            log_a = -LRU_C * gate_r * sp_ref[d]
            a_cum = jnp.exp(log_a)
            b_cum = jnp.sqrt(1.0 - jnp.exp(2.0 * log_a)) * (gate_i * xc)
            s = 1
            while s < tb:
                if d == 0:
                    ok = row >= s
                    a_sh = jnp.where(ok, pltpu.roll(a_cum, s, 0), 1.0)
                    b_sh = jnp.where(ok, pltpu.roll(b_cum, s, 0), 0.0)
                else:
                    ok = row < tb - s
                    a_sh = jnp.where(ok, pltpu.roll(a_cum, tb - s, 0), 1.0)
                    b_sh = jnp.where(ok, pltpu.roll(b_cum, tb - s, 0), 0.0)
                b_cum = a_cum * b_sh + b_cum
                a_cum = a_cum * a_sh
                s *= 2
            h = b_cum + a_cum * h_ref[...]
            if d == 0:
                h_ref[...] = h[tb - 1:tb, :]
                o_ref[rows, :] = h
            else:
                h_ref[...] = h[0:1, :]
                o_ref[rows, :] = (o_ref[rows, :] + h) * jax.nn.gelu(g_ref[rows, :].astype(F32))
            return carry

        lax.fori_loop(0, nblk, blk_body, 0)


def _lru_mix(p_mid, seq, tc, lp):
    n = p_mid.shape[0]
    tb = _pick_tile(math.gcd(seq, tc), LRU_TB)
    nblk, cblk = seq // tb, tc // tb
    eye = jnp.eye(LRU_BLOCKS, dtype=F32)
    blockdiag = lambda w: jnp.einsum('dgij,gh->dgihj', w, eye).reshape(2, BR_W, BR_W).astype(BF16)
    consts = [lp['lru_conv_w'], lp['lru_conv_b'][:, None, :], blockdiag(lp['lru_wr']), lp['lru_br'][:, None, :],
              blockdiag(lp['lru_wi']), lp['lru_bi'][:, None, :], jax.nn.softplus(-lp['lru_lambda'])[:, None, :]]
    whole = lambda a: pl.BlockSpec(a.shape, lambda b: (0,) * a.ndim)
    x_cb = [name for name, _ in REST_COLS].index('lru_x')
    g_cb = [name for name, _ in REST_COLS].index('lru_gate')
    assert all(w == BR_W for _, w in REST_COLS[:max(x_cb, g_cb) + 1])
    return pl.pallas_call(
        functools.partial(_lru_body, tb=tb, nblk=nblk, cblk=cblk),
        grid=(n // seq,),
        in_specs=[pl.BlockSpec((seq, BR_W), lambda b: (b, x_cb)), pl.BlockSpec((seq, BR_W), lambda b: (b, g_cb))]
        + [whole(a) for a in consts],
        out_specs=pl.BlockSpec((seq, BR_W), lambda b: (b, 0)),
        out_shape=jax.ShapeDtypeStruct((n, BR_W), F32),
        scratch_shapes=[pltpu.VMEM((1, BR_W), F32)],
        compiler_params=pltpu.CompilerParams(dimension_semantics=("parallel",), vmem_limit_bytes=VMEM_LIMIT),
    )(p_mid, p_mid, *consts)


CONV_TB = 256
CONV_HALO = 16


def _conv_body(val_ref, gate_ref, w_ref, b_ref, lng_ref, lnb_ref, o_ref, *, tb, nblk, cblk):
    seq = val_ref.shape[0]
    n_ext = tb + 2 * CONV_HALO
    pad = CONV_K // 2

    def glu(rows):
        return val_ref[rows, :].astype(F32) * jax.nn.sigmoid(gate_ref[rows, :].astype(F32))

    def blk_body(blk, carry):
        t0 = blk * tb
        rows = pl.ds(pl.multiple_of(t0, tb), tb)
        lo_ok = (blk != 0) & (blk != cblk)
        hi_ok = (blk != cblk - 1) & (blk != nblk - 1)
        lo0 = jnp.maximum(t0 - CONV_HALO, 0)
        hi0 = jnp.minimum(t0 + tb, seq - CONV_HALO)
        lo = glu(pl.ds(pl.multiple_of(lo0, CONV_HALO), CONV_HALO)) * jnp.where(lo_ok, 1.0, 0.0)
        hi = glu(pl.ds(pl.multiple_of(hi0, CONV_HALO), CONV_HALO)) * jnp.where(hi_ok, 1.0, 0.0)
        ext = jnp.concatenate([lo, glu(rows), hi], axis=0)
        acc = jnp.zeros((tb, ext.shape[1]), F32) + b_ref[...]
        for r in range(8):
            rolled = ext if r == 0 else pltpu.roll(ext, n_ext - r, 0)
            for j in range(CONV_K):
                off = CONV_HALO - pad + j
                if off % 8 == r:
                    acc = acc + w_ref[j:j + 1, :] * rolled[off - r:off - r + tb, :]
        mu = jnp.mean(acc, axis=-1, keepdims=True)
        xc = acc - mu
        yn = xc * lax.rsqrt(jnp.mean(xc * xc, axis=-1, keepdims=True) + LN_EPS) * lng_ref[...] + lnb_ref[...]
        o_ref[rows, :] = yn * jax.nn.sigmoid(yn)
        return carry

    lax.fori_loop(0, nblk, blk_body, 0)


def _conv_mix(p_mid, seq, tc, lp):
    n = p_mid.shape[0]
    tb = _pick_tile(math.gcd(seq, tc), CONV_TB, CONV_HALO)
    names = [name for name, _ in REST_COLS]
    v_cb, g_cb = names.index('conv_val'), names.index('conv_gate')
    assert all(w == BR_W for _, w in REST_COLS[:max(v_cb, g_cb) + 1]) and CONV_HALO >= CONV_K // 2
    consts = [lp['conv_w'], lp['conv_b'][None, :], lp['conv_ln_g'][None, :], lp['conv_ln_b'][None, :]]
    whole = lambda a: pl.BlockSpec(a.shape, lambda b: (0,) * a.ndim)
    return pl.pallas_call(
        functools.partial(_conv_body, tb=tb, nblk=seq // tb, cblk=tc // tb),
        grid=(n // seq,),
        in_specs=[pl.BlockSpec((seq, BR_W), lambda b: (b, v_cb)), pl.BlockSpec((seq, BR_W), lambda b: (b, g_cb))]
        + [whole(a) for a in consts],
        out_specs=pl.BlockSpec((seq, BR_W), lambda b: (b, 0)),
        out_shape=jax.ShapeDtypeStruct((n, BR_W), F32),
        compiler_params=pltpu.CompilerParams(dimension_semantics=("parallel",), vmem_limit_bytes=VMEM_LIMIT),
    )(p_mid, p_mid, *consts)


RET_TB = 256
_RET_LOG_G = [math.log1p(-2.0 ** (-5.0 - h)) for h in range(RET_H)]


def _ret_body(qf, kf, vf, cf, sf, qb, kb, vb, cb, sb, dm_ref, xi_ref, zt_ref, of_ref, ob_ref, s_ref, *, n_chunks):
    c_len = RET_CHUNK
    qk_w = RET_H * RET_DK

    @pl.when(pl.program_id(1) == 0)
    def _():
        s_ref[...] = jnp.zeros_like(s_ref)

    lane = lax.broadcasted_iota(jnp.int32, (c_len, qk_w), 1)
    first_half = (lane % RET_DK) < (RET_DK // 2)

    def rotary(z, cos, sin):
        swapped = jnp.where(first_half, pltpu.roll(z, qk_w - RET_DK // 2, 1), pltpu.roll(z, RET_DK // 2, 1))
        return z * cos + swapped * sin

    in_refs = ((qf, kf, vf, cf, sf), (qb, kb, vb, cb, sb))
    out_refs = (of_ref, ob_ref)

    def chunk(c, carry):
        per_dir = []
        for d in range(2):
            cc = c if d == 0 else n_chunks - 1 - c
            sl = pl.ds(pl.multiple_of(cc * c_len, c_len), c_len)
            q_ref, k_ref, v_ref, c_ref, sn_ref = in_refs[d]
            cos, sin = c_ref[sl, :], sn_ref[sl, :]
            q = rotary(q_ref[sl, :].astype(F32), cos, sin)
            k = rotary(k_ref[sl, :].astype(F32), cos, sin) * (RET_DK ** -0.5)
            per_dir.append(dict(q=q.astype(BF16), k=k.astype(BF16), kz=(k * zt_ref[d]).astype(BF16),
                                v=v_ref[sl, :].astype(BF16), sl=sl))
        groups = [(d, h) for d in range(2) for h in range(RET_H)]
        ksl = lambda h: slice(h * RET_DK, (h + 1) * RET_DK)
        vsl = lambda h: slice(h * RET_DV, (h + 1) * RET_DV)
        q_h = [per_dir[d]['q'][:, ksl(h)] for d, h in groups]
        v_h = [per_dir[d]['v'][:, vsl(h)] for d, h in groups]
        att = [lax.dot_general(q_h[i], per_dir[d]['k'][:, ksl(h)], _NT, preferred_element_type=F32) * dm_ref[d, h]
               for i, (d, h) in enumerate(groups)]
        s0 = [s_ref[d, h] for d, h in groups]
        o = [_bdot(att[i], v_h[i]) + _bdot(q_h[i], s0[i]) * xi_ref[d][:, vsl(h)]
             for i, (d, h) in enumerate(groups)]
        for i, (d, h) in enumerate(groups):
            kv = lax.dot_general(per_dir[d]['kz'][:, ksl(h)], v_h[i], _TN, preferred_element_type=F32)
            s_ref[d, h] = s0[i] * math.exp(_RET_LOG_G[h] * c_len) + kv
        for d in range(2):
            out_refs[d][per_dir[d]['sl'], :] = jnp.concatenate([o[d * RET_H + h] for h in range(RET_H)], axis=1)
        return carry

    lax.fori_loop(0, n_chunks, chunk, 0)


def _ret_mix(p_mid, seq, tc, n_tok):
    n = p_mid.shape[0]
    tb = _pick_tile(math.gcd(seq, tc), RET_TB, RET_CHUNK)
    nblk, cblk = seq // tb, tc // tb
    c_len = RET_CHUNK
    n_rows = n_tok // GRID_W
    rows = jnp.repeat(jnp.arange(n_rows, dtype=F32), GRID_W)
    cols = jnp.tile(jnp.arange(GRID_W, dtype=F32), n_rows)
    n_freq = RET_DK // 4
    inv = ROPE_BASE ** (-jnp.arange(n_freq, dtype=F32) / n_freq)
    ang = jnp.concatenate([rows[:, None] * inv, cols[:, None] * inv], axis=-1)
    cos_h = jnp.concatenate([jnp.cos(ang), jnp.cos(ang)], axis=-1)
    sin_h = jnp.concatenate([-jnp.sin(ang), jnp.sin(ang)], axis=-1)
    pad_ctx = lambda tbl, fill: jnp.concatenate(
        [jnp.full((tc, RET_H * RET_DK), fill, F32), jnp.tile(tbl, (1, RET_H))], axis=0)
    cos_t, sin_t = pad_ctx(cos_h, 1.0), pad_ctx(sin_h, 0.0)
    log_g = jnp.asarray(_RET_LOG_G, F32)
    idx = jnp.arange(c_len, dtype=F32)
    dm, xi, zt = [], [], []
    for d in range(2):
        pos = idx if d == 0 else c_len - 1.0 - idx
        diff = pos[:, None] - pos[None, :]
        keep = diff >= 0 if d == 0 else diff > 0
        dm.append(jnp.where(keep[None], jnp.exp(log_g[:, None, None] * jnp.maximum(diff, 0.0)[None]), 0.0))
        xi.append(jnp.repeat(jnp.exp(log_g[None, :] * (pos[:, None] + 1.0)), RET_DV, axis=1))
        zt.append(jnp.repeat(jnp.exp(log_g[None, :] * (c_len - 1.0 - pos)[:, None]), RET_DK, axis=1))
    consts = [jnp.stack(dm), jnp.stack(xi), jnp.stack(zt)]

    def fwd_blk(t):
        return t

    def bwd_blk(t):
        return jnp.where(t < cblk, cblk - 1 - t, nblk - 1 - (t - cblk))

    names = [name for name, _ in REST_COLS]
    offs = {name: sum(w for _, w in REST_COLS[:i]) for i, (name, _) in enumerate(REST_COLS)}
    qk_w = RET_H * RET_DK
    in_specs = []
    for blk_fn in (fwd_blk, bwd_blk):
        spec = lambda width, cb, blk_fn=blk_fn: pl.BlockSpec((tb, width), lambda b, t: (b * nblk + blk_fn(t), cb))
        tbl = lambda blk_fn=blk_fn: pl.BlockSpec((tb, qk_w), lambda b, t: (blk_fn(t), 0))
        assert offs['ret_q'] % qk_w == 0 and offs['ret_k'] % qk_w == 0 and offs['ret_v'] % BR_W == 0
        in_specs += [spec(qk_w, offs['ret_q'] // qk_w), spec(qk_w, offs['ret_k'] // qk_w),
                     spec(BR_W, offs['ret_v'] // BR_W), tbl(), tbl()]
    whole = lambda a: pl.BlockSpec(a.shape, lambda b, t: (0,) * a.ndim)
    in_specs += [whole(a) for a in consts]
    out_spec = lambda blk_fn: pl.BlockSpec((tb, BR_W), lambda b, t: (b * nblk + blk_fn(t), 0))
    return pl.pallas_call(
        functools.partial(_ret_body, n_chunks=tb // c_len),
        grid=(n // seq, nblk),
        in_specs=in_specs,
        out_specs=[out_spec(fwd_blk), out_spec(bwd_blk)],
        out_shape=[jax.ShapeDtypeStruct((n, BR_W), F32)] * 2,
        scratch_shapes=[pltpu.VMEM((2, RET_H, RET_DK, RET_DV), F32)],
        compiler_params=pltpu.CompilerParams(
            dimension_semantics=("parallel", "arbitrary"), vmem_limit_bytes=VMEM_LIMIT),
    )(*([p_mid, p_mid, p_mid, cos_t, sin_t] * 2), *consts)


def _ret_out_proj(o_f, o_b, p_mid, lp, w_proj, tm):
    def act(i, of, ob, g, gn_g, gn_b):
        o = of + ob
        g = g.astype(F32)
        normed = []
        for h in range(RET_H):
            oh = o[:, h * RET_DV:(h + 1) * RET_DV]
            oc = oh - jnp.mean(oh, axis=-1, keepdims=True)
            normed.append(oc * lax.rsqrt(jnp.mean(oc * oc, axis=-1, keepdims=True) + LN_EPS))
        yn = jnp.concatenate(normed, axis=1) * gn_g + gn_b
        return g * jax.nn.sigmoid(g) * yn

    off_g = sum(w for _, w in REST_COLS[:[name for name, _ in REST_COLS].index('ret_g')])
    assert off_g % BR_W == 0
    return _fused_mm([o_f, o_b, (p_mid, BR_W, off_g // BR_W)],
                     [_whole(lp['ret_gn_g'][None, :]), _whole(lp['ret_gn_b'][None, :])],
                     w_proj.astype(BF16), act, tm=tm, tn=w_proj.shape[1], out_dtype=BF16)


def _moe_body(be_ref, nu_ref, x_ref, w1_ref, w3_ref, w2_ref, o_ref):
    i = pl.program_id(0)

    @pl.when(i < nu_ref[0])
    def _():
        xb = x_ref[...]
        h1 = jnp.dot(xb, w1_ref[0].astype(BF16), preferred_element_type=F32)
        h3 = jnp.dot(xb, w3_ref[0].astype(BF16), preferred_element_type=F32)
        hid = (h1 * jax.nn.sigmoid(h1) * h3).astype(BF16)
        o_ref[...] = jnp.dot(hid, w2_ref[0].astype(BF16), preferred_element_type=F32)

    @pl.when(i >= nu_ref[0])
    def _():
        o_ref[...] = jnp.zeros_like(o_ref)


def _moe_experts(xb, block_e, n_used, w1, w3, w2):
    n_rows, d = xb.shape
    n_blocks = n_rows // MOE_BM
    de = w1.shape[2]
    grid_spec = pltpu.PrefetchScalarGridSpec(
        num_scalar_prefetch=2,
        grid=(n_blocks,),
        in_specs=[
            pl.BlockSpec((MOE_BM, d), lambda i, be, nu: (i, 0)),
            pl.BlockSpec((1, d, de), lambda i, be, nu: (be[i], 0, 0)),
            pl.BlockSpec((1, d, de), lambda i, be, nu: (be[i], 0, 0)),
            pl.BlockSpec((1, de, d), lambda i, be, nu: (be[i], 0, 0)),
        ],
        out_specs=pl.BlockSpec((MOE_BM, d), lambda i, be, nu: (i, 0)),
    )
    return pl.pallas_call(
        _moe_body,
        grid_spec=grid_spec,
        out_shape=jax.ShapeDtypeStruct((n_rows, d), F32),
        compiler_params=pltpu.CompilerParams(dimension_semantics=("arbitrary",),
                                             vmem_limit_bytes=VMEM_LIMIT),
    )(block_e, n_used, xb, w1, w3, w2)


def _moe(tokens, w_router, b_router, w1, w3, w2):
    n_tok = tokens.shape[0]
    aff = jax.nn.sigmoid(jnp.dot(tokens, w_router, precision=lax.Precision.HIGHEST))
    sel = (aff + b_router).reshape(-1, N_GROUPS, EXPERTS_PER_GROUP)
    grp_score = jnp.sum(lax.top_k(sel, TOP_K)[0], axis=-1)
    grp = jnp.argmax(grp_score, axis=-1)
    grp_mask = jnp.arange(N_GROUPS)[None, :] == grp[:, None]
    masked = jnp.where(grp_mask[:, :, None], sel, -jnp.inf).reshape(-1, N_EXPERTS)
    _, e_idx = lax.top_k(masked, TOP_K)
    wts = jnp.take_along_axis(aff, e_idx, axis=1)
    wts = wts / jnp.sum(wts, axis=-1, keepdims=True)

    n_asg = n_tok * TOP_K
    flat_e = e_idx.reshape(-1).astype(jnp.int32)
    order = jnp.argsort(flat_e).astype(jnp.int32)
    se, st = flat_e[order], order // TOP_K
    counts = jnp.sum(flat_e[:, None] == jnp.arange(N_EXPERTS, dtype=jnp.int32)[None, :], axis=0, dtype=jnp.int32)
    starts = jnp.cumsum(counts) - counts
    padded = (counts + MOE_BM - 1) // MOE_BM * MOE_BM
    pad_end = jnp.cumsum(padded)
    pad_start = pad_end - padded
    dest = pad_start[se] + jnp.arange(n_asg, dtype=jnp.int32) - starts[se]
    dest_asg = dest[jnp.argsort(order)].reshape(n_tok, TOP_K)
    n_blocks = -(-n_asg // MOE_BM) + N_EXPERTS
    block_e = jnp.minimum(jnp.searchsorted(pad_end, jnp.arange(n_blocks) * MOE_BM, side='right'),
                          N_EXPERTS - 1).astype(jnp.int32)
    n_used = (pad_end[-1] // MOE_BM).astype(jnp.int32).reshape(1)
    slot_e = jnp.repeat(block_e, MOE_BM)
    rank = jnp.arange(n_blocks * MOE_BM, dtype=jnp.int32) - pad_start[slot_e]
    slot_tok = st[jnp.clip(starts[slot_e] + rank, 0, n_asg - 1)]
    xb = tokens.astype(BF16)[slot_tok]
    yb = _moe_experts(xb, block_e, n_used, w1, w3, w2)
    return wts[:, 0:1] * yb[dest_asg[:, 0]] + wts[:, 1:2] * yb[dest_asg[:, 1]]


def _merge_out(z, merge, branches, b_merge, w_out_l, g1, cg1, seq, tc, d):
    tm2 = _pick_tile(seq, 576, 16)
    bpb = seq // tm2

    def seg_select(i, ctx_val, lat_val):
        row = (i % bpb) * tm2 + lax.broadcasted_iota(jnp.int32, (tm2, 1), 0)
        return jnp.where(row < tc, ctx_val, lat_val)

    def merge_pro(i, mg, b0, b1, b2, b3, bm):
        gates = jax.nn.sigmoid(mg.astype(F32) + bm)
        b0, b1, b2, b3 = [b.astype(F32) for b in (b0, b1, b2, b3)]
        return (gates[:, 0 * d:1 * d] * b0 + gates[:, 1 * d:2 * d] * b1
                + gates[:, 2 * d:3 * d] * b2 + gates[:, 3 * d:4 * d] * b3)

    def resid_epi(i, acc, zb, gl, gc):
        return zb + seg_select(i, gc, gl[0]) * acc

    tn = d // 2
    return _fused_mm([merge] + branches, [_whole(b_merge[None, :])],
                     w_out_l.astype(BF16), merge_pro, tm=tm2, tn=tn,
                     epilogue=resid_epi, epi_rows=[z],
                     epi_aux=[(g1, (1, 1, tn), lambda i, j: (i // bpb, 0, j)),
                              (cg1, (1, tn), lambda i, j: (0, j))])


def kernel(x, c, ctx, c_ctx, w_ada, b_ada, norm1_g, norm2_g, w_in, b_merge, rwkv_mu_prev, rwkv_mu_next, rwkv_w0, rwkv_w2, rwkv_a0, rwkv_a2, rwkv_g2, rwkv_k_k, rwkv_k_a, rwkv_r_k, rwkv_gn_g, rwkv_gn_b, rwkv_proj, conv_w, conv_b, conv_ln_g, conv_ln_b, conv_proj, lru_conv_w, lru_conv_b, lru_wr, lru_br, lru_wi, lru_bi, lru_lambda, lru_proj, ret_gn_g, ret_gn_b, ret_proj, w_out, w_router, b_router, e_w1, e_w3, e_w2, final_g):
    bsz, n_tok, d = x.shape
    tc = ctx.shape[1]
    seq = tc + n_tok
    depth = w_in.shape[0]
    per_layer = {
        'rwkv_w0': rwkv_w0, 'rwkv_w2': rwkv_w2, 'rwkv_a0': rwkv_a0, 'rwkv_a2': rwkv_a2, 'rwkv_g2': rwkv_g2,
        'rwkv_k_k': rwkv_k_k, 'rwkv_k_a': rwkv_k_a, 'rwkv_r_k': rwkv_r_k,
        'rwkv_gn_g': rwkv_gn_g, 'rwkv_gn_b': rwkv_gn_b,
        'conv_w': conv_w, 'conv_b': conv_b, 'conv_ln_g': conv_ln_g, 'conv_ln_b': conv_ln_b,
        'lru_conv_w': lru_conv_w, 'lru_conv_b': lru_conv_b, 'lru_wr': lru_wr, 'lru_br': lru_br,
        'lru_wi': lru_wi, 'lru_bi': lru_bi, 'lru_lambda': lru_lambda,
        'ret_gn_g': ret_gn_g, 'ret_gn_b': ret_gn_b,
    }
    tm = _pick_tile(seq, 576)

    z = jnp.concatenate([ctx, x], axis=1).reshape(bsz * seq, d)
    for layer in range(depth):
        lp = {name: arr[layer] for name, arr in per_layer.items()}
        last = layer == depth - 1
        cc = jnp.concatenate([c, c_ctx[None, :]], axis=0)
        n_mod = -(-cc.shape[0] // 8) * 8
        cc = jnp.pad(cc, ((0, n_mod - cc.shape[0]), (0, 0)))
        mod = _fused_mm([cc], [], w_ada[layer].astype(BF16), lambda i, cb: cb * jax.nn.sigmoid(cb),
                        tm=n_mod, tn=6 * d // 4,
                        epilogue=lambda i, acc, bb: acc + bb,
                        epi_aux=[(b_ada[layer][None, :], (1, 6 * d // 4), lambda i, j: (0, j))])
        mod_l = mod[:bsz].reshape(bsz, 1, 6 * d)
        mod_c = mod[bsz:bsz + 1]
        sh1, sc1, g1, sh2, sc2, g2 = [mod_l[:, :, n * d:(n + 1) * d] for n in range(6)]
        csh1, csc1, cg1, csh2, csc2, cg2 = [mod_c[:, n * d:(n + 1) * d] for n in range(6)]

        w_l = w_in[layer].astype(BF16)
        norm_args = (norm1_g[layer][None, :], sh1, sc1, csh1, csc1)
        p_rwkv = _in_proj(z, seq, tc, *norm_args, w_l[:, :RWKV_WIDTH], 640,
                          rwkv_mu_prev[layer][None, :], rwkv_mu_next[layer][None, :], out_dtype=BF16)
        p_mid = _in_proj(z, seq, tc, *norm_args, w_l[:, RWKV_WIDTH:RWKV_WIDTH + MID_WIDTH], 512, out_dtype=BF16)
        p_merge = _in_proj(z, seq, tc, *norm_args, w_l[:, RWKV_WIDTH + MID_WIDTH:], 512, out_dtype=BF16)

        y_f, y_b = _rwkv_mix(p_rwkv, seq, tc, lp)
        o_f, o_b = _ret_mix(p_mid, seq, tc, n_tok)
        branches = [
            _rwkv_out_proj(y_f, y_b, p_rwkv, lp, rwkv_proj[layer], tm),
            _plain_mm(_conv_mix(p_mid, seq, tc, lp), conv_proj[layer].astype(BF16), tm=tm, tn=d, out_dtype=BF16),
            _plain_mm(_lru_mix(p_mid, seq, tc, lp), lru_proj[layer].astype(BF16), tm=tm, tn=d, out_dtype=BF16),
            _ret_out_proj(o_f, o_b, p_mid, lp, ret_proj[layer], tm),
        ]
        z = _merge_out(z, p_merge, branches, b_merge[layer], w_out[layer], g1, cg1, seq, tc, d)

        z3 = z.reshape(bsz, seq, d)
        zn = z3 * lax.rsqrt(jnp.mean(z3 * z3, axis=-1, keepdims=True) + RMS_EPS) * norm2_g[layer]
        is_ctx = (jnp.arange(seq) < tc)[None, :, None]
        h2 = zn * (1.0 + jnp.where(is_ctx, csc2[None], sc2)) + jnp.where(is_ctx, csh2[None], sh2)
        gate2 = jnp.where(is_ctx, cg2[None], g2)
        w1b, w3b, w2b = e_w1[layer], e_w3[layer], e_w2[layer]
        if not last:
            y = _moe(h2.reshape(-1, d), w_router, b_router, w1b, w3b, w2b).reshape(bsz, seq, d)
            z = (z3 + gate2 * y).reshape(bsz * seq, d)
        else:
            y = _moe(h2[:, tc:].reshape(-1, d), w_router, b_router, w1b, w3b, w2b).reshape(bsz, n_tok, d)
            xl = z3[:, tc:] + g2 * y
            return xl * lax.rsqrt(jnp.mean(xl * xl, axis=-1, keepdims=True) + RMS_EPS) * final_g
    return None
```

```python
import functools
import math

import jax
import jax.numpy as jnp
from jax import lax
from jax.experimental import pallas as pl
from jax.experimental.pallas import tpu as pltpu

F32 = jnp.float32
BF16 = jnp.bfloat16

D_MODEL = 1024
GRID_W = 64
N_BRANCH = 4
BR_W = D_MODEL // 2
RWKV_HS = 64
RWKV_H = BR_W // RWKV_HS
RWKV_W_LORA = 64
RWKV_A_LORA = 64
RWKV_G_LORA = 128
RWKV_GN_EPS = 64e-5
L2_EPS = 1e-12
CONV_K = 31
LRU_BLOCKS = 8
LRU_BS = BR_W // LRU_BLOCKS
LRU_CONV = 4
LRU_C = 8.0
RET_H = 4
RET_DK = 64
RET_DV = BR_W // RET_H
RET_CHUNK = 128
ROPE_BASE = 10000.0
N_EXPERTS = 32
N_GROUPS = 4
EXPERTS_PER_GROUP = N_EXPERTS // N_GROUPS
TOP_K = 2
D_EXPERT = D_MODEL // 2
RMS_EPS = 1e-6
LN_EPS = 1e-5

REST_COLS = (('conv_val', BR_W), ('conv_gate', BR_W), ('lru_x', BR_W), ('lru_gate', BR_W),
             ('ret_q', RET_H * RET_DK), ('ret_k', RET_H * RET_DK), ('ret_v', BR_W), ('ret_g', BR_W))
RWKV_WIDTH = 3 * BR_W + 2 * RWKV_W_LORA + 2 * RWKV_A_LORA + RWKV_G_LORA
LORA_OFF = 3 * BR_W
LORA_W = 2 * RWKV_W_LORA + 2 * RWKV_A_LORA
MID_WIDTH = sum(w for _, w in REST_COLS)
MERGE_WIDTH = N_BRANCH * D_MODEL

VMEM_LIMIT = 48 * 1024 * 1024
RWKV_CHUNK = 64
RWKV_TB = 256
MOE_BM = 256


def _bdot(a, b):
    return jnp.dot(a.astype(BF16), b.astype(BF16), preferred_element_type=F32)


_NT = (((1,), (1,)), ((), ()))
_TN = (((0,), (0,)), ((), ()))


def _pick_tile(n, cap, mult=8):
    best = None
    for t in range(mult, cap + 1, mult):
        if n % t == 0:
            best = t
    assert best is not None, (n, cap, mult)
    return best


def _fused_mm_body(*refs, nr, na, ner, nea, prologue, epilogue, out_dtype):
    rows = refs[:nr]
    auxs = refs[nr:nr + na]
    w_ref = refs[nr + na]
    er = refs[nr + na + 1:nr + na + 1 + ner]
    ea = refs[nr + na + 1 + ner:nr + na + 1 + ner + nea]
    o_ref, act = refs[-2], refs[-1]
    i = pl.program_id(0)

    @pl.when(pl.program_id(1) == 0)
    def _():
        act[...] = prologue(i, *[r[...] for r in rows], *[a[...] for a in auxs]).astype(BF16)

    acc = jnp.dot(act[...], w_ref[...], preferred_element_type=F32)
    if epilogue is not None:
        acc = epilogue(i, acc, *[r[...] for r in er], *[a[...] for a in ea])
    o_ref[...] = acc.astype(out_dtype)


def _fused_mm(row_ins, aux_ins, w, prologue, *, tm, tn, out_dtype=F32, epilogue=None,
              epi_rows=(), epi_aux=()):
    row_ins = [r if isinstance(r, tuple) else (r, r.shape[1], 0) for r in row_ins]
    n = row_ins[0][0].shape[0]
    k, m = w.shape
    assert n % tm == 0 and m % tn == 0, (n, tm, m, tn)
    in_specs = [pl.BlockSpec((tm, bw), functools.partial(lambda i, j, cb: (i, cb), cb=cb))
                for (_, bw, cb) in row_ins]
    in_specs += [pl.BlockSpec(bs, im) for (_, bs, im) in aux_ins]
    in_specs += [pl.BlockSpec((k, tn), lambda i, j: (0, j))]
    in_specs += [pl.BlockSpec((tm, tn), lambda i, j: (i, j)) for _ in epi_rows]
    in_specs += [pl.BlockSpec(bs, im) for (_, bs, im) in epi_aux]
    body = functools.partial(_fused_mm_body, nr=len(row_ins), na=len(aux_ins), ner=len(epi_rows),
                             nea=len(epi_aux), prologue=prologue, epilogue=epilogue, out_dtype=out_dtype)
    return pl.pallas_call(
        body,
        grid=(n // tm, m // tn),
        in_specs=in_specs,
        out_specs=pl.BlockSpec((tm, tn), lambda i, j: (i, j)),
        out_shape=jax.ShapeDtypeStruct((n, m), out_dtype),
        scratch_shapes=[pltpu.VMEM((tm, k), BF16)],
        compiler_params=pltpu.CompilerParams(dimension_semantics=("parallel", "arbitrary"),
                                             vmem_limit_bytes=VMEM_LIMIT),
    )(*[a for (a, _, _) in row_ins], *[a for (a, _, _) in aux_ins], w, *epi_rows,
      *[a for (a, _, _) in epi_aux])


def _whole(a):
    nd = a.ndim
    return (a, a.shape, lambda i, j: (0,) * nd)


def _plain_mm(x, w, *, tm, tn, out_dtype=F32):
    return _fused_mm([x], [], w, lambda i, xb: xb, tm=tm, tn=tn, out_dtype=out_dtype)


def _in_proj_body(*refs, tc, shift):
    if shift:
        z_ref, g_ref, shl_ref, scl_ref, shc_ref, scc_ref, w_ref, mup_ref, mun_ref, o_ref, act = refs
    else:
        z_ref, g_ref, shl_ref, scl_ref, shc_ref, scc_ref, w_ref, o_ref, act = refs
    seq = z_ref.shape[0]
    row = lax.broadcasted_iota(jnp.int32, (seq, 1), 0)

    @pl.when(pl.program_id(1) == 0)
    def _():
        xb = z_ref[...]
        y = xb * lax.rsqrt(jnp.mean(xb * xb, axis=-1, keepdims=True) + RMS_EPS) * g_ref[...]
        is_ctx = row < tc
        sc = jnp.where(is_ctx, scc_ref[...], scl_ref[0])
        sh = jnp.where(is_ctx, shc_ref[...], shl_ref[0])
        act[...] = (y * (1.0 + sc) + sh).astype(BF16)

    acc = jnp.dot(act[...], w_ref[...], preferred_element_type=F32)
    if shift:
        prev = pltpu.roll(acc, 1, 0)
        nxt = pltpu.roll(acc, seq - 1, 0)
        prev = jnp.where((row == 0) | (row == tc), 0.0, prev)
        nxt = jnp.where((row == tc - 1) | (row == seq - 1), 0.0, nxt)
        acc = acc + mup_ref[...] * (prev - acc) + mun_ref[...] * (nxt - acc)
    o_ref[...] = acc.astype(o_ref.dtype)


def _in_proj(z, seq, tc, g, sh_l, sc_l, sh_c, sc_c, w, tn, mu_prev=None, mu_next=None, out_dtype=F32):
    n, d = z.shape
    m = w.shape[1]
    shift = mu_prev is not None
    assert m % tn == 0
    vec = lambda width: pl.BlockSpec((1, width), lambda b, j: (0, 0))
    per_b = pl.BlockSpec((1, 1, d), lambda b, j: (b, 0, 0))
    in_specs = [pl.BlockSpec((seq, d), lambda b, j: (b, 0)), vec(d), per_b, per_b, vec(d), vec(d),
                pl.BlockSpec((d, tn), lambda b, j: (0, j))]
    args = [z, g, sh_l, sc_l, sh_c, sc_c, w]
    if shift:
        in_specs += [pl.BlockSpec((1, tn), lambda b, j: (0, j))] * 2
        args += [mu_prev, mu_next]
    return pl.pallas_call(
        functools.partial(_in_proj_body, tc=tc, shift=shift),
        grid=(n // seq, m // tn),
        in_specs=in_specs,
        out_specs=pl.BlockSpec((seq, tn), lambda b, j: (b, j)),
        out_shape=jax.ShapeDtypeStruct((n, m), out_dtype),
        scratch_shapes=[pltpu.VMEM((seq, d), BF16)],
        compiler_params=pltpu.CompilerParams(dimension_semantics=("parallel", "arbitrary"),
                                             vmem_limit_bytes=VMEM_LIMIT),
    )(*args)


def _rwkv_body(rf, kf, vf, lf, rb_, kb, vb_, lb, w0_ref, w2_ref, a0_ref, a2_ref, kkw_ref, ka_ref, bd_ref,
               of_ref, ob_ref, s_ref, kk_ref, *, n_chunks):
    c_len = RWKV_CHUNK

    @pl.when(pl.program_id(1) == 0)
    def _():
        s_ref[...] = jnp.zeros_like(s_ref)

    for d, k_blk in enumerate((kf, kb)):
        kx = k_blk[...].astype(F32) * kkw_ref[...]
        kk_ref[d] = kx * lax.rsqrt(_bdot(kx * kx, bd_ref[...]) + L2_EPS)

    ri = lax.broadcasted_iota(jnp.int32, (c_len, c_len), 0)
    ci = lax.broadcasted_iota(jnp.int32, (c_len, c_len), 1)
    eye = (ri == ci).astype(F32)
    in_refs = ((rf, kf, vf, lf), (rb_, kb, vb_, lb))
    masks = []
    for d in range(2):
        before = (ri > ci) if d == 0 else (ri < ci)
        strict = before.astype(F32)
        incl = (before | (ri == ci)).astype(F32)
        m8 = jnp.where(((ri // 8) == (ci // 8)) & before, 1.0, 0.0).astype(F32)
        merges = []
        for size in (16, 32, 64):
            same = (ri // size) == (ci // size)
            inner = (ri // (size // 2)) == (ci // (size // 2))
            merges.append(jnp.where(same & jnp.logical_not(inner) & before, 1.0, 0.0).astype(F32))
        masks.append(dict(strict=strict, incl=incl, incl_b=incl.astype(BF16), m8=m8, merges=merges))
    out_refs = (of_ref, ob_ref)
    decay_scale = math.exp(-0.5)

    def chunk(c, carry):
        per_dir = []
        for d in range(2):
            cc = c if d == 0 else n_chunks - 1 - c
            sl = pl.ds(pl.multiple_of(cc * c_len, c_len), c_len)
            r_ref, k_ref, v_ref, lo_ref = in_refs[d]
            r = r_ref[sl, :].astype(F32)
            k = k_ref[sl, :].astype(F32)
            v = v_ref[sl, :].astype(F32)
            lo = lo_ref[sl, :].astype(F32)
            wlo = lo[:, d * RWKV_W_LORA:(d + 1) * RWKV_W_LORA]
            alo = lo[:, 2 * RWKV_W_LORA + d * RWKV_A_LORA:2 * RWKV_W_LORA + (d + 1) * RWKV_A_LORA]
            w_raw = w0_ref[d] + _bdot(jnp.tanh(wlo), w2_ref[d])
            lw = -decay_scale * jax.nn.sigmoid(w_raw)
            a = jax.nn.sigmoid(a0_ref[d] + _bdot(alo, a2_ref[d]))
            k_d = k * (1.0 + (a - 1.0) * ka_ref[...])
            kk = kk_ref[d, sl, :]
            hi = lw.astype(BF16)
            lo2 = (lw - hi.astype(F32)).astype(BF16)
            tri = masks[d]['incl_b']
            cum = (jnp.dot(tri, hi, preferred_element_type=F32)
                   + jnp.dot(tri, lo2, preferred_element_type=F32))
            eg = jnp.exp(cum)
            ieg = jnp.exp(-cum)
            last = c_len - 1 if d == 0 else 0
            per_dir.append(dict(
                rt=(r * eg).astype(BF16), kt=(k_d * ieg).astype(BF16), at=(-(kk * a) * ieg).astype(BF16),
                bt=(kk * jnp.exp(cum - lw)).astype(BF16), vb=v.astype(BF16), g_last=eg[last:last + 1, :],
                rows=sl))
        chains = [(d, h) for d in range(2) for h in range(RWKV_H)]
        hsl = lambda h: slice(h * RWKV_HS, (h + 1) * RWKV_HS)
        rb = [jnp.concatenate([per_dir[d]['rt'][:, hsl(h)], per_dir[d]['bt'][:, hsl(h)]], axis=0)
              for d, h in chains]
        at_h = [per_dir[d]['at'][:, hsl(h)] for d, h in chains]
        kt_h = [per_dir[d]['kt'][:, hsl(h)] for d, h in chains]
        v_h = [per_dir[d]['vb'][:, hsl(h)] for d, h in chains]
        mk = [masks[d] for d, h in chains]
        n = len(chains)
        m_a = [lax.dot_general(rb[i], at_h[i], _NT, preferred_element_type=F32) for i in range(n)]
        m_k = [lax.dot_general(rb[i], kt_h[i], _NT, preferred_element_type=F32) for i in range(n)]
        ra = [m_a[i][:c_len] * mk[i]['incl'] for i in range(n)]
        nmat = [m_a[i][c_len:] * mk[i]['strict'] for i in range(n)]
        rk = [m_k[i][:c_len] * mk[i]['incl'] for i in range(n)]
        bk = [m_k[i][c_len:] * mk[i]['strict'] for i in range(n)]
        d1 = [nmat[i] * mk[i]['m8'] for i in range(n)]
        d2 = [_bdot(d1[i], d1[i]) for i in range(n)]
        d4 = [_bdot(d2[i], d2[i]) for i in range(n)]
        tinv = [eye + d1[i] for i in range(n)]
        tinv = [tinv[i] + _bdot(tinv[i], d2[i]) for i in range(n)]
        tinv = [tinv[i] + _bdot(tinv[i], d4[i]) for i in range(n)]
        for lvl in range(3):
            e = [_bdot(tinv[i], nmat[i] * mk[i]['merges'][lvl]) for i in range(n)]
            tinv = [tinv[i] + _bdot(e[i], tinv[i]) for i in range(n)]
        s0 = [s_ref[d, h] for d, h in chains]
        rbs = [lax.dot_general(rb[i], s0[i].astype(BF16), _NT, preferred_element_type=F32) for i in range(n)]
        x = [rbs[i][c_len:] + _bdot(bk[i], v_h[i]) for i in range(n)]
        u = [_bdot(tinv[i], x[i]) for i in range(n)]
        y = [rbs[i][:c_len] + _bdot(ra[i], u[i]) + _bdot(rk[i], v_h[i]) for i in range(n)]
        for i, (d, h) in enumerate(chains):
            uv = jnp.concatenate([u[i].astype(BF16), v_h[i]], axis=0)
            ak = jnp.concatenate([at_h[i], kt_h[i]], axis=0)
            ds = lax.dot_general(uv, ak, _TN, preferred_element_type=F32)
            s_ref[d, h] = (s0[i] + ds) * per_dir[d]['g_last'][:, hsl(h)]
        for d in range(2):
            out_refs[d][per_dir[d]['rows'], :] = jnp.concatenate(
                [y[d * RWKV_H + h] for h in range(RWKV_H)], axis=1)
        return carry

    lax.fori_loop(0, n_chunks, chunk, 0)


def _rwkv_mix(p_rwkv, seq, tc, lp):
    n = p_rwkv.shape[0]
    bsz = n // seq
    tb = _pick_tile(math.gcd(seq, tc), RWKV_TB, RWKV_CHUNK)
    nblk, cblk = seq // tb, tc // tb
    assert LORA_OFF % LORA_W == 0

    def fwd_blk(t):
        return t

    def bwd_blk(t):
        return jnp.where(t < cblk, cblk - 1 - t, nblk - 1 - (t - cblk))

    def col_spec(width, cb, blk_fn):
        return pl.BlockSpec((tb, width), lambda b, t: (b * nblk + blk_fn(t), cb))

    in_specs = []
    for blk_fn in (fwd_blk, bwd_blk):
        in_specs += [col_spec(BR_W, 0, blk_fn), col_spec(BR_W, 1, blk_fn), col_spec(BR_W, 2, blk_fn),
                     col_spec(LORA_W, LORA_OFF // LORA_W, blk_fn)]
    whole = lambda a: pl.BlockSpec(a.shape, lambda b, t: (0,) * a.ndim)
    head_id = jnp.arange(BR_W) // RWKV_HS
    bd = (head_id[:, None] == head_id[None, :]).astype(BF16)
    consts = [lp['rwkv_w0'][:, None, :], lp['rwkv_w2'].astype(BF16), lp['rwkv_a0'][:, None, :],
              lp['rwkv_a2'].astype(BF16), lp['rwkv_k_k'][None, :], lp['rwkv_k_a'][None, :], bd]
    in_specs += [whole(a) for a in consts]
    return pl.pallas_call(
        functools.partial(_rwkv_body, n_chunks=tb // RWKV_CHUNK),
        grid=(bsz, nblk),
        in_specs=in_specs,
        out_specs=[col_spec(BR_W, 0, fwd_blk), col_spec(BR_W, 0, bwd_blk)],
        out_shape=[jax.ShapeDtypeStruct((n, BR_W), F32)] * 2,
        scratch_shapes=[pltpu.VMEM((2, RWKV_H, RWKV_HS, RWKV_HS), F32), pltpu.VMEM((2, tb, BR_W), F32)],
        compiler_params=pltpu.CompilerParams(
            dimension_semantics=("parallel", "arbitrary"), vmem_limit_bytes=VMEM_LIMIT),
    )(*([p_rwkv] * 8), *consts)


def _rwkv_out_proj(y_f, y_b, p_rwkv, lp, w_proj, tm):
    head_id = jnp.arange(BR_W) // RWKV_HS
    bd = (head_id[:, None] == head_id[None, :]).astype(BF16)

    def hsum(xv, bdm):
        hi = xv.astype(BF16)
        lo = (xv - hi.astype(F32)).astype(BF16)
        return (jnp.dot(hi, bdm, preferred_element_type=F32) + jnp.dot(lo, bdm, preferred_element_type=F32))

    def act(i, yf, yb, r, k, v, glo, gn_g, gn_b, r_k, g2, bdm):
        inv = 1.0 / RWKV_HS
        r, k, v, glo = [t.astype(F32) for t in (r, k, v, glo)]
        yb = yf + yb
        yc = yb - hsum(yb, bdm) * inv
        var = hsum(yc * yc, bdm) * inv
        yn = yc * lax.rsqrt(var + RWKV_GN_EPS) * gn_g + gn_b
        bonus = hsum(r * k * r_k, bdm) * v
        g = _bdot(jax.nn.sigmoid(glo), g2)
        return (yn + bonus) * g

    glo_cb = (LORA_OFF + LORA_W) // RWKV_G_LORA
    assert (LORA_OFF + LORA_W) % RWKV_G_LORA == 0
    return _fused_mm(
        [y_f, y_b, (p_rwkv, BR_W, 0), (p_rwkv, BR_W, 1), (p_rwkv, BR_W, 2), (p_rwkv, RWKV_G_LORA, glo_cb)],
        [_whole(lp['rwkv_gn_g'][None, :]), _whole(lp['rwkv_gn_b'][None, :]), _whole(lp['rwkv_r_k'][None, :]),
         _whole(lp['rwkv_g2'].astype(BF16)), _whole(bd)],
        w_proj.astype(BF16), act, tm=tm, tn=w_proj.shape[1], out_dtype=BF16)


LRU_TB = 256
LRU_HALO = 16


def _lru_body(x_ref, g_ref, cw_ref, cb_ref, wr_ref, br_ref, wi_ref, bi_ref, sp_ref, o_ref, h_ref, *,
              tb, nblk, cblk):
    seq = x_ref.shape[0]
    row = lax.broadcasted_iota(jnp.int32, (tb, 1), 0)
    n_ext = tb + LRU_HALO
    for d in range(2):
        h_ref[...] = jnp.zeros_like(h_ref)

        def blk_body(i, carry, d=d):
            if d == 0:
                blk = i
                halo_ok = (blk != 0) & (blk != cblk)
                halo0 = jnp.maximum(blk * tb - LRU_HALO, 0)
            else:
                blk = jnp.where(i < cblk, cblk - 1 - i, nblk - 1 - (i - cblk))
                halo_ok = (blk != cblk - 1) & (blk != nblk - 1)
                halo0 = jnp.minimum(blk * tb + tb, seq - LRU_HALO)
            rows = pl.ds(pl.multiple_of(blk * tb, tb), tb)
            cur = x_ref[rows, :].astype(F32)
            halo = (x_ref[pl.ds(pl.multiple_of(halo0, LRU_HALO), LRU_HALO), :].astype(F32)
                    * jnp.where(halo_ok, 1.0, 0.0))
            xc = cb_ref[d]
            if d == 0:
                ext = jnp.concatenate([halo, cur], axis=0)
                for j in range(LRU_CONV):
                    sh = LRU_CONV - 1 - j
                    tap = ext if sh == 0 else pltpu.roll(ext, sh, 0)
                    xc = xc + cw_ref[d, j:j + 1, :] * tap[LRU_HALO:, :]
            else:
                ext = jnp.concatenate([cur, halo], axis=0)
                for j in range(LRU_CONV):
                    tap = ext if j == 0 else pltpu.roll(ext, n_ext - j, 0)
                    xc = xc + cw_ref[d, j:j + 1, :] * tap[:tb, :]
            gate_r = jax.nn.sigmoid(_bdot(xc, wr_ref[d]) + br_ref[d])
            gate_i = jax.nn.sigmoid(_bdot(xc, wi_ref[d]) + bi_ref[d])
            log_a = -LRU_C * gate_r * sp_ref[d]
            a_cum = jnp.exp(log_a)
            b_cum = jnp.sqrt(1.0 - jnp.exp(2.0 * log_a)) * (gate_i * xc)
            s = 1
            while s < tb:
                if d == 0:
                    ok = row >= s
                    a_sh = jnp.where(ok, pltpu.roll(a_cum, s, 0), 1.0)
                    b_sh = jnp.where(ok, pltpu.roll(b_cum, s, 0), 0.0)
                else:
                    ok = row < tb - s
                    a_sh = jnp.where(ok, pltpu.roll(a_cum, tb - s, 0), 1.0)
                    b_sh = jnp.where(ok, pltpu.roll(b_cum, tb - s, 0), 0.0)
                b_cum = a_cum * b_sh + b_cum
                a_cum = a_cum * a_sh
                s *= 2
            h = b_cum + a_cum * h_ref[...]
            if d == 0:
                h_ref[...] = h[tb - 1:tb, :]
                o_ref[rows, :] = h
            else:
                h_ref[...] = h[0:1, :]
                o_ref[rows, :] = (o_ref[rows, :] + h) * jax.nn.gelu(g_ref[rows, :].astype(F32))
            return carry

        lax.fori_loop(0, nblk, blk_body, 0)


def _lru_mix(p_mid, seq, tc, lp):
    n = p_mid.shape[0]
    tb = _pick_tile(math.gcd(seq, tc), LRU_TB)
    nblk, cblk = seq // tb, tc // tb
    eye = jnp.eye(LRU_BLOCKS, dtype=F32)
    blockdiag = lambda w: jnp.einsum('dgij,gh->dgihj', w, eye).reshape(2, BR_W, BR_W).astype(BF16)
    consts = [lp['lru_conv_w'], lp['lru_conv_b'][:, None, :], blockdiag(lp['lru_wr']), lp['lru_br'][:, None, :],
              blockdiag(lp['lru_wi']), lp['lru_bi'][:, None, :], jax.nn.softplus(-lp['lru_lambda'])[:, None, :]]
    whole = lambda a: pl.BlockSpec(a.shape, lambda b: (0,) * a.ndim)
    x_cb = [name for name, _ in REST_COLS].index('lru_x')
    g_cb = [name for name, _ in REST_COLS].index('lru_gate')
    assert all(w == BR_W for _, w in REST_COLS[:max(x_cb, g_cb) + 1])
    return pl.pallas_call(
        functools.partial(_lru_body, tb=tb, nblk=nblk, cblk=cblk),
        grid=(n // seq,),
        in_specs=[pl.BlockSpec((seq, BR_W), lambda b: (b, x_cb)), pl.BlockSpec((seq, BR_W), lambda b: (b, g_cb))]
        + [whole(a) for a in consts],
        out_specs=pl.BlockSpec((seq, BR_W), lambda b: (b, 0)),
        out_shape=jax.ShapeDtypeStruct((n, BR_W), F32),
        scratch_shapes=[pltpu.VMEM((1, BR_W), F32)],
        compiler_params=pltpu.CompilerParams(dimension_semantics=("parallel",), vmem_limit_bytes=VMEM_LIMIT),
    )(p_mid, p_mid, *consts)


CONV_TB = 256
CONV_HALO = 16


def _conv_body(val_ref, gate_ref, w_ref, b_ref, lng_ref, lnb_ref, o_ref, *, tb, nblk, cblk):
    seq = val_ref.shape[0]
    n_ext = tb + 2 * CONV_HALO
    pad = CONV_K // 2

    def glu(rows):
        return val_ref[rows, :].astype(F32) * jax.nn.sigmoid(gate_ref[rows, :].astype(F32))

    def blk_body(blk, carry):
        t0 = blk * tb
        rows = pl.ds(pl.multiple_of(t0, tb), tb)
        lo_ok = (blk != 0) & (blk != cblk)
        hi_ok = (blk != cblk - 1) & (blk != nblk - 1)
        lo0 = jnp.maximum(t0 - CONV_HALO, 0)
        hi0 = jnp.minimum(t0 + tb, seq - CONV_HALO)
        lo = glu(pl.ds(pl.multiple_of(lo0, CONV_HALO), CONV_HALO)) * jnp.where(lo_ok, 1.0, 0.0)
        hi = glu(pl.ds(pl.multiple_of(hi0, CONV_HALO), CONV_HALO)) * jnp.where(hi_ok, 1.0, 0.0)
        ext = jnp.concatenate([lo, glu(rows), hi], axis=0)
        acc = jnp.zeros((tb, ext.shape[1]), F32) + b_ref[...]
        for r in range(8):
            rolled = ext if r == 0 else pltpu.roll(ext, n_ext - r, 0)
            for j in range(CONV_K):
                off = CONV_HALO - pad + j
                if off % 8 == r:
                    acc = acc + w_ref[j:j + 1, :] * rolled[off - r:off - r + tb, :]
        mu = jnp.mean(acc, axis=-1, keepdims=True)
        xc = acc - mu
        yn = xc * lax.rsqrt(jnp.mean(xc * xc, axis=-1, keepdims=True) + LN_EPS) * lng_ref[...] + lnb_ref[...]
        o_ref[rows, :] = yn * jax.nn.sigmoid(yn)
        return carry

    lax.fori_loop(0, nblk, blk_body, 0)


def _conv_mix(p_mid, seq, tc, lp):
    n = p_mid.shape[0]
    tb = _pick_tile(math.gcd(seq, tc), CONV_TB, CONV_HALO)
    names = [name for name, _ in REST_COLS]
    v_cb, g_cb = names.index('conv_val'), names.index('conv_gate')
    assert all(w == BR_W for _, w in REST_COLS[:max(v_cb, g_cb) + 1]) and CONV_HALO >= CONV_K // 2
    consts = [lp['conv_w'], lp['conv_b'][None, :], lp['conv_ln_g'][None, :], lp['conv_ln_b'][None, :]]
    whole = lambda a: pl.BlockSpec(a.shape, lambda b: (0,) * a.ndim)
    return pl.pallas_call(
        functools.partial(_conv_body, tb=tb, nblk=seq // tb, cblk=tc // tb),
        grid=(n // seq,),
        in_specs=[pl.BlockSpec((seq, BR_W), lambda b: (b, v_cb)), pl.BlockSpec((seq, BR_W), lambda b: (b, g_cb))]
        + [whole(a) for a in consts],
        out_specs=pl.BlockSpec((seq, BR_W), lambda b: (b, 0)),
        out_shape=jax.ShapeDtypeStruct((n, BR_W), F32),
        compiler_params=pltpu.CompilerParams(dimension_semantics=("parallel",), vmem_limit_bytes=VMEM_LIMIT),
    )(p_mid, p_mid, *consts)


RET_TB = 256
_RET_LOG_G = [math.log1p(-2.0 ** (-5.0 - h)) for h in range(RET_H)]


def _ret_body(qf, kf, vf, cf, sf, qb, kb, vb, cb, sb, dm_ref, xi_ref, zt_ref, of_ref, ob_ref, s_ref, *, n_chunks):
    c_len = RET_CHUNK
    qk_w = RET_H * RET_DK

    @pl.when(pl.program_id(1) == 0)
    def _():
        s_ref[...] = jnp.zeros_like(s_ref)

    lane = lax.broadcasted_iota(jnp.int32, (c_len, qk_w), 1)
    first_half = (lane % RET_DK) < (RET_DK // 2)

    def rotary(z, cos, sin):
        swapped = jnp.where(first_half, pltpu.roll(z, qk_w - RET_DK // 2, 1), pltpu.roll(z, RET_DK // 2, 1))
        return z * cos + swapped * sin

    in_refs = ((qf, kf, vf, cf, sf), (qb, kb, vb, cb, sb))
    out_refs = (of_ref, ob_ref)

    def chunk(c, carry):
        per_dir = []
        for d in range(2):
            cc = c if d == 0 else n_chunks - 1 - c
            sl = pl.ds(pl.multiple_of(cc * c_len, c_len), c_len)
            q_ref, k_ref, v_ref, c_ref, sn_ref = in_refs[d]
            cos, sin = c_ref[sl, :], sn_ref[sl, :]
            q = rotary(q_ref[sl, :].astype(F32), cos, sin)
            k = rotary(k_ref[sl, :].astype(F32), cos, sin) * (RET_DK ** -0.5)
            per_dir.append(dict(q=q.astype(BF16), k=k.astype(BF16), kz=(k * zt_ref[d]).astype(BF16),
                                v=v_ref[sl, :].astype(BF16), sl=sl))
        groups = [(d, h) for d in range(2) for h in range(RET_H)]
        ksl = lambda h: slice(h * RET_DK, (h + 1) * RET_DK)
        vsl = lambda h: slice(h * RET_DV, (h + 1) * RET_DV)
        q_h = [per_dir[d]['q'][:, ksl(h)] for d, h in groups]
        v_h = [per_dir[d]['v'][:, vsl(h)] for d, h in groups]
        att = [lax.dot_general(q_h[i], per_dir[d]['k'][:, ksl(h)], _NT, preferred_element_type=F32) * dm_ref[d, h]
               for i, (d, h) in enumerate(groups)]
        s0 = [s_ref[d, h] for d, h in groups]
        o = [_bdot(att[i], v_h[i]) + _bdot(q_h[i], s0[i]) * xi_ref[d][:, vsl(h)]
             for i, (d, h) in enumerate(groups)]
        for i, (d, h) in enumerate(groups):
            kv = lax.dot_general(per_dir[d]['kz'][:, ksl(h)], v_h[i], _TN, preferred_element_type=F32)
            s_ref[d, h] = s0[i] * math.exp(_RET_LOG_G[h] * c_len) + kv
        for d in range(2):
            out_refs[d][per_dir[d]['sl'], :] = jnp.concatenate([o[d * RET_H + h] for h in range(RET_H)], axis=1)
        return carry

    lax.fori_loop(0, n_chunks, chunk, 0)


def _ret_mix(p_mid, seq, tc, n_tok):
    n = p_mid.shape[0]
    tb = _pick_tile(math.gcd(seq, tc), RET_TB, RET_CHUNK)
    nblk, cblk = seq // tb, tc // tb
    c_len = RET_CHUNK
    n_rows = n_tok // GRID_W
    rows = jnp.repeat(jnp.arange(n_rows, dtype=F32), GRID_W)
    cols = jnp.tile(jnp.arange(GRID_W, dtype=F32), n_rows)
    n_freq = RET_DK // 4
    inv = ROPE_BASE ** (-jnp.arange(n_freq, dtype=F32) / n_freq)
    ang = jnp.concatenate([rows[:, None] * inv, cols[:, None] * inv], axis=-1)
    cos_h = jnp.concatenate([jnp.cos(ang), jnp.cos(ang)], axis=-1)
    sin_h = jnp.concatenate([-jnp.sin(ang), jnp.sin(ang)], axis=-1)
    pad_ctx = lambda tbl, fill: jnp.concatenate(
        [jnp.full((tc, RET_H * RET_DK), fill, F32), jnp.tile(tbl, (1, RET_H))], axis=0)
    cos_t, sin_t = pad_ctx(cos_h, 1.0), pad_ctx(sin_h, 0.0)
    log_g = jnp.asarray(_RET_LOG_G, F32)
    idx = jnp.arange(c_len, dtype=F32)
    dm, xi, zt = [], [], []
    for d in range(2):
        pos = idx if d == 0 else c_len - 1.0 - idx
        diff = pos[:, None] - pos[None, :]
        keep = diff >= 0 if d == 0 else diff > 0
        dm.append(jnp.where(keep[None], jnp.exp(log_g[:, None, None] * jnp.maximum(diff, 0.0)[None]), 0.0))
        xi.append(jnp.repeat(jnp.exp(log_g[None, :] * (pos[:, None] + 1.0)), RET_DV, axis=1))
        zt.append(jnp.repeat(jnp.exp(log_g[None, :] * (c_len - 1.0 - pos)[:, None]), RET_DK, axis=1))
    consts = [jnp.stack(dm), jnp.stack(xi), jnp.stack(zt)]

    def fwd_blk(t):
        return t

    def bwd_blk(t):
        return jnp.where(t < cblk, cblk - 1 - t, nblk - 1 - (t - cblk))

    names = [name for name, _ in REST_COLS]
    offs = {name: sum(w for _, w in REST_COLS[:i]) for i, (name, _) in enumerate(REST_COLS)}
    qk_w = RET_H * RET_DK
    in_specs = []
    for blk_fn in (fwd_blk, bwd_blk):
        spec = lambda width, cb, blk_fn=blk_fn: pl.BlockSpec((tb, width), lambda b, t: (b * nblk + blk_fn(t), cb))
        tbl = lambda blk_fn=blk_fn: pl.BlockSpec((tb, qk_w), lambda b, t: (blk_fn(t), 0))
        assert offs['ret_q'] % qk_w == 0 and offs['ret_k'] % qk_w == 0 and offs['ret_v'] % BR_W == 0
        in_specs += [spec(qk_w, offs['ret_q'] // qk_w), spec(qk_w, offs['ret_k'] // qk_w),
                     spec(BR_W, offs['ret_v'] // BR_W), tbl(), tbl()]
    whole = lambda a: pl.BlockSpec(a.shape, lambda b, t: (0,) * a.ndim)
    in_specs += [whole(a) for a in consts]
    out_spec = lambda blk_fn: pl.BlockSpec((tb, BR_W), lambda b, t: (b * nblk + blk_fn(t), 0))
    return pl.pallas_call(
        functools.partial(_ret_body, n_chunks=tb // c_len),
        grid=(n // seq, nblk),
        in_specs=in_specs,
        out_specs=[out_spec(fwd_blk), out_spec(bwd_blk)],
        out_shape=[jax.ShapeDtypeStruct((n, BR_W), F32)] * 2,
        scratch_shapes=[pltpu.VMEM((2, RET_H, RET_DK, RET_DV), F32)],
        compiler_params=pltpu.CompilerParams(
            dimension_semantics=("parallel", "arbitrary"), vmem_limit_bytes=VMEM_LIMIT),
    )(*([p_mid, p_mid, p_mid, cos_t, sin_t] * 2), *consts)


def _ret_out_proj(o_f, o_b, p_mid, lp, w_proj, tm):
    def act(i, of, ob, g, gn_g, gn_b):
        o = of + ob
        g = g.astype(F32)
        normed = []
        for h in range(RET_H):
            oh = o[:, h * RET_DV:(h + 1) * RET_DV]
            oc = oh - jnp.mean(oh, axis=-1, keepdims=True)
            normed.append(oc * lax.rsqrt(jnp.mean(oc * oc, axis=-1, keepdims=True) + LN_EPS))
        yn = jnp.concatenate(normed, axis=1) * gn_g + gn_b
        return g * jax.nn.sigmoid(g) * yn

    off_g = sum(w for _, w in REST_COLS[:[name for name, _ in REST_COLS].index('ret_g')])
    assert off_g % BR_W == 0
    return _fused_mm([o_f, o_b, (p_mid, BR_W, off_g // BR_W)],
                     [_whole(lp['ret_gn_g'][None, :]), _whole(lp['ret_gn_b'][None, :])],
                     w_proj.astype(BF16), act, tm=tm, tn=w_proj.shape[1], out_dtype=BF16)


ROUTER_PAD = 128


def _norm2_router_body(z_ref, g_ref, shl_ref, scl_ref, shc_ref, scc_ref, whi_ref, wlo_ref, tok_ref, log_ref, *,
                       skip, cblk):
    xb = z_ref[...]
    y = xb * lax.rsqrt(jnp.mean(xb * xb, axis=-1, keepdims=True) + RMS_EPS) * g_ref[...]
    is_ctx = (pl.program_id(1) + skip) < cblk
    sc = jnp.where(is_ctx, scc_ref[...], scl_ref[0])
    sh = jnp.where(is_ctx, shc_ref[...], shl_ref[0])
    h = y * (1.0 + sc) + sh
    hi = h.astype(BF16)
    lo = (h - hi.astype(F32)).astype(BF16)
    tok_ref[...] = hi
    log_ref[...] = (jnp.dot(hi, whi_ref[...], preferred_element_type=F32)
                    + jnp.dot(lo, whi_ref[...], preferred_element_type=F32)
                    + jnp.dot(hi, wlo_ref[...], preferred_element_type=F32))


def _norm2_router(z, seq, tc, g, sh_l, sc_l, sh_c, sc_c, w_router, latent_only):
    n, d = z.shape
    bsz = n // seq
    tb = _pick_tile(math.gcd(seq, tc), 256, 16)
    nblk, cblk = seq // tb, tc // tb
    skip = cblk if latent_only else 0
    nb = nblk - skip
    w_pad = jnp.pad(w_router, ((0, 0), (0, ROUTER_PAD - w_router.shape[1])))
    w_hi = w_pad.astype(BF16)
    w_lo = (w_pad - w_hi.astype(F32)).astype(BF16)
    vec = pl.BlockSpec((1, d), lambda b, i: (0, 0))
    per_b = pl.BlockSpec((1, 1, d), lambda b, i: (b, 0, 0))
    wspec = pl.BlockSpec((d, ROUTER_PAD), lambda b, i: (0, 0))
    return pl.pallas_call(
        functools.partial(_norm2_router_body, skip=skip, cblk=cblk),
        grid=(bsz, nb),
        in_specs=[pl.BlockSpec((tb, d), lambda b, i: (b * nblk + skip + i, 0)), vec, per_b, per_b, vec, vec,
                  wspec, wspec],
        out_specs=[pl.BlockSpec((tb, d), lambda b, i: (b * nb + i, 0)),
                   pl.BlockSpec((tb, ROUTER_PAD), lambda b, i: (b * nb + i, 0))],
        out_shape=[jax.ShapeDtypeStruct((bsz * nb * tb, d), BF16),
                   jax.ShapeDtypeStruct((bsz * nb * tb, ROUTER_PAD), F32)],
        compiler_params=pltpu.CompilerParams(dimension_semantics=("parallel", "parallel"),
                                             vmem_limit_bytes=VMEM_LIMIT),
    )(z, g, sh_l, sc_l, sh_c, sc_c, w_hi, w_lo)


def _moe_body(be_ref, nu_ref, x_ref, w1_ref, w3_ref, w2_ref, o_ref):
    i = pl.program_id(0)

    @pl.when(i < nu_ref[0])
    def _():
        xb = x_ref[...]
        h1 = jnp.dot(xb, w1_ref[0].astype(BF16), preferred_element_type=F32)
        h3 = jnp.dot(xb, w3_ref[0].astype(BF16), preferred_element_type=F32)
        hid = (h1 * jax.nn.sigmoid(h1) * h3).astype(BF16)
        o_ref[...] = jnp.dot(hid, w2_ref[0].astype(BF16), preferred_element_type=F32)

    @pl.when(i >= nu_ref[0])
    def _():
        o_ref[...] = jnp.zeros_like(o_ref)


def _moe_experts(xb, block_e, n_used, w1, w3, w2, layer):
    n_rows, d = xb.shape
    n_blocks = n_rows // MOE_BM
    de = w1.shape[3]
    grid_spec = pltpu.PrefetchScalarGridSpec(
        num_scalar_prefetch=2,
        grid=(n_blocks,),
        in_specs=[
            pl.BlockSpec((MOE_BM, d), lambda i, be, nu: (i, 0)),
            pl.BlockSpec((None, 1, d, de), lambda i, be, nu: (layer, be[i], 0, 0)),
            pl.BlockSpec((None, 1, d, de), lambda i, be, nu: (layer, be[i], 0, 0)),
            pl.BlockSpec((None, 1, de, d), lambda i, be, nu: (layer, be[i], 0, 0)),
        ],
        out_specs=pl.BlockSpec((MOE_BM, d), lambda i, be, nu: (i, 0)),
    )
    return pl.pallas_call(
        _moe_body,
        grid_spec=grid_spec,
        out_shape=jax.ShapeDtypeStruct((n_rows, d), F32),
        compiler_params=pltpu.CompilerParams(dimension_semantics=("arbitrary",),
                                             vmem_limit_bytes=VMEM_LIMIT),
    )(block_e, n_used, xb, w1, w3, w2)


def _moe(tokens, logits, b_router, w1, w3, w2, layer):
    n_tok = tokens.shape[0]
    aff = jax.nn.sigmoid(logits[:, :N_EXPERTS])
    sel = (aff + b_router).reshape(-1, N_GROUPS, EXPERTS_PER_GROUP)
    grp_score = jnp.sum(lax.top_k(sel, TOP_K)[0], axis=-1)
    grp = jnp.argmax(grp_score, axis=-1)
    grp_mask = jnp.arange(N_GROUPS)[None, :] == grp[:, None]
    masked = jnp.where(grp_mask[:, :, None], sel, -jnp.inf).reshape(-1, N_EXPERTS)
    _, e_idx = lax.top_k(masked, TOP_K)
    wts = jnp.take_along_axis(aff, e_idx, axis=1)
    wts = wts / jnp.sum(wts, axis=-1, keepdims=True)

    n_asg = n_tok * TOP_K
    flat_e = e_idx.reshape(-1).astype(jnp.int32)
    order = jnp.argsort(flat_e).astype(jnp.int32)
    se, st = flat_e[order], order // TOP_K
    counts = jnp.sum(flat_e[:, None] == jnp.arange(N_EXPERTS, dtype=jnp.int32)[None, :], axis=0, dtype=jnp.int32)
    starts = jnp.cumsum(counts) - counts
    padded = (counts + MOE_BM - 1) // MOE_BM * MOE_BM
    pad_end = jnp.cumsum(padded)
    pad_start = pad_end - padded
    dest = pad_start[se] + jnp.arange(n_asg, dtype=jnp.int32) - starts[se]
    dest_asg = dest[jnp.argsort(order)].reshape(n_tok, TOP_K)
    n_blocks = -(-n_asg // MOE_BM) + N_EXPERTS
    block_e = jnp.minimum(jnp.searchsorted(pad_end, jnp.arange(n_blocks) * MOE_BM, side='right'),
                          N_EXPERTS - 1).astype(jnp.int32)
    n_used = (pad_end[-1] // MOE_BM).astype(jnp.int32).reshape(1)
    slot_e = jnp.repeat(block_e, MOE_BM)
    rank = jnp.arange(n_blocks * MOE_BM, dtype=jnp.int32) - pad_start[slot_e]
    slot_tok = st[jnp.clip(starts[slot_e] + rank, 0, n_asg - 1)]
    xb = tokens[slot_tok]
    yb = _moe_experts(xb, block_e, n_used, w1, w3, w2, layer)
    return wts[:, 0:1] * yb[dest_asg[:, 0]] + wts[:, 1:2] * yb[dest_asg[:, 1]]


def _merge_out(z, merge, branches, b_merge, w_out_l, g1, cg1, seq, tc, d):
    tm2 = _pick_tile(seq, 576, 16)
    bpb = seq // tm2

    def seg_select(i, ctx_val, lat_val):
        row = (i % bpb) * tm2 + lax.broadcasted_iota(jnp.int32, (tm2, 1), 0)
        return jnp.where(row < tc, ctx_val, lat_val)

    def merge_pro(i, mg, b0, b1, b2, b3, bm):
        gates = jax.nn.sigmoid(mg.astype(F32) + bm)
        b0, b1, b2, b3 = [b.astype(F32) for b in (b0, b1, b2, b3)]
        return (gates[:, 0 * d:1 * d] * b0 + gates[:, 1 * d:2 * d] * b1
                + gates[:, 2 * d:3 * d] * b2 + gates[:, 3 * d:4 * d] * b3)

    def resid_epi(i, acc, zb, gl, gc):
        return zb + seg_select(i, gc, gl[0]) * acc

    tn = d // 2
    return _fused_mm([merge] + branches, [_whole(b_merge[None, :])],
                     w_out_l.astype(BF16), merge_pro, tm=tm2, tn=tn,
                     epilogue=resid_epi, epi_rows=[z],
                     epi_aux=[(g1, (1, 1, tn), lambda i, j: (i // bpb, 0, j)),
                              (cg1, (1, tn), lambda i, j: (0, j))])


def kernel(x, c, ctx, c_ctx, w_ada, b_ada, norm1_g, norm2_g, w_in, b_merge, rwkv_mu_prev, rwkv_mu_next, rwkv_w0, rwkv_w2, rwkv_a0, rwkv_a2, rwkv_g2, rwkv_k_k, rwkv_k_a, rwkv_r_k, rwkv_gn_g, rwkv_gn_b, rwkv_proj, conv_w, conv_b, conv_ln_g, conv_ln_b, conv_proj, lru_conv_w, lru_conv_b, lru_wr, lru_br, lru_wi, lru_bi, lru_lambda, lru_proj, ret_gn_g, ret_gn_b, ret_proj, w_out, w_router, b_router, e_w1, e_w3, e_w2, final_g):
    bsz, n_tok, d = x.shape
    tc = ctx.shape[1]
    seq = tc + n_tok
    depth = w_in.shape[0]
    per_layer = {
        'rwkv_w0': rwkv_w0, 'rwkv_w2': rwkv_w2, 'rwkv_a0': rwkv_a0, 'rwkv_a2': rwkv_a2, 'rwkv_g2': rwkv_g2,
        'rwkv_k_k': rwkv_k_k, 'rwkv_k_a': rwkv_k_a, 'rwkv_r_k': rwkv_r_k,
        'rwkv_gn_g': rwkv_gn_g, 'rwkv_gn_b': rwkv_gn_b,
        'conv_w': conv_w, 'conv_b': conv_b, 'conv_ln_g': conv_ln_g, 'conv_ln_b': conv_ln_b,
        'lru_conv_w': lru_conv_w, 'lru_conv_b': lru_conv_b, 'lru_wr': lru_wr, 'lru_br': lru_br,
        'lru_wi': lru_wi, 'lru_bi': lru_bi, 'lru_lambda': lru_lambda,
        'ret_gn_g': ret_gn_g, 'ret_gn_b': ret_gn_b,
    }
    tm = _pick_tile(seq, 576)

    z = jnp.concatenate([ctx, x], axis=1).reshape(bsz * seq, d)
    for layer in range(depth):
        lp = {name: arr[layer] for name, arr in per_layer.items()}
        last = layer == depth - 1
        cc = jnp.concatenate([c, c_ctx[None, :]], axis=0)
        n_mod = -(-cc.shape[0] // 8) * 8
        cc = jnp.pad(cc, ((0, n_mod - cc.shape[0]), (0, 0)))
        mod = _fused_mm([cc], [], w_ada[layer].astype(BF16), lambda i, cb: cb * jax.nn.sigmoid(cb),
                        tm=n_mod, tn=6 * d // 4,
                        epilogue=lambda i, acc, bb: acc + bb,
                        epi_aux=[(b_ada[layer][None, :], (1, 6 * d // 4), lambda i, j: (0, j))])
        mod_l = mod[:bsz].reshape(bsz, 1, 6 * d)
        mod_c = mod[bsz:bsz + 1]
        sh1, sc1, g1, sh2, sc2, g2 = [mod_l[:, :, n * d:(n + 1) * d] for n in range(6)]
        csh1, csc1, cg1, csh2, csc2, cg2 = [mod_c[:, n * d:(n + 1) * d] for n in range(6)]

        w_l = w_in[layer].astype(BF16)
        norm_args = (norm1_g[layer][None, :], sh1, sc1, csh1, csc1)
        p_rwkv = _in_proj(z, seq, tc, *norm_args, w_l[:, :RWKV_WIDTH], 640,
                          rwkv_mu_prev[layer][None, :], rwkv_mu_next[layer][None, :], out_dtype=BF16)
        p_mid = _in_proj(z, seq, tc, *norm_args, w_l[:, RWKV_WIDTH:RWKV_WIDTH + MID_WIDTH], 512, out_dtype=BF16)
        p_merge = _in_proj(z, seq, tc, *norm_args, w_l[:, RWKV_WIDTH + MID_WIDTH:], 512, out_dtype=BF16)

        y_f, y_b = _rwkv_mix(p_rwkv, seq, tc, lp)
        o_f, o_b = _ret_mix(p_mid, seq, tc, n_tok)
        branches = [
            _rwkv_out_proj(y_f, y_b, p_rwkv, lp, rwkv_proj[layer], tm),
            _plain_mm(_conv_mix(p_mid, seq, tc, lp), conv_proj[layer].astype(BF16), tm=tm, tn=d, out_dtype=BF16),
            _plain_mm(_lru_mix(p_mid, seq, tc, lp), lru_proj[layer].astype(BF16), tm=tm, tn=d, out_dtype=BF16),
            _ret_out_proj(o_f, o_b, p_mid, lp, ret_proj[layer], tm),
        ]
        z = _merge_out(z, p_merge, branches, b_merge[layer], w_out[layer], g1, cg1, seq, tc, d)

        z3 = z.reshape(bsz, seq, d)
        tok, logits = _norm2_router(z, seq, tc, norm2_g[layer][None, :], sh2, sc2, csh2, csc2, w_router,
                                    latent_only=last)
        y = _moe(tok, logits, b_router, e_w1, e_w3, e_w2, layer)
        if not last:
            is_ctx = (jnp.arange(seq) < tc)[None, :, None]
            gate2 = jnp.where(is_ctx, cg2[None], g2)
            z = (z3 + gate2 * y.reshape(bsz, seq, d)).reshape(bsz * seq, d)
        else:
            xl = z3[:, tc:] + g2 * y.reshape(bsz, n_tok, d)
            return xl * lax.rsqrt(jnp.mean(xl * xl, axis=-1, keepdims=True) + RMS_EPS) * final_g
    return None
```

```python
import functools
import math

import jax
import jax.numpy as jnp
from jax import lax
from jax.experimental import pallas as pl
from jax.experimental.pallas import tpu as pltpu

F32 = jnp.float32
BF16 = jnp.bfloat16

D_MODEL = 1024
GRID_W = 64
N_BRANCH = 4
BR_W = D_MODEL // 2
RWKV_HS = 64
RWKV_H = BR_W // RWKV_HS
RWKV_W_LORA = 64
RWKV_A_LORA = 64
RWKV_G_LORA = 128
RWKV_GN_EPS = 64e-5
L2_EPS = 1e-12
CONV_K = 31
LRU_BLOCKS = 8
LRU_BS = BR_W // LRU_BLOCKS
LRU_CONV = 4
LRU_C = 8.0
RET_H = 4
RET_DK = 64
RET_DV = BR_W // RET_H
RET_CHUNK = 128
ROPE_BASE = 10000.0
N_EXPERTS = 32
N_GROUPS = 4
EXPERTS_PER_GROUP = N_EXPERTS // N_GROUPS
TOP_K = 2
D_EXPERT = D_MODEL // 2
RMS_EPS = 1e-6
LN_EPS = 1e-5

REST_COLS = (('conv_val', BR_W), ('conv_gate', BR_W), ('lru_x', BR_W), ('lru_gate', BR_W),
             ('ret_q', RET_H * RET_DK), ('ret_k', RET_H * RET_DK), ('ret_v', BR_W), ('ret_g', BR_W))
RWKV_WIDTH = 3 * BR_W + 2 * RWKV_W_LORA + 2 * RWKV_A_LORA + RWKV_G_LORA
LORA_OFF = 3 * BR_W
LORA_W = 2 * RWKV_W_LORA + 2 * RWKV_A_LORA
MID_WIDTH = sum(w for _, w in REST_COLS)
MERGE_WIDTH = N_BRANCH * D_MODEL

VMEM_LIMIT = 48 * 1024 * 1024
RWKV_CHUNK = 64
RWKV_TB = 256
MOE_BM = 256


def _bdot(a, b):
    return jnp.dot(a.astype(BF16), b.astype(BF16), preferred_element_type=F32)


_NT = (((1,), (1,)), ((), ()))
_TN = (((0,), (0,)), ((), ()))


def _pick_tile(n, cap, mult=8):
    best = None
    for t in range(mult, cap + 1, mult):
        if n % t == 0:
            best = t
    assert best is not None, (n, cap, mult)
    return best


def _fused_mm_body(*refs, nr, na, ner, nea, prologue, epilogue, out_dtype):
    rows = refs[:nr]
    auxs = refs[nr:nr + na]
    w_ref = refs[nr + na]
    er = refs[nr + na + 1:nr + na + 1 + ner]
    ea = refs[nr + na + 1 + ner:nr + na + 1 + ner + nea]
    o_ref, act = refs[-2], refs[-1]
    i = pl.program_id(0)

    @pl.when(pl.program_id(1) == 0)
    def _():
        act[...] = prologue(i, *[r[...] for r in rows], *[a[...] for a in auxs]).astype(BF16)

    acc = jnp.dot(act[...], w_ref[...], preferred_element_type=F32)
    if epilogue is not None:
        acc = epilogue(i, acc, *[r[...] for r in er], *[a[...] for a in ea])
    o_ref[...] = acc.astype(out_dtype)


def _fused_mm(row_ins, aux_ins, w, prologue, *, tm, tn, out_dtype=F32, epilogue=None,
              epi_rows=(), epi_aux=()):
    row_ins = [r if isinstance(r, tuple) else (r, r.shape[1], 0) for r in row_ins]
    n = row_ins[0][0].shape[0]
    k, m = w.shape
    assert n % tm == 0 and m % tn == 0, (n, tm, m, tn)
    in_specs = [pl.BlockSpec((tm, bw), functools.partial(lambda i, j, cb: (i, cb), cb=cb))
                for (_, bw, cb) in row_ins]
    in_specs += [pl.BlockSpec(bs, im) for (_, bs, im) in aux_ins]
    in_specs += [pl.BlockSpec((k, tn), lambda i, j: (0, j))]
    in_specs += [pl.BlockSpec((tm, tn), lambda i, j: (i, j)) for _ in epi_rows]
    in_specs += [pl.BlockSpec(bs, im) for (_, bs, im) in epi_aux]
    body = functools.partial(_fused_mm_body, nr=len(row_ins), na=len(aux_ins), ner=len(epi_rows),
                             nea=len(epi_aux), prologue=prologue, epilogue=epilogue, out_dtype=out_dtype)
    return pl.pallas_call(
        body,
        grid=(n // tm, m // tn),
        in_specs=in_specs,
        out_specs=pl.BlockSpec((tm, tn), lambda i, j: (i, j)),
        out_shape=jax.ShapeDtypeStruct((n, m), out_dtype),
        scratch_shapes=[pltpu.VMEM((tm, k), BF16)],
        compiler_params=pltpu.CompilerParams(dimension_semantics=("parallel", "arbitrary"),
                                             vmem_limit_bytes=VMEM_LIMIT),
    )(*[a for (a, _, _) in row_ins], *[a for (a, _, _) in aux_ins], w, *epi_rows,
      *[a for (a, _, _) in epi_aux])


def _whole(a):
    nd = a.ndim
    return (a, a.shape, lambda i, j: (0,) * nd)


def _plain_mm(x, w, *, tm, tn, out_dtype=F32):
    return _fused_mm([x], [], w, lambda i, xb: xb, tm=tm, tn=tn, out_dtype=out_dtype)


def _in_proj_body(*refs, tc, shift):
    if shift:
        z_ref, g_ref, shl_ref, scl_ref, shc_ref, scc_ref, w_ref, mup_ref, mun_ref, o_ref, act = refs
    else:
        z_ref, g_ref, shl_ref, scl_ref, shc_ref, scc_ref, w_ref, o_ref, act = refs
    seq = z_ref.shape[0]
    row = lax.broadcasted_iota(jnp.int32, (seq, 1), 0)

    @pl.when(pl.program_id(1) == 0)
    def _():
        xb = z_ref[...]
        y = xb * lax.rsqrt(jnp.mean(xb * xb, axis=-1, keepdims=True) + RMS_EPS) * g_ref[...]
        is_ctx = row < tc
        sc = jnp.where(is_ctx, scc_ref[...], scl_ref[0])
        sh = jnp.where(is_ctx, shc_ref[...], shl_ref[0])
        act[...] = (y * (1.0 + sc) + sh).astype(BF16)

    acc = jnp.dot(act[...], w_ref[...], preferred_element_type=F32)
    if shift:
        prev = pltpu.roll(acc, 1, 0)
        nxt = pltpu.roll(acc, seq - 1, 0)
        prev = jnp.where((row == 0) | (row == tc), 0.0, prev)
        nxt = jnp.where((row == tc - 1) | (row == seq - 1), 0.0, nxt)
        acc = acc + mup_ref[...] * (prev - acc) + mun_ref[...] * (nxt - acc)
    o_ref[...] = acc.astype(o_ref.dtype)


def _in_proj(z, seq, tc, g, sh_l, sc_l, sh_c, sc_c, w, tn, mu_prev=None, mu_next=None, out_dtype=F32):
    n, d = z.shape
    m = w.shape[1]
    shift = mu_prev is not None
    assert m % tn == 0
    vec = lambda width: pl.BlockSpec((1, width), lambda b, j: (0, 0))
    per_b = pl.BlockSpec((1, 1, d), lambda b, j: (b, 0, 0))
    in_specs = [pl.BlockSpec((seq, d), lambda b, j: (b, 0)), vec(d), per_b, per_b, vec(d), vec(d),
                pl.BlockSpec((d, tn), lambda b, j: (0, j))]
    args = [z, g, sh_l, sc_l, sh_c, sc_c, w]
    if shift:
        in_specs += [pl.BlockSpec((1, tn), lambda b, j: (0, j))] * 2
        args += [mu_prev, mu_next]
    return pl.pallas_call(
        functools.partial(_in_proj_body, tc=tc, shift=shift),
        grid=(n // seq, m // tn),
        in_specs=in_specs,
        out_specs=pl.BlockSpec((seq, tn), lambda b, j: (b, j)),
        out_shape=jax.ShapeDtypeStruct((n, m), out_dtype),
        scratch_shapes=[pltpu.VMEM((seq, d), BF16)],
        compiler_params=pltpu.CompilerParams(dimension_semantics=("parallel", "arbitrary"),
                                             vmem_limit_bytes=VMEM_LIMIT),
    )(*args)


def _rwkv_body(rf, kf, vf, lf, rb_, kb, vb_, lb, w0_ref, w2_ref, a0_ref, a2_ref, kkw_ref, ka_ref, bd_ref,
               of_ref, ob_ref, s_ref, kk_ref, *, n_chunks):
    c_len = RWKV_CHUNK

    @pl.when(pl.program_id(1) == 0)
    def _():
        s_ref[...] = jnp.zeros_like(s_ref)

    for d, k_blk in enumerate((kf, kb)):
        kx = k_blk[...].astype(F32) * kkw_ref[...]
        kk_ref[d] = kx * lax.rsqrt(_bdot(kx * kx, bd_ref[...]) + L2_EPS)

    ri = lax.broadcasted_iota(jnp.int32, (c_len, c_len), 0)
    ci = lax.broadcasted_iota(jnp.int32, (c_len, c_len), 1)
    eye = (ri == ci).astype(F32)
    in_refs = ((rf, kf, vf, lf), (rb_, kb, vb_, lb))
    masks = []
    for d in range(2):
        before = (ri > ci) if d == 0 else (ri < ci)
        strict = before.astype(F32)
        incl = (before | (ri == ci)).astype(F32)
        m8 = jnp.where(((ri // 8) == (ci // 8)) & before, 1.0, 0.0).astype(F32)
        merges = []
        for size in (16, 32, 64):
            same = (ri // size) == (ci // size)
            inner = (ri // (size // 2)) == (ci // (size // 2))
            merges.append(jnp.where(same & jnp.logical_not(inner) & before, 1.0, 0.0).astype(F32))
        masks.append(dict(strict=strict, incl=incl, incl_b=incl.astype(BF16), m8=m8, merges=merges))
    out_refs = (of_ref, ob_ref)
    decay_scale = math.exp(-0.5)

    def chunk(c, carry):
        per_dir = []
        for d in range(2):
            cc = c if d == 0 else n_chunks - 1 - c
            sl = pl.ds(pl.multiple_of(cc * c_len, c_len), c_len)
            r_ref, k_ref, v_ref, lo_ref = in_refs[d]
            r = r_ref[sl, :].astype(F32)
            k = k_ref[sl, :].astype(F32)
            v = v_ref[sl, :].astype(F32)
            lo = lo_ref[sl, :].astype(F32)
            wlo = lo[:, d * RWKV_W_LORA:(d + 1) * RWKV_W_LORA]
            alo = lo[:, 2 * RWKV_W_LORA + d * RWKV_A_LORA:2 * RWKV_W_LORA + (d + 1) * RWKV_A_LORA]
            w_raw = w0_ref[d] + _bdot(jnp.tanh(wlo), w2_ref[d])
            lw = -decay_scale * jax.nn.sigmoid(w_raw)
            a = jax.nn.sigmoid(a0_ref[d] + _bdot(alo, a2_ref[d]))
            k_d = k * (1.0 + (a - 1.0) * ka_ref[...])
            kk = kk_ref[d, sl, :]
            hi = lw.astype(BF16)
            lo2 = (lw - hi.astype(F32)).astype(BF16)
            tri = masks[d]['incl_b']
            cum = (jnp.dot(tri, hi, preferred_element_type=F32)
                   + jnp.dot(tri, lo2, preferred_element_type=F32))
            eg = jnp.exp(cum)
            ieg = jnp.exp(-cum)
            last = c_len - 1 if d == 0 else 0
            per_dir.append(dict(
                rt=(r * eg).astype(BF16), kt=(k_d * ieg).astype(BF16), at=(-(kk * a) * ieg).astype(BF16),
                bt=(kk * jnp.exp(cum - lw)).astype(BF16), vb=v.astype(BF16), g_last=eg[last:last + 1, :],
                rows=sl))
        chains = [(d, h) for d in range(2) for h in range(RWKV_H)]
        hsl = lambda h: slice(h * RWKV_HS, (h + 1) * RWKV_HS)
        rb = [jnp.concatenate([per_dir[d]['rt'][:, hsl(h)], per_dir[d]['bt'][:, hsl(h)]], axis=0)
              for d, h in chains]
        at_h = [per_dir[d]['at'][:, hsl(h)] for d, h in chains]
        kt_h = [per_dir[d]['kt'][:, hsl(h)] for d, h in chains]
        v_h = [per_dir[d]['vb'][:, hsl(h)] for d, h in chains]
        mk = [masks[d] for d, h in chains]
        n = len(chains)
        m_a = [lax.dot_general(rb[i], at_h[i], _NT, preferred_element_type=F32) for i in range(n)]
        m_k = [lax.dot_general(rb[i], kt_h[i], _NT, preferred_element_type=F32) for i in range(n)]
        ra = [m_a[i][:c_len] * mk[i]['incl'] for i in range(n)]
        nmat = [m_a[i][c_len:] * mk[i]['strict'] for i in range(n)]
        rk = [m_k[i][:c_len] * mk[i]['incl'] for i in range(n)]
        bk = [m_k[i][c_len:] * mk[i]['strict'] for i in range(n)]
        d1 = [nmat[i] * mk[i]['m8'] for i in range(n)]
        d2 = [_bdot(d1[i], d1[i]) for i in range(n)]
        d4 = [_bdot(d2[i], d2[i]) for i in range(n)]
        tinv = [eye + d1[i] for i in range(n)]
        tinv = [tinv[i] + _bdot(tinv[i], d2[i]) for i in range(n)]
        tinv = [tinv[i] + _bdot(tinv[i], d4[i]) for i in range(n)]
        for lvl in range(3):
            e = [_bdot(tinv[i], nmat[i] * mk[i]['merges'][lvl]) for i in range(n)]
            tinv = [tinv[i] + _bdot(e[i], tinv[i]) for i in range(n)]
        s0 = [s_ref[d, h] for d, h in chains]
        rbs = [lax.dot_general(rb[i], s0[i].astype(BF16), _NT, preferred_element_type=F32) for i in range(n)]
        x = [rbs[i][c_len:] + _bdot(bk[i], v_h[i]) for i in range(n)]
        u = [_bdot(tinv[i], x[i]) for i in range(n)]
        y = [rbs[i][:c_len] + _bdot(ra[i], u[i]) + _bdot(rk[i], v_h[i]) for i in range(n)]
        for i, (d, h) in enumerate(chains):
            uv = jnp.concatenate([u[i].astype(BF16), v_h[i]], axis=0)
            ak = jnp.concatenate([at_h[i], kt_h[i]], axis=0)
            ds = lax.dot_general(uv, ak, _TN, preferred_element_type=F32)
            s_ref[d, h] = (s0[i] + ds) * per_dir[d]['g_last'][:, hsl(h)]
        for d in range(2):
            out_refs[d][per_dir[d]['rows'], :] = jnp.concatenate(
                [y[d * RWKV_H + h] for h in range(RWKV_H)], axis=1)
        return carry

    lax.fori_loop(0, n_chunks, chunk, 0)


def _rwkv_mix(p_rwkv, seq, tc, lp):
    n = p_rwkv.shape[0]
    bsz = n // seq
    tb = _pick_tile(math.gcd(seq, tc), RWKV_TB, RWKV_CHUNK)
    nblk, cblk = seq // tb, tc // tb
    assert LORA_OFF % LORA_W == 0

    def fwd_blk(t):
        return t

    def bwd_blk(t):
        return jnp.where(t < cblk, cblk - 1 - t, nblk - 1 - (t - cblk))

    def col_spec(width, cb, blk_fn):
        return pl.BlockSpec((tb, width), lambda b, t: (b * nblk + blk_fn(t), cb))

    in_specs = []
    for blk_fn in (fwd_blk, bwd_blk):
        in_specs += [col_spec(BR_W, 0, blk_fn), col_spec(BR_W, 1, blk_fn), col_spec(BR_W, 2, blk_fn),
                     col_spec(LORA_W, LORA_OFF // LORA_W, blk_fn)]
    whole = lambda a: pl.BlockSpec(a.shape, lambda b, t: (0,) * a.ndim)
    head_id = jnp.arange(BR_W) // RWKV_HS
    bd = (head_id[:, None] == head_id[None, :]).astype(BF16)
    consts = [lp['rwkv_w0'][:, None, :], lp['rwkv_w2'].astype(BF16), lp['rwkv_a0'][:, None, :],
              lp['rwkv_a2'].astype(BF16), lp['rwkv_k_k'][None, :], lp['rwkv_k_a'][None, :], bd]
    in_specs += [whole(a) for a in consts]
    return pl.pallas_call(
        functools.partial(_rwkv_body, n_chunks=tb // RWKV_CHUNK),
        grid=(bsz, nblk),
        in_specs=in_specs,
        out_specs=[col_spec(BR_W, 0, fwd_blk), col_spec(BR_W, 0, bwd_blk)],
        out_shape=[jax.ShapeDtypeStruct((n, BR_W), F32)] * 2,
        scratch_shapes=[pltpu.VMEM((2, RWKV_H, RWKV_HS, RWKV_HS), F32), pltpu.VMEM((2, tb, BR_W), F32)],
        compiler_params=pltpu.CompilerParams(
            dimension_semantics=("parallel", "arbitrary"), vmem_limit_bytes=VMEM_LIMIT),
    )(*([p_rwkv] * 8), *consts)


def _rwkv_out_proj(y_f, y_b, p_rwkv, lp, w_proj, tm):
    head_id = jnp.arange(BR_W) // RWKV_HS
    bd = (head_id[:, None] == head_id[None, :]).astype(BF16)

    def hsum(xv, bdm):
        hi = xv.astype(BF16)
        lo = (xv - hi.astype(F32)).astype(BF16)
        return (jnp.dot(hi, bdm, preferred_element_type=F32) + jnp.dot(lo, bdm, preferred_element_type=F32))

    def act(i, yf, yb, r, k, v, glo, gn_g, gn_b, r_k, g2, bdm):
        inv = 1.0 / RWKV_HS
        r, k, v, glo = [t.astype(F32) for t in (r, k, v, glo)]
        yb = yf + yb
        yc = yb - hsum(yb, bdm) * inv
        var = hsum(yc * yc, bdm) * inv
        yn = yc * lax.rsqrt(var + RWKV_GN_EPS) * gn_g + gn_b
        bonus = hsum(r * k * r_k, bdm) * v
        g = _bdot(jax.nn.sigmoid(glo), g2)
        return (yn + bonus) * g

    glo_cb = (LORA_OFF + LORA_W) // RWKV_G_LORA
    assert (LORA_OFF + LORA_W) % RWKV_G_LORA == 0
    return _fused_mm(
        [y_f, y_b, (p_rwkv, BR_W, 0), (p_rwkv, BR_W, 1), (p_rwkv, BR_W, 2), (p_rwkv, RWKV_G_LORA, glo_cb)],
        [_whole(lp['rwkv_gn_g'][None, :]), _whole(lp['rwkv_gn_b'][None, :]), _whole(lp['rwkv_r_k'][None, :]),
         _whole(lp['rwkv_g2'].astype(BF16)), _whole(bd)],
        w_proj.astype(BF16), act, tm=tm, tn=w_proj.shape[1], out_dtype=BF16)


LRU_TB = 256
LRU_HALO = 16


def _lru_body(x_ref, g_ref, cw_ref, cb_ref, wr_ref, br_ref, wi_ref, bi_ref, sp_ref, o_ref, h_ref, *,
              tb, nblk, cblk):
    seq = x_ref.shape[0]
    row = lax.broadcasted_iota(jnp.int32, (tb, 1), 0)
    n_ext = tb + LRU_HALO
    for d in range(2):
        h_ref[...] = jnp.zeros_like(h_ref)

        def blk_body(i, carry, d=d):
            if d == 0:
                blk = i
                halo_ok = (blk != 0) & (blk != cblk)
                halo0 = jnp.maximum(blk * tb - LRU_HALO, 0)
            else:
                blk = jnp.where(i < cblk, cblk - 1 - i, nblk - 1 - (i - cblk))
                halo_ok = (blk != cblk - 1) & (blk != nblk - 1)
                halo0 = jnp.minimum(blk * tb + tb, seq - LRU_HALO)
            rows = pl.ds(pl.multiple_of(blk * tb, tb), tb)
            cur = x_ref[rows, :].astype(F32)
            halo = (x_ref[pl.ds(pl.multiple_of(halo0, LRU_HALO), LRU_HALO), :].astype(F32)
                    * jnp.where(halo_ok, 1.0, 0.0))
            xc = cb_ref[d]
            if d == 0:
                ext = jnp.concatenate([halo, cur], axis=0)
                for j in range(LRU_CONV):
                    sh = LRU_CONV - 1 - j
                    tap = ext if sh == 0 else pltpu.roll(ext, sh, 0)
                    xc = xc + cw_ref[d, j:j + 1, :] * tap[LRU_HALO:, :]
            else:
                ext = jnp.concatenate([cur, halo], axis=0)
                for j in range(LRU_CONV):
                    tap = ext if j == 0 else pltpu.roll(ext, n_ext - j, 0)
                    xc = xc + cw_ref[d, j:j + 1, :] * tap[:tb, :]
            gate_r = jax.nn.sigmoid(_bdot(xc, wr_ref[d]) + br_ref[d])
            gate_i = jax.nn.sigmoid(_bdot(xc, wi_ref[d]) + bi_ref[d])
            log_a = -LRU_C * gate_r * sp_ref[d]
            a_cum = jnp.exp(log_a)
            b_cum = jnp.sqrt(1.0 - jnp.exp(2.0 * log_a)) * (gate_i * xc)
            s = 1
            while s < tb:
                if d == 0:
                    ok = row >= s
                    a_sh = jnp.where(ok, pltpu.roll(a_cum, s, 0), 1.0)
                    b_sh = jnp.where(ok, pltpu.roll(b_cum, s, 0), 0.0)
                else:
                    ok = row < tb - s
                    a_sh = jnp.where(ok, pltpu.roll(a_cum, tb - s, 0), 1.0)
                    b_sh = jnp.where(ok, pltpu.roll(b_cum, tb - s, 0), 0.0)
                b_cum = a_cum * b_sh + b_cum
                a_cum = a_cum * a_sh
                s *= 2
            h = b_cum + a_cum * h_ref[...]
            if d == 0:
                h_ref[...] = h[tb - 1:tb, :]
                o_ref[rows, :] = h
            else:
                h_ref[...] = h[0:1, :]
                o_ref[rows, :] = (o_ref[rows, :] + h) * jax.nn.gelu(g_ref[rows, :].astype(F32))
            return carry

        lax.fori_loop(0, nblk, blk_body, 0)


def _lru_mix(p_mid, seq, tc, lp):
    n = p_mid.shape[0]
    tb = _pick_tile(math.gcd(seq, tc), LRU_TB)
    nblk, cblk = seq // tb, tc // tb
    eye = jnp.eye(LRU_BLOCKS, dtype=F32)
    blockdiag = lambda w: jnp.einsum('dgij,gh->dgihj', w, eye).reshape(2, BR_W, BR_W).astype(BF16)
    consts = [lp['lru_conv_w'], lp['lru_conv_b'][:, None, :], blockdiag(lp['lru_wr']), lp['lru_br'][:, None, :],
              blockdiag(lp['lru_wi']), lp['lru_bi'][:, None, :], jax.nn.softplus(-lp['lru_lambda'])[:, None, :]]
    whole = lambda a: pl.BlockSpec(a.shape, lambda b: (0,) * a.ndim)
    x_cb = [name for name, _ in REST_COLS].index('lru_x')
    g_cb = [name for name, _ in REST_COLS].index('lru_gate')
    assert all(w == BR_W for _, w in REST_COLS[:max(x_cb, g_cb) + 1])
    return pl.pallas_call(
        functools.partial(_lru_body, tb=tb, nblk=nblk, cblk=cblk),
        grid=(n // seq,),
        in_specs=[pl.BlockSpec((seq, BR_W), lambda b: (b, x_cb)), pl.BlockSpec((seq, BR_W), lambda b: (b, g_cb))]
        + [whole(a) for a in consts],
        out_specs=pl.BlockSpec((seq, BR_W), lambda b: (b, 0)),
        out_shape=jax.ShapeDtypeStruct((n, BR_W), F32),
        scratch_shapes=[pltpu.VMEM((1, BR_W), F32)],
        compiler_params=pltpu.CompilerParams(dimension_semantics=("parallel",), vmem_limit_bytes=VMEM_LIMIT),
    )(p_mid, p_mid, *consts)


CONV_TB = 256
CONV_HALO = 16


def _conv_body(val_ref, gate_ref, w_ref, b_ref, lng_ref, lnb_ref, o_ref, *, tb, nblk, cblk):
    seq = val_ref.shape[0]
    n_ext = tb + 2 * CONV_HALO
    pad = CONV_K // 2

    def glu(rows):
        return val_ref[rows, :].astype(F32) * jax.nn.sigmoid(gate_ref[rows, :].astype(F32))

    def blk_body(blk, carry):
        t0 = blk * tb
        rows = pl.ds(pl.multiple_of(t0, tb), tb)
        lo_ok = (blk != 0) & (blk != cblk)
        hi_ok = (blk != cblk - 1) & (blk != nblk - 1)
        lo0 = jnp.maximum(t0 - CONV_HALO, 0)
        hi0 = jnp.minimum(t0 + tb, seq - CONV_HALO)
        lo = glu(pl.ds(pl.multiple_of(lo0, CONV_HALO), CONV_HALO)) * jnp.where(lo_ok, 1.0, 0.0)
        hi = glu(pl.ds(pl.multiple_of(hi0, CONV_HALO), CONV_HALO)) * jnp.where(hi_ok, 1.0, 0.0)
        ext = jnp.concatenate([lo, glu(rows), hi], axis=0)
        acc = jnp.zeros((tb, ext.shape[1]), F32) + b_ref[...]
        for r in range(8):
            rolled = ext if r == 0 else pltpu.roll(ext, n_ext - r, 0)
            for j in range(CONV_K):
                off = CONV_HALO - pad + j
                if off % 8 == r:
                    acc = acc + w_ref[j:j + 1, :] * rolled[off - r:off - r + tb, :]
        mu = jnp.mean(acc, axis=-1, keepdims=True)
        xc = acc - mu
        yn = xc * lax.rsqrt(jnp.mean(xc * xc, axis=-1, keepdims=True) + LN_EPS) * lng_ref[...] + lnb_ref[...]
        o_ref[rows, :] = yn * jax.nn.sigmoid(yn)
        return carry

    lax.fori_loop(0, nblk, blk_body, 0)


def _conv_mix(p_mid, seq, tc, lp):
    n = p_mid.shape[0]
    tb = _pick_tile(math.gcd(seq, tc), CONV_TB, CONV_HALO)
    names = [name for name, _ in REST_COLS]
    v_cb, g_cb = names.index('conv_val'), names.index('conv_gate')
    assert all(w == BR_W for _, w in REST_COLS[:max(v_cb, g_cb) + 1]) and CONV_HALO >= CONV_K // 2
    consts = [lp['conv_w'], lp['conv_b'][None, :], lp['conv_ln_g'][None, :], lp['conv_ln_b'][None, :]]
    whole = lambda a: pl.BlockSpec(a.shape, lambda b: (0,) * a.ndim)
    return pl.pallas_call(
        functools.partial(_conv_body, tb=tb, nblk=seq // tb, cblk=tc // tb),
        grid=(n // seq,),
        in_specs=[pl.BlockSpec((seq, BR_W), lambda b: (b, v_cb)), pl.BlockSpec((seq, BR_W), lambda b: (b, g_cb))]
        + [whole(a) for a in consts],
        out_specs=pl.BlockSpec((seq, BR_W), lambda b: (b, 0)),
        out_shape=jax.ShapeDtypeStruct((n, BR_W), F32),
        compiler_params=pltpu.CompilerParams(dimension_semantics=("parallel",), vmem_limit_bytes=VMEM_LIMIT),
    )(p_mid, p_mid, *consts)


RET_TB = 256
_RET_LOG_G = [math.log1p(-2.0 ** (-5.0 - h)) for h in range(RET_H)]


def _ret_body(qf, kf, vf, cf, sf, qb, kb, vb, cb, sb, dm_ref, xi_ref, zt_ref, of_ref, ob_ref, s_ref, *, n_chunks):
    c_len = RET_CHUNK
    qk_w = RET_H * RET_DK

    @pl.when(pl.program_id(1) == 0)
    def _():
        s_ref[...] = jnp.zeros_like(s_ref)

    lane = lax.broadcasted_iota(jnp.int32, (c_len, qk_w), 1)
    first_half = (lane % RET_DK) < (RET_DK // 2)

    def rotary(z, cos, sin):
        swapped = jnp.where(first_half, pltpu.roll(z, qk_w - RET_DK // 2, 1), pltpu.roll(z, RET_DK // 2, 1))
        return z * cos + swapped * sin

    in_refs = ((qf, kf, vf, cf, sf), (qb, kb, vb, cb, sb))
    out_refs = (of_ref, ob_ref)

    def chunk(c, carry):
        per_dir = []
        for d in range(2):
            cc = c if d == 0 else n_chunks - 1 - c
            sl = pl.ds(pl.multiple_of(cc * c_len, c_len), c_len)
            q_ref, k_ref, v_ref, c_ref, sn_ref = in_refs[d]
            cos, sin = c_ref[sl, :], sn_ref[sl, :]
            q = rotary(q_ref[sl, :].astype(F32), cos, sin)
            k = rotary(k_ref[sl, :].astype(F32), cos, sin) * (RET_DK ** -0.5)
            per_dir.append(dict(q=q.astype(BF16), k=k.astype(BF16), kz=(k * zt_ref[d]).astype(BF16),
                                v=v_ref[sl, :].astype(BF16), sl=sl))
        groups = [(d, h) for d in range(2) for h in range(RET_H)]
        ksl = lambda h: slice(h * RET_DK, (h + 1) * RET_DK)
        vsl = lambda h: slice(h * RET_DV, (h + 1) * RET_DV)
        q_h = [per_dir[d]['q'][:, ksl(h)] for d, h in groups]
        v_h = [per_dir[d]['v'][:, vsl(h)] for d, h in groups]
        att = [lax.dot_general(q_h[i], per_dir[d]['k'][:, ksl(h)], _NT, preferred_element_type=F32) * dm_ref[d, h]
               for i, (d, h) in enumerate(groups)]
        s0 = [s_ref[d, h] for d, h in groups]
        o = [_bdot(att[i], v_h[i]) + _bdot(q_h[i], s0[i]) * xi_ref[d][:, vsl(h)]
             for i, (d, h) in enumerate(groups)]
        for i, (d, h) in enumerate(groups):
            kv = lax.dot_general(per_dir[d]['kz'][:, ksl(h)], v_h[i], _TN, preferred_element_type=F32)
            s_ref[d, h] = s0[i] * math.exp(_RET_LOG_G[h] * c_len) + kv
        for d in range(2):
            out_refs[d][per_dir[d]['sl'], :] = jnp.concatenate([o[d * RET_H + h] for h in range(RET_H)], axis=1)
        return carry

    lax.fori_loop(0, n_chunks, chunk, 0)


def _ret_mix(p_mid, seq, tc, n_tok):
    n = p_mid.shape[0]
    tb = _pick_tile(math.gcd(seq, tc), RET_TB, RET_CHUNK)
    nblk, cblk = seq // tb, tc // tb
    c_len = RET_CHUNK
    n_rows = n_tok // GRID_W
    rows = jnp.repeat(jnp.arange(n_rows, dtype=F32), GRID_W)
    cols = jnp.tile(jnp.arange(GRID_W, dtype=F32), n_rows)
    n_freq = RET_DK // 4
    inv = ROPE_BASE ** (-jnp.arange(n_freq, dtype=F32) / n_freq)
    ang = jnp.concatenate([rows[:, None] * inv, cols[:, None] * inv], axis=-1)
    cos_h = jnp.concatenate([jnp.cos(ang), jnp.cos(ang)], axis=-1)
    sin_h = jnp.concatenate([-jnp.sin(ang), jnp.sin(ang)], axis=-1)
    pad_ctx = lambda tbl, fill: jnp.concatenate(
        [jnp.full((tc, RET_H * RET_DK), fill, F32), jnp.tile(tbl, (1, RET_H))], axis=0)
    cos_t, sin_t = pad_ctx(cos_h, 1.0), pad_ctx(sin_h, 0.0)
    log_g = jnp.asarray(_RET_LOG_G, F32)
    idx = jnp.arange(c_len, dtype=F32)
    dm, xi, zt = [], [], []
    for d in range(2):
        pos = idx if d == 0 else c_len - 1.0 - idx
        diff = pos[:, None] - pos[None, :]
        keep = diff >= 0 if d == 0 else diff > 0
        dm.append(jnp.where(keep[None], jnp.exp(log_g[:, None, None] * jnp.maximum(diff, 0.0)[None]), 0.0))
        xi.append(jnp.repeat(jnp.exp(log_g[None, :] * (pos[:, None] + 1.0)), RET_DV, axis=1))
        zt.append(jnp.repeat(jnp.exp(log_g[None, :] * (c_len - 1.0 - pos)[:, None]), RET_DK, axis=1))
    consts = [jnp.stack(dm), jnp.stack(xi), jnp.stack(zt)]

    def fwd_blk(t):
        return t

    def bwd_blk(t):
        return jnp.where(t < cblk, cblk - 1 - t, nblk - 1 - (t - cblk))

    names = [name for name, _ in REST_COLS]
    offs = {name: sum(w for _, w in REST_COLS[:i]) for i, (name, _) in enumerate(REST_COLS)}
    qk_w = RET_H * RET_DK
    in_specs = []
    for blk_fn in (fwd_blk, bwd_blk):
        spec = lambda width, cb, blk_fn=blk_fn: pl.BlockSpec((tb, width), lambda b, t: (b * nblk + blk_fn(t), cb))
        tbl = lambda blk_fn=blk_fn: pl.BlockSpec((tb, qk_w), lambda b, t: (blk_fn(t), 0))
        assert offs['ret_q'] % qk_w == 0 and offs['ret_k'] % qk_w == 0 and offs['ret_v'] % BR_W == 0
        in_specs += [spec(qk_w, offs['ret_q'] // qk_w), spec(qk_w, offs['ret_k'] // qk_w),
                     spec(BR_W, offs['ret_v'] // BR_W), tbl(), tbl()]
    whole = lambda a: pl.BlockSpec(a.shape, lambda b, t: (0,) * a.ndim)
    in_specs += [whole(a) for a in consts]
    out_spec = lambda blk_fn: pl.BlockSpec((tb, BR_W), lambda b, t: (b * nblk + blk_fn(t), 0))
    return pl.pallas_call(
        functools.partial(_ret_body, n_chunks=tb // c_len),
        grid=(n // seq, nblk),
        in_specs=in_specs,
        out_specs=[out_spec(fwd_blk), out_spec(bwd_blk)],
        out_shape=[jax.ShapeDtypeStruct((n, BR_W), F32)] * 2,
        scratch_shapes=[pltpu.VMEM((2, RET_H, RET_DK, RET_DV), F32)],
        compiler_params=pltpu.CompilerParams(
            dimension_semantics=("parallel", "arbitrary"), vmem_limit_bytes=VMEM_LIMIT),
    )(*([p_mid, p_mid, p_mid, cos_t, sin_t] * 2), *consts)


def _ret_out_proj(o_f, o_b, p_mid, lp, w_proj, tm):
    def act(i, of, ob, g, gn_g, gn_b):
        o = of + ob
        g = g.astype(F32)
        normed = []
        for h in range(RET_H):
            oh = o[:, h * RET_DV:(h + 1) * RET_DV]
            oc = oh - jnp.mean(oh, axis=-1, keepdims=True)
            normed.append(oc * lax.rsqrt(jnp.mean(oc * oc, axis=-1, keepdims=True) + LN_EPS))
        yn = jnp.concatenate(normed, axis=1) * gn_g + gn_b
        return g * jax.nn.sigmoid(g) * yn

    off_g = sum(w for _, w in REST_COLS[:[name for name, _ in REST_COLS].index('ret_g')])
    assert off_g % BR_W == 0
    return _fused_mm([o_f, o_b, (p_mid, BR_W, off_g // BR_W)],
                     [_whole(lp['ret_gn_g'][None, :]), _whole(lp['ret_gn_b'][None, :])],
                     w_proj.astype(BF16), act, tm=tm, tn=w_proj.shape[1], out_dtype=BF16)


ROUTER_PAD = 128


def _norm2_router_body(z_ref, g_ref, shl_ref, scl_ref, shc_ref, scc_ref, whi_ref, wlo_ref, tok_ref, log_ref, *,
                       skip, cblk):
    xb = z_ref[...]
    y = xb * lax.rsqrt(jnp.mean(xb * xb, axis=-1, keepdims=True) + RMS_EPS) * g_ref[...]
    is_ctx = (pl.program_id(1) + skip) < cblk
    sc = jnp.where(is_ctx, scc_ref[...], scl_ref[0])
    sh = jnp.where(is_ctx, shc_ref[...], shl_ref[0])
    h = y * (1.0 + sc) + sh
    hi = h.astype(BF16)
    lo = (h - hi.astype(F32)).astype(BF16)
    tok_ref[...] = hi
    log_ref[...] = (jnp.dot(hi, whi_ref[...], preferred_element_type=F32)
                    + jnp.dot(lo, whi_ref[...], preferred_element_type=F32)
                    + jnp.dot(hi, wlo_ref[...], preferred_element_type=F32))


def _norm2_router(z, seq, tc, g, sh_l, sc_l, sh_c, sc_c, w_router, latent_only):
    n, d = z.shape
    bsz = n // seq
    tb = _pick_tile(math.gcd(seq, tc), 256, 16)
    nblk, cblk = seq // tb, tc // tb
    skip = cblk if latent_only else 0
    nb = nblk - skip
    w_pad = jnp.pad(w_router, ((0, 0), (0, ROUTER_PAD - w_router.shape[1])))
    w_hi = w_pad.astype(BF16)
    w_lo = (w_pad - w_hi.astype(F32)).astype(BF16)
    vec = pl.BlockSpec((1, d), lambda b, i: (0, 0))
    per_b = pl.BlockSpec((1, 1, d), lambda b, i: (b, 0, 0))
    wspec = pl.BlockSpec((d, ROUTER_PAD), lambda b, i: (0, 0))
    return pl.pallas_call(
        functools.partial(_norm2_router_body, skip=skip, cblk=cblk),
        grid=(bsz, nb),
        in_specs=[pl.BlockSpec((tb, d), lambda b, i: (b * nblk + skip + i, 0)), vec, per_b, per_b, vec, vec,
                  wspec, wspec],
        out_specs=[pl.BlockSpec((tb, d), lambda b, i: (b * nb + i, 0)),
                   pl.BlockSpec((tb, ROUTER_PAD), lambda b, i: (b * nb + i, 0))],
        out_shape=[jax.ShapeDtypeStruct((bsz * nb * tb, d), BF16),
                   jax.ShapeDtypeStruct((bsz * nb * tb, ROUTER_PAD), F32)],
        compiler_params=pltpu.CompilerParams(dimension_semantics=("parallel", "parallel"),
                                             vmem_limit_bytes=VMEM_LIMIT),
    )(z, g, sh_l, sc_l, sh_c, sc_c, w_hi, w_lo)


def _moe_body(be_ref, nu_ref, x_ref, w1_ref, w3_ref, w2_ref, o_ref):
    i = pl.program_id(0)

    @pl.when(i < nu_ref[0])
    def _():
        xb = x_ref[...]
        h1 = jnp.dot(xb, w1_ref[0].astype(BF16), preferred_element_type=F32)
        h3 = jnp.dot(xb, w3_ref[0].astype(BF16), preferred_element_type=F32)
        hid = (h1 * jax.nn.sigmoid(h1) * h3).astype(BF16)
        o_ref[...] = jnp.dot(hid, w2_ref[0].astype(BF16), preferred_element_type=F32)

    @pl.when(i >= nu_ref[0])
    def _():
        o_ref[...] = jnp.zeros_like(o_ref)


def _moe_experts(xb, block_e, n_used, w1, w3, w2, layer):
    n_rows, d = xb.shape
    n_blocks = n_rows // MOE_BM
    de = w1.shape[3]
    grid_spec = pltpu.PrefetchScalarGridSpec(
        num_scalar_prefetch=2,
        grid=(n_blocks,),
        in_specs=[
            pl.BlockSpec((MOE_BM, d), lambda i, be, nu: (i, 0)),
            pl.BlockSpec((None, 1, d, de), lambda i, be, nu: (layer, be[i], 0, 0)),
            pl.BlockSpec((None, 1, d, de), lambda i, be, nu: (layer, be[i], 0, 0)),
            pl.BlockSpec((None, 1, de, d), lambda i, be, nu: (layer, be[i], 0, 0)),
        ],
        out_specs=pl.BlockSpec((MOE_BM, d), lambda i, be, nu: (i, 0)),
    )
    return pl.pallas_call(
        _moe_body,
        grid_spec=grid_spec,
        out_shape=jax.ShapeDtypeStruct((n_rows, d), F32),
        compiler_params=pltpu.CompilerParams(dimension_semantics=("arbitrary",),
                                             vmem_limit_bytes=VMEM_LIMIT),
    )(block_e, n_used, xb, w1, w3, w2)


def _moe(tokens, logits, b_router, w1, w3, w2, layer):
    n_tok = tokens.shape[0]
    aff = jax.nn.sigmoid(logits[:, :N_EXPERTS])
    assert TOP_K == 2
    sel = (aff + b_router).reshape(-1, N_GROUPS, EXPERTS_PER_GROUP)

    def top2(v):
        lane = jnp.arange(v.shape[-1], dtype=jnp.int32)
        i1 = jnp.argmax(v, axis=-1).astype(jnp.int32)
        rest = jnp.where(lane == i1[..., None], -jnp.inf, v)
        i2 = jnp.argmax(rest, axis=-1).astype(jnp.int32)
        return jnp.max(v, axis=-1), jnp.max(rest, axis=-1), i1, i2

    g1, g2, _, _ = top2(sel)
    grp = jnp.argmax(g1 + g2, axis=-1)
    grp_mask = jnp.arange(N_GROUPS)[None, :] == grp[:, None]
    masked = jnp.where(grp_mask[:, :, None], sel, -jnp.inf).reshape(-1, N_EXPERTS)
    _, _, e1, e2 = top2(masked)
    e_idx = jnp.stack([e1, e2], axis=-1)
    wts = jnp.take_along_axis(aff, e_idx, axis=1)
    wts = wts / jnp.sum(wts, axis=-1, keepdims=True)

    n_asg = n_tok * TOP_K
    flat_e = e_idx.reshape(-1).astype(jnp.int32)
    order = jnp.argsort(flat_e).astype(jnp.int32)
    se, st = flat_e[order], order // TOP_K
    counts = jnp.sum(flat_e[:, None] == jnp.arange(N_EXPERTS, dtype=jnp.int32)[None, :], axis=0, dtype=jnp.int32)
    starts = jnp.cumsum(counts) - counts
    padded = (counts + MOE_BM - 1) // MOE_BM * MOE_BM
    pad_end = jnp.cumsum(padded)
    pad_start = pad_end - padded
    dest = pad_start[se] + jnp.arange(n_asg, dtype=jnp.int32) - starts[se]
    dest_asg = dest[jnp.argsort(order)].reshape(n_tok, TOP_K)
    n_blocks = -(-n_asg // MOE_BM) + N_EXPERTS
    block_e = jnp.minimum(jnp.searchsorted(pad_end, jnp.arange(n_blocks) * MOE_BM, side='right'),
                          N_EXPERTS - 1).astype(jnp.int32)
    n_used = (pad_end[-1] // MOE_BM).astype(jnp.int32).reshape(1)
    slot_e = jnp.repeat(block_e, MOE_BM)
    rank = jnp.arange(n_blocks * MOE_BM, dtype=jnp.int32) - pad_start[slot_e]
    slot_tok = st[jnp.clip(starts[slot_e] + rank, 0, n_asg - 1)]
    xb = tokens[slot_tok]
    yb = _moe_experts(xb, block_e, n_used, w1, w3, w2, layer)
    return wts[:, 0:1] * yb[dest_asg[:, 0]] + wts[:, 1:2] * yb[dest_asg[:, 1]]


def _merge_out(z, merge, branches, b_merge, w_out_l, g1, cg1, seq, tc, d):
    tm2 = _pick_tile(seq, 576, 16)
    bpb = seq // tm2

    def seg_select(i, ctx_val, lat_val):
        row = (i % bpb) * tm2 + lax.broadcasted_iota(jnp.int32, (tm2, 1), 0)
        return jnp.where(row < tc, ctx_val, lat_val)

    def merge_pro(i, mg, b0, b1, b2, b3, bm):
        gates = jax.nn.sigmoid(mg.astype(F32) + bm)
        b0, b1, b2, b3 = [b.astype(F32) for b in (b0, b1, b2, b3)]
        return (gates[:, 0 * d:1 * d] * b0 + gates[:, 1 * d:2 * d] * b1
                + gates[:, 2 * d:3 * d] * b2 + gates[:, 3 * d:4 * d] * b3)

    def resid_epi(i, acc, zb, gl, gc):
        return zb + seg_select(i, gc, gl[0]) * acc

    tn = d // 2
    return _fused_mm([merge] + branches, [_whole(b_merge[None, :])],
                     w_out_l.astype(BF16), merge_pro, tm=tm2, tn=tn,
                     epilogue=resid_epi, epi_rows=[z],
                     epi_aux=[(g1, (1, 1, tn), lambda i, j: (i // bpb, 0, j)),
                              (cg1, (1, tn), lambda i, j: (0, j))])


def kernel(x, c, ctx, c_ctx, w_ada, b_ada, norm1_g, norm2_g, w_in, b_merge, rwkv_mu_prev, rwkv_mu_next, rwkv_w0, rwkv_w2, rwkv_a0, rwkv_a2, rwkv_g2, rwkv_k_k, rwkv_k_a, rwkv_r_k, rwkv_gn_g, rwkv_gn_b, rwkv_proj, conv_w, conv_b, conv_ln_g, conv_ln_b, conv_proj, lru_conv_w, lru_conv_b, lru_wr, lru_br, lru_wi, lru_bi, lru_lambda, lru_proj, ret_gn_g, ret_gn_b, ret_proj, w_out, w_router, b_router, e_w1, e_w3, e_w2, final_g):
    bsz, n_tok, d = x.shape
    tc = ctx.shape[1]
    seq = tc + n_tok
    depth = w_in.shape[0]
    per_layer = {
        'rwkv_w0': rwkv_w0, 'rwkv_w2': rwkv_w2, 'rwkv_a0': rwkv_a0, 'rwkv_a2': rwkv_a2, 'rwkv_g2': rwkv_g2,
        'rwkv_k_k': rwkv_k_k, 'rwkv_k_a': rwkv_k_a, 'rwkv_r_k': rwkv_r_k,
        'rwkv_gn_g': rwkv_gn_g, 'rwkv_gn_b': rwkv_gn_b,
        'conv_w': conv_w, 'conv_b': conv_b, 'conv_ln_g': conv_ln_g, 'conv_ln_b': conv_ln_b,
        'lru_conv_w': lru_conv_w, 'lru_conv_b': lru_conv_b, 'lru_wr': lru_wr, 'lru_br': lru_br,
        'lru_wi': lru_wi, 'lru_bi': lru_bi, 'lru_lambda': lru_lambda,
        'ret_gn_g': ret_gn_g, 'ret_gn_b': ret_gn_b,
    }
    tm = _pick_tile(seq, 576)

    z = jnp.concatenate([ctx, x], axis=1).reshape(bsz * seq, d)
    for layer in range(depth):
        lp = {name: arr[layer] for name, arr in per_layer.items()}
        last = layer == depth - 1
        cc = jnp.concatenate([c, c_ctx[None, :]], axis=0)
        n_mod = -(-cc.shape[0] // 8) * 8
        cc = jnp.pad(cc, ((0, n_mod - cc.shape[0]), (0, 0)))
        mod = _fused_mm([cc], [], w_ada[layer].astype(BF16), lambda i, cb: cb * jax.nn.sigmoid(cb),
                        tm=n_mod, tn=6 * d // 4,
                        epilogue=lambda i, acc, bb: acc + bb,
                        epi_aux=[(b_ada[layer][None, :], (1, 6 * d // 4), lambda i, j: (0, j))])
        mod_l = mod[:bsz].reshape(bsz, 1, 6 * d)
        mod_c = mod[bsz:bsz + 1]
        sh1, sc1, g1, sh2, sc2, g2 = [mod_l[:, :, n * d:(n + 1) * d] for n in range(6)]
        csh1, csc1, cg1, csh2, csc2, cg2 = [mod_c[:, n * d:(n + 1) * d] for n in range(6)]

        w_l = w_in[layer].astype(BF16)
        norm_args = (norm1_g[layer][None, :], sh1, sc1, csh1, csc1)
        p_rwkv = _in_proj(z, seq, tc, *norm_args, w_l[:, :RWKV_WIDTH], 640,
                          rwkv_mu_prev[layer][None, :], rwkv_mu_next[layer][None, :], out_dtype=BF16)
        p_mid = _in_proj(z, seq, tc, *norm_args, w_l[:, RWKV_WIDTH:RWKV_WIDTH + MID_WIDTH], 512, out_dtype=BF16)
        p_merge = _in_proj(z, seq, tc, *norm_args, w_l[:, RWKV_WIDTH + MID_WIDTH:], 512, out_dtype=BF16)

        y_f, y_b = _rwkv_mix(p_rwkv, seq, tc, lp)
        o_f, o_b = _ret_mix(p_mid, seq, tc, n_tok)
        branches = [
            _rwkv_out_proj(y_f, y_b, p_rwkv, lp, rwkv_proj[layer], tm),
            _plain_mm(_conv_mix(p_mid, seq, tc, lp), conv_proj[layer].astype(BF16), tm=tm, tn=d, out_dtype=BF16),
            _plain_mm(_lru_mix(p_mid, seq, tc, lp), lru_proj[layer].astype(BF16), tm=tm, tn=d, out_dtype=BF16),
            _ret_out_proj(o_f, o_b, p_mid, lp, ret_proj[layer], tm),
        ]
        z = _merge_out(z, p_merge, branches, b_merge[layer], w_out[layer], g1, cg1, seq, tc, d)

        z3 = z.reshape(bsz, seq, d)
        tok, logits = _norm2_router(z, seq, tc, norm2_g[layer][None, :], sh2, sc2, csh2, csc2, w_router,
                                    latent_only=last)
        y = _moe(tok, logits, b_router, e_w1, e_w3, e_w2, layer)
        if not last:
            is_ctx = (jnp.arange(seq) < tc)[None, :, None]
            gate2 = jnp.where(is_ctx, cg2[None], g2)
            z = (z3 + gate2 * y.reshape(bsz, seq, d)).reshape(bsz * seq, d)
        else:
            xl = z3[:, tc:] + g2 * y.reshape(bsz, n_tok, d)
            return xl * lax.rsqrt(jnp.mean(xl * xl, axis=-1, keepdims=True) + RMS_EPS) * final_g
    return None
```

```python
import functools
import math

import jax
import jax.numpy as jnp
from jax import lax
from jax.experimental import pallas as pl
from jax.experimental.pallas import tpu as pltpu

F32 = jnp.float32
BF16 = jnp.bfloat16

D_MODEL = 1024
GRID_W = 64
N_BRANCH = 4
BR_W = D_MODEL // 2
RWKV_HS = 64
RWKV_H = BR_W // RWKV_HS
RWKV_W_LORA = 64
RWKV_A_LORA = 64
RWKV_G_LORA = 128
RWKV_GN_EPS = 64e-5
L2_EPS = 1e-12
CONV_K = 31
LRU_BLOCKS = 8
LRU_BS = BR_W // LRU_BLOCKS
LRU_CONV = 4
LRU_C = 8.0
RET_H = 4
RET_DK = 64
RET_DV = BR_W // RET_H
RET_CHUNK = 128
ROPE_BASE = 10000.0
N_EXPERTS = 32
N_GROUPS = 4
EXPERTS_PER_GROUP = N_EXPERTS // N_GROUPS
TOP_K = 2
D_EXPERT = D_MODEL // 2
RMS_EPS = 1e-6
LN_EPS = 1e-5

REST_COLS = (('conv_val', BR_W), ('conv_gate', BR_W), ('lru_x', BR_W), ('lru_gate', BR_W),
             ('ret_q', RET_H * RET_DK), ('ret_k', RET_H * RET_DK), ('ret_v', BR_W), ('ret_g', BR_W))
RWKV_WIDTH = 3 * BR_W + 2 * RWKV_W_LORA + 2 * RWKV_A_LORA + RWKV_G_LORA
LORA_OFF = 3 * BR_W
LORA_W = 2 * RWKV_W_LORA + 2 * RWKV_A_LORA
MID_WIDTH = sum(w for _, w in REST_COLS)
MERGE_WIDTH = N_BRANCH * D_MODEL

VMEM_LIMIT = 48 * 1024 * 1024
RWKV_CHUNK = 64
RWKV_TB = 256
MOE_BM = 256


def _bdot(a, b):
    return jnp.dot(a.astype(BF16), b.astype(BF16), preferred_element_type=F32)


_NT = (((1,), (1,)), ((), ()))
_TN = (((0,), (0,)), ((), ()))


def _pick_tile(n, cap, mult=8):
    best = None
    for t in range(mult, cap + 1, mult):
        if n % t == 0:
            best = t
    assert best is not None, (n, cap, mult)
    return best


def _fused_mm_body(*refs, nr, na, ner, nea, prologue, epilogue, out_dtype):
    rows = refs[:nr]
    auxs = refs[nr:nr + na]
    w_ref = refs[nr + na]
    er = refs[nr + na + 1:nr + na + 1 + ner]
    ea = refs[nr + na + 1 + ner:nr + na + 1 + ner + nea]
    o_ref, act = refs[-2], refs[-1]
    i = pl.program_id(0)

    @pl.when(pl.program_id(1) == 0)
    def _():
        act[...] = prologue(i, *[r[...] for r in rows], *[a[...] for a in auxs]).astype(BF16)

    acc = jnp.dot(act[...], w_ref[...], preferred_element_type=F32)
    if epilogue is not None:
        acc = epilogue(i, acc, *[r[...] for r in er], *[a[...] for a in ea])
    o_ref[...] = acc.astype(out_dtype)


def _fused_mm(row_ins, aux_ins, w, prologue, *, tm, tn, out_dtype=F32, epilogue=None,
              epi_rows=(), epi_aux=()):
    row_ins = [r if isinstance(r, tuple) else (r, r.shape[1], 0) for r in row_ins]
    n = row_ins[0][0].shape[0]
    k, m = w.shape
    assert n % tm == 0 and m % tn == 0, (n, tm, m, tn)
    in_specs = [pl.BlockSpec((tm, bw), functools.partial(lambda i, j, cb: (i, cb), cb=cb))
                for (_, bw, cb) in row_ins]
    in_specs += [pl.BlockSpec(bs, im) for (_, bs, im) in aux_ins]
    in_specs += [pl.BlockSpec((k, tn), lambda i, j: (0, j))]
    in_specs += [pl.BlockSpec((tm, tn), lambda i, j: (i, j)) for _ in epi_rows]
    in_specs += [pl.BlockSpec(bs, im) for (_, bs, im) in epi_aux]
    body = functools.partial(_fused_mm_body, nr=len(row_ins), na=len(aux_ins), ner=len(epi_rows),
                             nea=len(epi_aux), prologue=prologue, epilogue=epilogue, out_dtype=out_dtype)
    return pl.pallas_call(
        body,
        grid=(n // tm, m // tn),
        in_specs=in_specs,
        out_specs=pl.BlockSpec((tm, tn), lambda i, j: (i, j)),
        out_shape=jax.ShapeDtypeStruct((n, m), out_dtype),
        scratch_shapes=[pltpu.VMEM((tm, k), BF16)],
        compiler_params=pltpu.CompilerParams(dimension_semantics=("parallel", "arbitrary"),
                                             vmem_limit_bytes=VMEM_LIMIT),
    )(*[a for (a, _, _) in row_ins], *[a for (a, _, _) in aux_ins], w, *epi_rows,
      *[a for (a, _, _) in epi_aux])


def _whole(a):
    nd = a.ndim
    return (a, a.shape, lambda i, j: (0,) * nd)


def _plain_mm(x, w, *, tm, tn, out_dtype=F32):
    return _fused_mm([x], [], w, lambda i, xb: xb, tm=tm, tn=tn, out_dtype=out_dtype)


def _in_proj_body(*refs, tc, shift):
    if shift:
        z_ref, g_ref, shl_ref, scl_ref, shc_ref, scc_ref, w_ref, mup_ref, mun_ref, o_ref, act = refs
    else:
        z_ref, g_ref, shl_ref, scl_ref, shc_ref, scc_ref, w_ref, o_ref, act = refs
    seq = z_ref.shape[0]
    row = lax.broadcasted_iota(jnp.int32, (seq, 1), 0)

    @pl.when(pl.program_id(1) == 0)
    def _():
        xb = z_ref[...]
        y = xb * lax.rsqrt(jnp.mean(xb * xb, axis=-1, keepdims=True) + RMS_EPS) * g_ref[...]
        is_ctx = row < tc
        sc = jnp.where(is_ctx, scc_ref[...], scl_ref[0])
        sh = jnp.where(is_ctx, shc_ref[...], shl_ref[0])
        act[...] = (y * (1.0 + sc) + sh).astype(BF16)

    acc = jnp.dot(act[...], w_ref[...], preferred_element_type=F32)
    if shift:
        prev = pltpu.roll(acc, 1, 0)
        nxt = pltpu.roll(acc, seq - 1, 0)
        prev = jnp.where((row == 0) | (row == tc), 0.0, prev)
        nxt = jnp.where((row == tc - 1) | (row == seq - 1), 0.0, nxt)
        acc = acc + mup_ref[...] * (prev - acc) + mun_ref[...] * (nxt - acc)
    o_ref[...] = acc.astype(o_ref.dtype)


def _in_proj(z, seq, tc, g, sh_l, sc_l, sh_c, sc_c, w, tn, mu_prev=None, mu_next=None, out_dtype=F32):
    n, d = z.shape
    m = w.shape[1]
    shift = mu_prev is not None
    assert m % tn == 0
    vec = lambda width: pl.BlockSpec((1, width), lambda b, j: (0, 0))
    per_b = pl.BlockSpec((1, 1, d), lambda b, j: (b, 0, 0))
    in_specs = [pl.BlockSpec((seq, d), lambda b, j: (b, 0)), vec(d), per_b, per_b, vec(d), vec(d),
                pl.BlockSpec((d, tn), lambda b, j: (0, j))]
    args = [z, g, sh_l, sc_l, sh_c, sc_c, w]
    if shift:
        in_specs += [pl.BlockSpec((1, tn), lambda b, j: (0, j))] * 2
        args += [mu_prev, mu_next]
    return pl.pallas_call(
        functools.partial(_in_proj_body, tc=tc, shift=shift),
        grid=(n // seq, m // tn),
        in_specs=in_specs,
        out_specs=pl.BlockSpec((seq, tn), lambda b, j: (b, j)),
        out_shape=jax.ShapeDtypeStruct((n, m), out_dtype),
        scratch_shapes=[pltpu.VMEM((seq, d), BF16)],
        compiler_params=pltpu.CompilerParams(dimension_semantics=("parallel", "arbitrary"),
                                             vmem_limit_bytes=VMEM_LIMIT),
    )(*args)


def _rwkv_body(rf, kf, vf, lf, rb_, kb, vb_, lb, w0_ref, w2_ref, a0_ref, a2_ref, kkw_ref, ka_ref, bd_ref,
               of_ref, ob_ref, s_ref, kk_ref, *, n_chunks):
    c_len = RWKV_CHUNK

    @pl.when(pl.program_id(1) == 0)
    def _():
        s_ref[...] = jnp.zeros_like(s_ref)

    for d, k_blk in enumerate((kf, kb)):
        kx = k_blk[...].astype(F32) * kkw_ref[...]
        kk_ref[d] = kx * lax.rsqrt(_bdot(kx * kx, bd_ref[...]) + L2_EPS)

    ri = lax.broadcasted_iota(jnp.int32, (c_len, c_len), 0)
    ci = lax.broadcasted_iota(jnp.int32, (c_len, c_len), 1)
    eye = (ri == ci).astype(F32)
    in_refs = ((rf, kf, vf, lf), (rb_, kb, vb_, lb))
    masks = []
    for d in range(2):
        before = (ri > ci) if d == 0 else (ri < ci)
        strict = before.astype(F32)
        incl = (before | (ri == ci)).astype(F32)
        m8 = jnp.where(((ri // 8) == (ci // 8)) & before, 1.0, 0.0).astype(F32)
        merges = []
        for size in (16, 32, 64):
            same = (ri // size) == (ci // size)
            inner = (ri // (size // 2)) == (ci // (size // 2))
            merges.append(jnp.where(same & jnp.logical_not(inner) & before, 1.0, 0.0).astype(F32))
        masks.append(dict(strict=strict, incl=incl, incl_b=incl.astype(BF16), m8=m8, merges=merges))
    out_refs = (of_ref, ob_ref)
    decay_scale = math.exp(-0.5)

    def chunk(c, carry):
        per_dir = []
        for d in range(2):
            cc = c if d == 0 else n_chunks - 1 - c
            sl = pl.ds(pl.multiple_of(cc * c_len, c_len), c_len)
            r_ref, k_ref, v_ref, lo_ref = in_refs[d]
            r = r_ref[sl, :].astype(F32)
            k = k_ref[sl, :].astype(F32)
            v = v_ref[sl, :].astype(F32)
            lo = lo_ref[sl, :].astype(F32)
            wlo = lo[:, d * RWKV_W_LORA:(d + 1) * RWKV_W_LORA]
            alo = lo[:, 2 * RWKV_W_LORA + d * RWKV_A_LORA:2 * RWKV_W_LORA + (d + 1) * RWKV_A_LORA]
            w_raw = w0_ref[d] + _bdot(jnp.tanh(wlo), w2_ref[d])
            lw = -decay_scale * jax.nn.sigmoid(w_raw)
            a = jax.nn.sigmoid(a0_ref[d] + _bdot(alo, a2_ref[d]))
            k_d = k * (1.0 + (a - 1.0) * ka_ref[...])
            kk = kk_ref[d, sl, :]
            hi = lw.astype(BF16)
            lo2 = (lw - hi.astype(F32)).astype(BF16)
            tri = masks[d]['incl_b']
            cum = (jnp.dot(tri, hi, preferred_element_type=F32)
                   + jnp.dot(tri, lo2, preferred_element_type=F32))
            eg = jnp.exp(cum)
            ieg = jnp.exp(-cum)
            last = c_len - 1 if d == 0 else 0
            per_dir.append(dict(
                rt=(r * eg).astype(BF16), kt=(k_d * ieg).astype(BF16), at=(-(kk * a) * ieg).astype(BF16),
                bt=(kk * jnp.exp(cum - lw)).astype(BF16), vb=v.astype(BF16), g_last=eg[last:last + 1, :],
                rows=sl))
        chains = [(d, h) for d in range(2) for h in range(RWKV_H)]
        hsl = lambda h: slice(h * RWKV_HS, (h + 1) * RWKV_HS)
        rb = [jnp.concatenate([per_dir[d]['rt'][:, hsl(h)], per_dir[d]['bt'][:, hsl(h)]], axis=0)
              for d, h in chains]
        at_h = [per_dir[d]['at'][:, hsl(h)] for d, h in chains]
        kt_h = [per_dir[d]['kt'][:, hsl(h)] for d, h in chains]
        v_h = [per_dir[d]['vb'][:, hsl(h)] for d, h in chains]
        mk = [masks[d] for d, h in chains]
        n = len(chains)
        m_a = [lax.dot_general(rb[i], at_h[i], _NT, preferred_element_type=F32) for i in range(n)]
        m_k = [lax.dot_general(rb[i], kt_h[i], _NT, preferred_element_type=F32) for i in range(n)]
        ra = [m_a[i][:c_len] * mk[i]['incl'] for i in range(n)]
        nmat = [m_a[i][c_len:] * mk[i]['strict'] for i in range(n)]
        rk = [m_k[i][:c_len] * mk[i]['incl'] for i in range(n)]
        bk = [m_k[i][c_len:] * mk[i]['strict'] for i in range(n)]
        d1 = [nmat[i] * mk[i]['m8'] for i in range(n)]
        d2 = [_bdot(d1[i], d1[i]) for i in range(n)]
        d4 = [_bdot(d2[i], d2[i]) for i in range(n)]
        tinv = [eye + d1[i] for i in range(n)]
        tinv = [tinv[i] + _bdot(tinv[i], d2[i]) for i in range(n)]
        tinv = [tinv[i] + _bdot(tinv[i], d4[i]) for i in range(n)]
        for lvl in range(3):
            e = [_bdot(tinv[i], nmat[i] * mk[i]['merges'][lvl]) for i in range(n)]
            tinv = [tinv[i] + _bdot(e[i], tinv[i]) for i in range(n)]
        s0 = [s_ref[d, h] for d, h in chains]
        rbs = [lax.dot_general(rb[i], s0[i].astype(BF16), _NT, preferred_element_type=F32) for i in range(n)]
        x = [rbs[i][c_len:] + _bdot(bk[i], v_h[i]) for i in range(n)]
        u = [_bdot(tinv[i], x[i]) for i in range(n)]
        y = [rbs[i][:c_len] + _bdot(ra[i], u[i]) + _bdot(rk[i], v_h[i]) for i in range(n)]
        for i, (d, h) in enumerate(chains):
            uv = jnp.concatenate([u[i].astype(BF16), v_h[i]], axis=0)
            ak = jnp.concatenate([at_h[i], kt_h[i]], axis=0)
            ds = lax.dot_general(uv, ak, _TN, preferred_element_type=F32)
            s_ref[d, h] = (s0[i] + ds) * per_dir[d]['g_last'][:, hsl(h)]
        for d in range(2):
            out_refs[d][per_dir[d]['rows'], :] = jnp.concatenate(
                [y[d * RWKV_H + h] for h in range(RWKV_H)], axis=1)
        return carry

    lax.fori_loop(0, n_chunks, chunk, 0)


def _rwkv_mix(p_rwkv, seq, tc, lp):
    n = p_rwkv.shape[0]
    bsz = n // seq
    tb = _pick_tile(math.gcd(seq, tc), RWKV_TB, RWKV_CHUNK)
    nblk, cblk = seq // tb, tc // tb
    assert LORA_OFF % LORA_W == 0

    def fwd_blk(t):
        return t

    def bwd_blk(t):
        return jnp.where(t < cblk, cblk - 1 - t, nblk - 1 - (t - cblk))

    def col_spec(width, cb, blk_fn):
        return pl.BlockSpec((tb, width), lambda b, t: (b * nblk + blk_fn(t), cb))

    in_specs = []
    for blk_fn in (fwd_blk, bwd_blk):
        in_specs += [col_spec(BR_W, 0, blk_fn), col_spec(BR_W, 1, blk_fn), col_spec(BR_W, 2, blk_fn),
                     col_spec(LORA_W, LORA_OFF // LORA_W, blk_fn)]
    whole = lambda a: pl.BlockSpec(a.shape, lambda b, t: (0,) * a.ndim)
    head_id = jnp.arange(BR_W) // RWKV_HS
    bd = (head_id[:, None] == head_id[None, :]).astype(BF16)
    consts = [lp['rwkv_w0'][:, None, :], lp['rwkv_w2'].astype(BF16), lp['rwkv_a0'][:, None, :],
              lp['rwkv_a2'].astype(BF16), lp['rwkv_k_k'][None, :], lp['rwkv_k_a'][None, :], bd]
    in_specs += [whole(a) for a in consts]
    return pl.pallas_call(
        functools.partial(_rwkv_body, n_chunks=tb // RWKV_CHUNK),
        grid=(bsz, nblk),
        in_specs=in_specs,
        out_specs=[col_spec(BR_W, 0, fwd_blk), col_spec(BR_W, 0, bwd_blk)],
        out_shape=[jax.ShapeDtypeStruct((n, BR_W), F32)] * 2,
        scratch_shapes=[pltpu.VMEM((2, RWKV_H, RWKV_HS, RWKV_HS), F32), pltpu.VMEM((2, tb, BR_W), F32)],
        compiler_params=pltpu.CompilerParams(
            dimension_semantics=("parallel", "arbitrary"), vmem_limit_bytes=VMEM_LIMIT),
    )(*([p_rwkv] * 8), *consts)


def _rwkv_out_proj(y_f, y_b, p_rwkv, lp, w_proj, tm):
    head_id = jnp.arange(BR_W) // RWKV_HS
    bd = (head_id[:, None] == head_id[None, :]).astype(BF16)

    def hsum(xv, bdm):
        hi = xv.astype(BF16)
        lo = (xv - hi.astype(F32)).astype(BF16)
        return (jnp.dot(hi, bdm, preferred_element_type=F32) + jnp.dot(lo, bdm, preferred_element_type=F32))

    def act(i, yf, yb, r, k, v, glo, gn_g, gn_b, r_k, g2, bdm):
        inv = 1.0 / RWKV_HS
        r, k, v, glo = [t.astype(F32) for t in (r, k, v, glo)]
        yb = yf + yb
        yc = yb - hsum(yb, bdm) * inv
        var = hsum(yc * yc, bdm) * inv
        yn = yc * lax.rsqrt(var + RWKV_GN_EPS) * gn_g + gn_b
        bonus = hsum(r * k * r_k, bdm) * v
        g = _bdot(jax.nn.sigmoid(glo), g2)
        return (yn + bonus) * g

    glo_cb = (LORA_OFF + LORA_W) // RWKV_G_LORA
    assert (LORA_OFF + LORA_W) % RWKV_G_LORA == 0
    return _fused_mm(
        [y_f, y_b, (p_rwkv, BR_W, 0), (p_rwkv, BR_W, 1), (p_rwkv, BR_W, 2), (p_rwkv, RWKV_G_LORA, glo_cb)],
        [_whole(lp['rwkv_gn_g'][None, :]), _whole(lp['rwkv_gn_b'][None, :]), _whole(lp['rwkv_r_k'][None, :]),
         _whole(lp['rwkv_g2'].astype(BF16)), _whole(bd)],
        w_proj.astype(BF16), act, tm=tm, tn=w_proj.shape[1], out_dtype=BF16)


LRU_TB = 256
LRU_HALO = 16


def _lru_body(x_ref, g_ref, cw_ref, cb_ref, wr_ref, br_ref, wi_ref, bi_ref, sp_ref, o_ref, h_ref, *,
              tb, nblk, cblk):
    seq = x_ref.shape[0]
    row = lax.broadcasted_iota(jnp.int32, (tb, 1), 0)
    n_ext = tb + LRU_HALO
    for d in range(2):
        h_ref[...] = jnp.zeros_like(h_ref)

        def blk_body(i, carry, d=d):
            if d == 0:
                blk = i
                halo_ok = (blk != 0) & (blk != cblk)
                halo0 = jnp.maximum(blk * tb - LRU_HALO, 0)
            else:
                blk = jnp.where(i < cblk, cblk - 1 - i, nblk - 1 - (i - cblk))
                halo_ok = (blk != cblk - 1) & (blk != nblk - 1)
                halo0 = jnp.minimum(blk * tb + tb, seq - LRU_HALO)
            rows = pl.ds(pl.multiple_of(blk * tb, tb), tb)
            cur = x_ref[rows, :].astype(F32)
            halo = (x_ref[pl.ds(pl.multiple_of(halo0, LRU_HALO), LRU_HALO), :].astype(F32)
                    * jnp.where(halo_ok, 1.0, 0.0))
            xc = cb_ref[d]
            if d == 0:
                ext = jnp.concatenate([halo, cur], axis=0)
                for j in range(LRU_CONV):
                    sh = LRU_CONV - 1 - j
                    tap = ext if sh == 0 else pltpu.roll(ext, sh, 0)
                    xc = xc + cw_ref[d, j:j + 1, :] * tap[LRU_HALO:, :]
            else:
                ext = jnp.concatenate([cur, halo], axis=0)
                for j in range(LRU_CONV):
                    tap = ext if j == 0 else pltpu.roll(ext, n_ext - j, 0)
                    xc = xc + cw_ref[d, j:j + 1, :] * tap[:tb, :]
            gate_r = jax.nn.sigmoid(_bdot(xc, wr_ref[d]) + br_ref[d])
            gate_i = jax.nn.sigmoid(_bdot(xc, wi_ref[d]) + bi_ref[d])
            log_a = -LRU_C * gate_r * sp_ref[d]
            a_cum = jnp.exp(log_a)
            b_cum = jnp.sqrt(1.0 - jnp.exp(2.0 * log_a)) * (gate_i * xc)
            s = 1
            while s < tb:
                if d == 0:
                    ok = row >= s
                    a_sh = jnp.where(ok, pltpu.roll(a_cum, s, 0), 1.0)
                    b_sh = jnp.where(ok, pltpu.roll(b_cum, s, 0), 0.0)
                else:
                    ok = row < tb - s
                    a_sh = jnp.where(ok, pltpu.roll(a_cum, tb - s, 0), 1.0)
                    b_sh = jnp.where(ok, pltpu.roll(b_cum, tb - s, 0), 0.0)
                b_cum = a_cum * b_sh + b_cum
                a_cum = a_cum * a_sh
                s *= 2
            h = b_cum + a_cum * h_ref[...]
            if d == 0:
                h_ref[...] = h[tb - 1:tb, :]
                o_ref[rows, :] = h
            else:
                h_ref[...] = h[0:1, :]
                o_ref[rows, :] = (o_ref[rows, :] + h) * jax.nn.gelu(g_ref[rows, :].astype(F32))
            return carry

        lax.fori_loop(0, nblk, blk_body, 0)


def _lru_mix(p_mid, seq, tc, lp):
    n = p_mid.shape[0]
    tb = _pick_tile(math.gcd(seq, tc), LRU_TB)
    nblk, cblk = seq // tb, tc // tb
    eye = jnp.eye(LRU_BLOCKS, dtype=F32)
    blockdiag = lambda w: jnp.einsum('dgij,gh->dgihj', w, eye).reshape(2, BR_W, BR_W).astype(BF16)
    consts = [lp['lru_conv_w'], lp['lru_conv_b'][:, None, :], blockdiag(lp['lru_wr']), lp['lru_br'][:, None, :],
              blockdiag(lp['lru_wi']), lp['lru_bi'][:, None, :], jax.nn.softplus(-lp['lru_lambda'])[:, None, :]]
    whole = lambda a: pl.BlockSpec(a.shape, lambda b: (0,) * a.ndim)
    x_cb = [name for name, _ in REST_COLS].index('lru_x')
    g_cb = [name for name, _ in REST_COLS].index('lru_gate')
    assert all(w == BR_W for _, w in REST_COLS[:max(x_cb, g_cb) + 1])
    return pl.pallas_call(
        functools.partial(_lru_body, tb=tb, nblk=nblk, cblk=cblk),
        grid=(n // seq,),
        in_specs=[pl.BlockSpec((seq, BR_W), lambda b: (b, x_cb)), pl.BlockSpec((seq, BR_W), lambda b: (b, g_cb))]
        + [whole(a) for a in consts],
        out_specs=pl.BlockSpec((seq, BR_W), lambda b: (b, 0)),
        out_shape=jax.ShapeDtypeStruct((n, BR_W), F32),
        scratch_shapes=[pltpu.VMEM((1, BR_W), F32)],
        compiler_params=pltpu.CompilerParams(dimension_semantics=("parallel",), vmem_limit_bytes=VMEM_LIMIT),
    )(p_mid, p_mid, *consts)


CONV_TB = 256
CONV_HALO = 16


def _conv_body(val_ref, gate_ref, w_ref, b_ref, lng_ref, lnb_ref, o_ref, *, tb, nblk, cblk):
    seq = val_ref.shape[0]
    n_ext = tb + 2 * CONV_HALO
    pad = CONV_K // 2

    def glu(rows):
        return val_ref[rows, :].astype(F32) * jax.nn.sigmoid(gate_ref[rows, :].astype(F32))

    def blk_body(blk, carry):
        t0 = blk * tb
        rows = pl.ds(pl.multiple_of(t0, tb), tb)
        lo_ok = (blk != 0) & (blk != cblk)
        hi_ok = (blk != cblk - 1) & (blk != nblk - 1)
        lo0 = jnp.maximum(t0 - CONV_HALO, 0)
        hi0 = jnp.minimum(t0 + tb, seq - CONV_HALO)
        lo = glu(pl.ds(pl.multiple_of(lo0, CONV_HALO), CONV_HALO)) * jnp.where(lo_ok, 1.0, 0.0)
        hi = glu(pl.ds(pl.multiple_of(hi0, CONV_HALO), CONV_HALO)) * jnp.where(hi_ok, 1.0, 0.0)
        ext = jnp.concatenate([lo, glu(rows), hi], axis=0)
        acc = jnp.zeros((tb, ext.shape[1]), F32) + b_ref[...]
        for r in range(8):
            rolled = ext if r == 0 else pltpu.roll(ext, n_ext - r, 0)
            for j in range(CONV_K):
                off = CONV_HALO - pad + j
                if off % 8 == r:
                    acc = acc + w_ref[j:j + 1, :] * rolled[off - r:off - r + tb, :]
        mu = jnp.mean(acc, axis=-1, keepdims=True)
        xc = acc - mu
        yn = xc * lax.rsqrt(jnp.mean(xc * xc, axis=-1, keepdims=True) + LN_EPS) * lng_ref[...] + lnb_ref[...]
        o_ref[rows, :] = yn * jax.nn.sigmoid(yn)
        return carry

    lax.fori_loop(0, nblk, blk_body, 0)


def _conv_mix(p_mid, seq, tc, lp):
    n = p_mid.shape[0]
    tb = _pick_tile(math.gcd(seq, tc), CONV_TB, CONV_HALO)
    names = [name for name, _ in REST_COLS]
    v_cb, g_cb = names.index('conv_val'), names.index('conv_gate')
    assert all(w == BR_W for _, w in REST_COLS[:max(v_cb, g_cb) + 1]) and CONV_HALO >= CONV_K // 2
    consts = [lp['conv_w'], lp['conv_b'][None, :], lp['conv_ln_g'][None, :], lp['conv_ln_b'][None, :]]
    whole = lambda a: pl.BlockSpec(a.shape, lambda b: (0,) * a.ndim)
    return pl.pallas_call(
        functools.partial(_conv_body, tb=tb, nblk=seq // tb, cblk=tc // tb),
        grid=(n // seq,),
        in_specs=[pl.BlockSpec((seq, BR_W), lambda b: (b, v_cb)), pl.BlockSpec((seq, BR_W), lambda b: (b, g_cb))]
        + [whole(a) for a in consts],
        out_specs=pl.BlockSpec((seq, BR_W), lambda b: (b, 0)),
        out_shape=jax.ShapeDtypeStruct((n, BR_W), F32),
        compiler_params=pltpu.CompilerParams(dimension_semantics=("parallel",), vmem_limit_bytes=VMEM_LIMIT),
    )(p_mid, p_mid, *consts)


RET_TB = 256
_RET_LOG_G = [math.log1p(-2.0 ** (-5.0 - h)) for h in range(RET_H)]


def _ret_body(qf, kf, vf, cf, sf, qb, kb, vb, cb, sb, dm_ref, xi_ref, zt_ref, of_ref, ob_ref, s_ref, *, n_chunks):
    c_len = RET_CHUNK
    qk_w = RET_H * RET_DK

    @pl.when(pl.program_id(1) == 0)
    def _():
        s_ref[...] = jnp.zeros_like(s_ref)

    lane = lax.broadcasted_iota(jnp.int32, (c_len, qk_w), 1)
    first_half = (lane % RET_DK) < (RET_DK // 2)

    def rotary(z, cos, sin):
        swapped = jnp.where(first_half, pltpu.roll(z, qk_w - RET_DK // 2, 1), pltpu.roll(z, RET_DK // 2, 1))
        return z * cos + swapped * sin

    in_refs = ((qf, kf, vf, cf, sf), (qb, kb, vb, cb, sb))
    out_refs = (of_ref, ob_ref)

    def chunk(c, carry):
        per_dir = []
        for d in range(2):
            cc = c if d == 0 else n_chunks - 1 - c
            sl = pl.ds(pl.multiple_of(cc * c_len, c_len), c_len)
            q_ref, k_ref, v_ref, c_ref, sn_ref = in_refs[d]
            cos, sin = c_ref[sl, :], sn_ref[sl, :]
            q = rotary(q_ref[sl, :].astype(F32), cos, sin)
            k = rotary(k_ref[sl, :].astype(F32), cos, sin) * (RET_DK ** -0.5)
            per_dir.append(dict(q=q.astype(BF16), k=k.astype(BF16), kz=(k * zt_ref[d]).astype(BF16),
                                v=v_ref[sl, :].astype(BF16), sl=sl))
        groups = [(d, h) for d in range(2) for h in range(RET_H)]
        ksl = lambda h: slice(h * RET_DK, (h + 1) * RET_DK)
        vsl = lambda h: slice(h * RET_DV, (h + 1) * RET_DV)
        q_h = [per_dir[d]['q'][:, ksl(h)] for d, h in groups]
        v_h = [per_dir[d]['v'][:, vsl(h)] for d, h in groups]
        att = [lax.dot_general(q_h[i], per_dir[d]['k'][:, ksl(h)], _NT, preferred_element_type=F32) * dm_ref[d, h]
               for i, (d, h) in enumerate(groups)]
        s0 = [s_ref[d, h] for d, h in groups]
        o = [_bdot(att[i], v_h[i]) + _bdot(q_h[i], s0[i]) * xi_ref[d][:, vsl(h)]
             for i, (d, h) in enumerate(groups)]
        for i, (d, h) in enumerate(groups):
            kv = lax.dot_general(per_dir[d]['kz'][:, ksl(h)], v_h[i], _TN, preferred_element_type=F32)
            s_ref[d, h] = s0[i] * math.exp(_RET_LOG_G[h] * c_len) + kv
        for d in range(2):
            out_refs[d][per_dir[d]['sl'], :] = jnp.concatenate([o[d * RET_H + h] for h in range(RET_H)], axis=1)
        return carry

    lax.fori_loop(0, n_chunks, chunk, 0)


def _ret_mix(p_mid, seq, tc, n_tok):
    n = p_mid.shape[0]
    tb = _pick_tile(math.gcd(seq, tc), RET_TB, RET_CHUNK)
    nblk, cblk = seq // tb, tc // tb
    c_len = RET_CHUNK
    n_rows = n_tok // GRID_W
    rows = jnp.repeat(jnp.arange(n_rows, dtype=F32), GRID_W)
    cols = jnp.tile(jnp.arange(GRID_W, dtype=F32), n_rows)
    n_freq = RET_DK // 4
    inv = ROPE_BASE ** (-jnp.arange(n_freq, dtype=F32) / n_freq)
    ang = jnp.concatenate([rows[:, None] * inv, cols[:, None] * inv], axis=-1)
    cos_h = jnp.concatenate([jnp.cos(ang), jnp.cos(ang)], axis=-1)
    sin_h = jnp.concatenate([-jnp.sin(ang), jnp.sin(ang)], axis=-1)
    pad_ctx = lambda tbl, fill: jnp.concatenate(
        [jnp.full((tc, RET_H * RET_DK), fill, F32), jnp.tile(tbl, (1, RET_H))], axis=0)
    cos_t, sin_t = pad_ctx(cos_h, 1.0), pad_ctx(sin_h, 0.0)
    log_g = jnp.asarray(_RET_LOG_G, F32)
    idx = jnp.arange(c_len, dtype=F32)
    dm, xi, zt = [], [], []
    for d in range(2):
        pos = idx if d == 0 else c_len - 1.0 - idx
        diff = pos[:, None] - pos[None, :]
        keep = diff >= 0 if d == 0 else diff > 0
        dm.append(jnp.where(keep[None], jnp.exp(log_g[:, None, None] * jnp.maximum(diff, 0.0)[None]), 0.0))
        xi.append(jnp.repeat(jnp.exp(log_g[None, :] * (pos[:, None] + 1.0)), RET_DV, axis=1))
        zt.append(jnp.repeat(jnp.exp(log_g[None, :] * (c_len - 1.0 - pos)[:, None]), RET_DK, axis=1))
    consts = [jnp.stack(dm), jnp.stack(xi), jnp.stack(zt)]

    def fwd_blk(t):
        return t

    def bwd_blk(t):
        return jnp.where(t < cblk, cblk - 1 - t, nblk - 1 - (t - cblk))

    names = [name for name, _ in REST_COLS]
    offs = {name: sum(w for _, w in REST_COLS[:i]) for i, (name, _) in enumerate(REST_COLS)}
    qk_w = RET_H * RET_DK
    in_specs = []
    for blk_fn in (fwd_blk, bwd_blk):
        spec = lambda width, cb, blk_fn=blk_fn: pl.BlockSpec((tb, width), lambda b, t: (b * nblk + blk_fn(t), cb))
        tbl = lambda blk_fn=blk_fn: pl.BlockSpec((tb, qk_w), lambda b, t: (blk_fn(t), 0))
        assert offs['ret_q'] % qk_w == 0 and offs['ret_k'] % qk_w == 0 and offs['ret_v'] % BR_W == 0
        in_specs += [spec(qk_w, offs['ret_q'] // qk_w), spec(qk_w, offs['ret_k'] // qk_w),
                     spec(BR_W, offs['ret_v'] // BR_W), tbl(), tbl()]
    whole = lambda a: pl.BlockSpec(a.shape, lambda b, t: (0,) * a.ndim)
    in_specs += [whole(a) for a in consts]
    out_spec = lambda blk_fn: pl.BlockSpec((tb, BR_W), lambda b, t: (b * nblk + blk_fn(t), 0))
    return pl.pallas_call(
        functools.partial(_ret_body, n_chunks=tb // c_len),
        grid=(n // seq, nblk),
        in_specs=in_specs,
        out_specs=[out_spec(fwd_blk), out_spec(bwd_blk)],
        out_shape=[jax.ShapeDtypeStruct((n, BR_W), F32)] * 2,
        scratch_shapes=[pltpu.VMEM((2, RET_H, RET_DK, RET_DV), F32)],
        compiler_params=pltpu.CompilerParams(
            dimension_semantics=("parallel", "arbitrary"), vmem_limit_bytes=VMEM_LIMIT),
    )(*([p_mid, p_mid, p_mid, cos_t, sin_t] * 2), *consts)


def _ret_out_proj(o_f, o_b, p_mid, lp, w_proj, tm):
    def act(i, of, ob, g, gn_g, gn_b):
        o = of + ob
        g = g.astype(F32)
        normed = []
        for h in range(RET_H):
            oh = o[:, h * RET_DV:(h + 1) * RET_DV]
            oc = oh - jnp.mean(oh, axis=-1, keepdims=True)
            normed.append(oc * lax.rsqrt(jnp.mean(oc * oc, axis=-1, keepdims=True) + LN_EPS))
        yn = jnp.concatenate(normed, axis=1) * gn_g + gn_b
        return g * jax.nn.sigmoid(g) * yn

    off_g = sum(w for _, w in REST_COLS[:[name for name, _ in REST_COLS].index('ret_g')])
    assert off_g % BR_W == 0
    return _fused_mm([o_f, o_b, (p_mid, BR_W, off_g // BR_W)],
                     [_whole(lp['ret_gn_g'][None, :]), _whole(lp['ret_gn_b'][None, :])],
                     w_proj.astype(BF16), act, tm=tm, tn=w_proj.shape[1], out_dtype=BF16)


ROUTER_PAD = 128


def _norm2_router_body(z_ref, g_ref, shl_ref, scl_ref, shc_ref, scc_ref, whi_ref, wlo_ref, tok_ref, log_ref, *,
                       skip, cblk):
    xb = z_ref[...]
    y = xb * lax.rsqrt(jnp.mean(xb * xb, axis=-1, keepdims=True) + RMS_EPS) * g_ref[...]
    is_ctx = (pl.program_id(1) + skip) < cblk
    sc = jnp.where(is_ctx, scc_ref[...], scl_ref[0])
    sh = jnp.where(is_ctx, shc_ref[...], shl_ref[0])
    h = y * (1.0 + sc) + sh
    hi = h.astype(BF16)
    lo = (h - hi.astype(F32)).astype(BF16)
    tok_ref[...] = hi
    log_ref[...] = (jnp.dot(hi, whi_ref[...], preferred_element_type=F32)
                    + jnp.dot(lo, whi_ref[...], preferred_element_type=F32)
                    + jnp.dot(hi, wlo_ref[...], preferred_element_type=F32))


def _norm2_router(z, seq, tc, g, sh_l, sc_l, sh_c, sc_c, w_router, latent_only):
    n, d = z.shape
    bsz = n // seq
    tb = _pick_tile(math.gcd(seq, tc), 256, 16)
    nblk, cblk = seq // tb, tc // tb
    skip = cblk if latent_only else 0
    nb = nblk - skip
    w_pad = jnp.pad(w_router, ((0, 0), (0, ROUTER_PAD - w_router.shape[1])))
    w_hi = w_pad.astype(BF16)
    w_lo = (w_pad - w_hi.astype(F32)).astype(BF16)
    vec = pl.BlockSpec((1, d), lambda b, i: (0, 0))
    per_b = pl.BlockSpec((1, 1, d), lambda b, i: (b, 0, 0))
    wspec = pl.BlockSpec((d, ROUTER_PAD), lambda b, i: (0, 0))
    return pl.pallas_call(
        functools.partial(_norm2_router_body, skip=skip, cblk=cblk),
        grid=(bsz, nb),
        in_specs=[pl.BlockSpec((tb, d), lambda b, i: (b * nblk + skip + i, 0)), vec, per_b, per_b, vec, vec,
                  wspec, wspec],
        out_specs=[pl.BlockSpec((tb, d), lambda b, i: (b * nb + i, 0)),
                   pl.BlockSpec((tb, ROUTER_PAD), lambda b, i: (b * nb + i, 0))],
        out_shape=[jax.ShapeDtypeStruct((bsz * nb * tb, d), BF16),
                   jax.ShapeDtypeStruct((bsz * nb * tb, ROUTER_PAD), F32)],
        compiler_params=pltpu.CompilerParams(dimension_semantics=("parallel", "parallel"),
                                             vmem_limit_bytes=VMEM_LIMIT),
    )(z, g, sh_l, sc_l, sh_c, sc_c, w_hi, w_lo)


def _moe_body(be_ref, nu_ref, x_ref, w1_ref, w3_ref, w2_ref, o_ref, w1b, w3b, w2b):
    i = pl.program_id(0)

    @pl.when((i == 0) | (be_ref[i] != be_ref[jnp.maximum(i - 1, 0)]))
    def _():
        w1b[...] = w1_ref[0].astype(BF16)
        w3b[...] = w3_ref[0].astype(BF16)
        w2b[...] = w2_ref[0].astype(BF16)

    @pl.when(i < nu_ref[0])
    def _():
        xb = x_ref[...]
        h1 = jnp.dot(xb, w1b[...], preferred_element_type=F32)
        h3 = jnp.dot(xb, w3b[...], preferred_element_type=F32)
        hid = (h1 * jax.nn.sigmoid(h1) * h3).astype(BF16)
        o_ref[...] = jnp.dot(hid, w2b[...], preferred_element_type=F32)

    @pl.when(i >= nu_ref[0])
    def _():
        o_ref[...] = jnp.zeros_like(o_ref)


def _moe_experts(xb, block_e, n_used, w1, w3, w2, layer):
    n_rows, d = xb.shape
    n_blocks = n_rows // MOE_BM
    de = w1.shape[3]
    grid_spec = pltpu.PrefetchScalarGridSpec(
        num_scalar_prefetch=2,
        grid=(n_blocks,),
        in_specs=[
            pl.BlockSpec((MOE_BM, d), lambda i, be, nu: (i, 0)),
            pl.BlockSpec((None, 1, d, de), lambda i, be, nu: (layer, be[i], 0, 0)),
            pl.BlockSpec((None, 1, d, de), lambda i, be, nu: (layer, be[i], 0, 0)),
            pl.BlockSpec((None, 1, de, d), lambda i, be, nu: (layer, be[i], 0, 0)),
        ],
        out_specs=pl.BlockSpec((MOE_BM, d), lambda i, be, nu: (i, 0)),
        scratch_shapes=[pltpu.VMEM((d, de), BF16), pltpu.VMEM((d, de), BF16), pltpu.VMEM((de, d), BF16)],
    )
    return pl.pallas_call(
        _moe_body,
        grid_spec=grid_spec,
        out_shape=jax.ShapeDtypeStruct((n_rows, d), F32),
        compiler_params=pltpu.CompilerParams(dimension_semantics=("arbitrary",),
                                             vmem_limit_bytes=VMEM_LIMIT),
    )(block_e, n_used, xb, w1, w3, w2)


def _moe(tokens, logits, b_router, w1, w3, w2, layer):
    n_tok = tokens.shape[0]
    aff = jax.nn.sigmoid(logits[:, :N_EXPERTS])
    assert TOP_K == 2
    sel = (aff + b_router).reshape(-1, N_GROUPS, EXPERTS_PER_GROUP)

    def top2(v):
        lane = jnp.arange(v.shape[-1], dtype=jnp.int32)
        i1 = jnp.argmax(v, axis=-1).astype(jnp.int32)
        rest = jnp.where(lane == i1[..., None], -jnp.inf, v)
        i2 = jnp.argmax(rest, axis=-1).astype(jnp.int32)
        return jnp.max(v, axis=-1), jnp.max(rest, axis=-1), i1, i2

    g1, g2, _, _ = top2(sel)
    grp = jnp.argmax(g1 + g2, axis=-1)
    grp_mask = jnp.arange(N_GROUPS)[None, :] == grp[:, None]
    masked = jnp.where(grp_mask[:, :, None], sel, -jnp.inf).reshape(-1, N_EXPERTS)
    _, _, e1, e2 = top2(masked)
    e_idx = jnp.stack([e1, e2], axis=-1)
    wts = jnp.take_along_axis(aff, e_idx, axis=1)
    wts = wts / jnp.sum(wts, axis=-1, keepdims=True)

    n_asg = n_tok * TOP_K
    flat_e = e_idx.reshape(-1).astype(jnp.int32)
    order = jnp.argsort(flat_e).astype(jnp.int32)
    se, st = flat_e[order], order // TOP_K
    counts = jnp.sum(flat_e[:, None] == jnp.arange(N_EXPERTS, dtype=jnp.int32)[None, :], axis=0, dtype=jnp.int32)
    starts = jnp.cumsum(counts) - counts
    padded = (counts + MOE_BM - 1) // MOE_BM * MOE_BM
    pad_end = jnp.cumsum(padded)
    pad_start = pad_end - padded
    dest = pad_start[se] + jnp.arange(n_asg, dtype=jnp.int32) - starts[se]
    dest_asg = dest[jnp.argsort(order)].reshape(n_tok, TOP_K)
    n_blocks = -(-n_asg // MOE_BM) + N_EXPERTS
    block_start = jnp.arange(n_blocks, dtype=jnp.int32) * MOE_BM
    block_e = jnp.minimum(jnp.sum(pad_end[None, :] <= block_start[:, None], axis=1, dtype=jnp.int32),
                          N_EXPERTS - 1)
    n_used = (pad_end[-1] // MOE_BM).astype(jnp.int32).reshape(1)
    slot_e = jnp.repeat(block_e, MOE_BM)
    rank = jnp.arange(n_blocks * MOE_BM, dtype=jnp.int32) - pad_start[slot_e]
    slot_tok = st[jnp.clip(starts[slot_e] + rank, 0, n_asg - 1)]
    xb = tokens[slot_tok]
    yb = _moe_experts(xb, block_e, n_used, w1, w3, w2, layer)
    return wts[:, 0:1] * yb[dest_asg[:, 0]] + wts[:, 1:2] * yb[dest_asg[:, 1]]


def _merge_out(z, merge, branches, b_merge, w_out_l, g1, cg1, seq, tc, d):
    tm2 = _pick_tile(seq, 576, 16)
    bpb = seq // tm2

    def seg_select(i, ctx_val, lat_val):
        row = (i % bpb) * tm2 + lax.broadcasted_iota(jnp.int32, (tm2, 1), 0)
        return jnp.where(row < tc, ctx_val, lat_val)

    def merge_pro(i, mg, b0, b1, b2, b3, bm):
        gates = jax.nn.sigmoid(mg.astype(F32) + bm)
        b0, b1, b2, b3 = [b.astype(F32) for b in (b0, b1, b2, b3)]
        return (gates[:, 0 * d:1 * d] * b0 + gates[:, 1 * d:2 * d] * b1
                + gates[:, 2 * d:3 * d] * b2 + gates[:, 3 * d:4 * d] * b3)

    def resid_epi(i, acc, zb, gl, gc):
        return zb + seg_select(i, gc, gl[0]) * acc

    tn = d // 2
    return _fused_mm([merge] + branches, [_whole(b_merge[None, :])],
                     w_out_l.astype(BF16), merge_pro, tm=tm2, tn=tn,
                     epilogue=resid_epi, epi_rows=[z],
                     epi_aux=[(g1, (1, 1, tn), lambda i, j: (i // bpb, 0, j)),
                              (cg1, (1, tn), lambda i, j: (0, j))])


def kernel(x, c, ctx, c_ctx, w_ada, b_ada, norm1_g, norm2_g, w_in, b_merge, rwkv_mu_prev, rwkv_mu_next, rwkv_w0, rwkv_w2, rwkv_a0, rwkv_a2, rwkv_g2, rwkv_k_k, rwkv_k_a, rwkv_r_k, rwkv_gn_g, rwkv_gn_b, rwkv_proj, conv_w, conv_b, conv_ln_g, conv_ln_b, conv_proj, lru_conv_w, lru_conv_b, lru_wr, lru_br, lru_wi, lru_bi, lru_lambda, lru_proj, ret_gn_g, ret_gn_b, ret_proj, w_out, w_router, b_router, e_w1, e_w3, e_w2, final_g):
    bsz, n_tok, d = x.shape
    tc = ctx.shape[1]
    seq = tc + n_tok
    depth = w_in.shape[0]
    per_layer = {
        'rwkv_w0': rwkv_w0, 'rwkv_w2': rwkv_w2, 'rwkv_a0': rwkv_a0, 'rwkv_a2': rwkv_a2, 'rwkv_g2': rwkv_g2,
        'rwkv_k_k': rwkv_k_k, 'rwkv_k_a': rwkv_k_a, 'rwkv_r_k': rwkv_r_k,
        'rwkv_gn_g': rwkv_gn_g, 'rwkv_gn_b': rwkv_gn_b,
        'conv_w': conv_w, 'conv_b': conv_b, 'conv_ln_g': conv_ln_g, 'conv_ln_b': conv_ln_b,
        'lru_conv_w': lru_conv_w, 'lru_conv_b': lru_conv_b, 'lru_wr': lru_wr, 'lru_br': lru_br,
        'lru_wi': lru_wi, 'lru_bi': lru_bi, 'lru_lambda': lru_lambda,
        'ret_gn_g': ret_gn_g, 'ret_gn_b': ret_gn_b,
    }
    tm = _pick_tile(seq, 576)

    z = jnp.concatenate([ctx, x], axis=1).reshape(bsz * seq, d)
    for layer in range(depth):
        lp = {name: arr[layer] for name, arr in per_layer.items()}
        last = layer == depth - 1
        cc = jnp.concatenate([c, c_ctx[None, :]], axis=0)
        n_mod = -(-cc.shape[0] // 8) * 8
        cc = jnp.pad(cc, ((0, n_mod - cc.shape[0]), (0, 0)))
        mod = _fused_mm([cc], [], w_ada[layer].astype(BF16), lambda i, cb: cb * jax.nn.sigmoid(cb),
                        tm=n_mod, tn=6 * d // 4,
                        epilogue=lambda i, acc, bb: acc + bb,
                        epi_aux=[(b_ada[layer][None, :], (1, 6 * d // 4), lambda i, j: (0, j))])
        mod_l = mod[:bsz].reshape(bsz, 1, 6 * d)
        mod_c = mod[bsz:bsz + 1]
        sh1, sc1, g1, sh2, sc2, g2 = [mod_l[:, :, n * d:(n + 1) * d] for n in range(6)]
        csh1, csc1, cg1, csh2, csc2, cg2 = [mod_c[:, n * d:(n + 1) * d] for n in range(6)]

        w_l = w_in[layer].astype(BF16)
        norm_args = (norm1_g[layer][None, :], sh1, sc1, csh1, csc1)
        p_rwkv = _in_proj(z, seq, tc, *norm_args, w_l[:, :RWKV_WIDTH], 640,
                          rwkv_mu_prev[layer][None, :], rwkv_mu_next[layer][None, :], out_dtype=BF16)
        p_mid = _in_proj(z, seq, tc, *norm_args, w_l[:, RWKV_WIDTH:RWKV_WIDTH + MID_WIDTH], 512, out_dtype=BF16)
        p_merge = _in_proj(z, seq, tc, *norm_args, w_l[:, RWKV_WIDTH + MID_WIDTH:], 512, out_dtype=BF16)

        y_f, y_b = _rwkv_mix(p_rwkv, seq, tc, lp)
        o_f, o_b = _ret_mix(p_mid, seq, tc, n_tok)
        branches = [
            _rwkv_out_proj(y_f, y_b, p_rwkv, lp, rwkv_proj[layer], tm),
            _plain_mm(_conv_mix(p_mid, seq, tc, lp), conv_proj[layer].astype(BF16), tm=tm, tn=d, out_dtype=BF16),
            _plain_mm(_lru_mix(p_mid, seq, tc, lp), lru_proj[layer].astype(BF16), tm=tm, tn=d, out_dtype=BF16),
            _ret_out_proj(o_f, o_b, p_mid, lp, ret_proj[layer], tm),
        ]
        z = _merge_out(z, p_merge, branches, b_merge[layer], w_out[layer], g1, cg1, seq, tc, d)

        z3 = z.reshape(bsz, seq, d)
        tok, logits = _norm2_router(z, seq, tc, norm2_g[layer][None, :], sh2, sc2, csh2, csc2, w_router,
                                    latent_only=last)
        y = _moe(tok, logits, b_router, e_w1, e_w3, e_w2, layer)
        if not last:
            is_ctx = (jnp.arange(seq) < tc)[None, :, None]
            gate2 = jnp.where(is_ctx, cg2[None], g2)
            z = (z3 + gate2 * y.reshape(bsz, seq, d)).reshape(bsz * seq, d)
        else:
            xl = z3[:, tc:] + g2 * y.reshape(bsz, n_tok, d)
            return xl * lax.rsqrt(jnp.mean(xl * xl, axis=-1, keepdims=True) + RMS_EPS) * final_g
    return None
```

```python
import functools
import math

import jax
import jax.numpy as jnp
from jax import lax
from jax.experimental import pallas as pl
from jax.experimental.pallas import tpu as pltpu

F32 = jnp.float32
BF16 = jnp.bfloat16

D_MODEL = 1024
GRID_W = 64
N_BRANCH = 4
BR_W = D_MODEL // 2
RWKV_HS = 64
RWKV_H = BR_W // RWKV_HS
RWKV_W_LORA = 64
RWKV_A_LORA = 64
RWKV_G_LORA = 128
RWKV_GN_EPS = 64e-5
L2_EPS = 1e-12
CONV_K = 31
LRU_BLOCKS = 8
LRU_BS = BR_W // LRU_BLOCKS
LRU_CONV = 4
LRU_C = 8.0
RET_H = 4
RET_DK = 64
RET_DV = BR_W // RET_H
RET_CHUNK = 128
ROPE_BASE = 10000.0
N_EXPERTS = 32
N_GROUPS = 4
EXPERTS_PER_GROUP = N_EXPERTS // N_GROUPS
TOP_K = 2
D_EXPERT = D_MODEL // 2
RMS_EPS = 1e-6
LN_EPS = 1e-5

REST_COLS = (('conv_val', BR_W), ('conv_gate', BR_W), ('lru_x', BR_W), ('lru_gate', BR_W),
             ('ret_q', RET_H * RET_DK), ('ret_k', RET_H * RET_DK), ('ret_v', BR_W), ('ret_g', BR_W))
RWKV_WIDTH = 3 * BR_W + 2 * RWKV_W_LORA + 2 * RWKV_A_LORA + RWKV_G_LORA
LORA_OFF = 3 * BR_W
LORA_W = 2 * RWKV_W_LORA + 2 * RWKV_A_LORA
MID_WIDTH = sum(w for _, w in REST_COLS)
MERGE_WIDTH = N_BRANCH * D_MODEL

VMEM_LIMIT = 48 * 1024 * 1024
RWKV_CHUNK = 64
RWKV_TB = 256
MOE_BM = 256


def _bdot(a, b):
    return jnp.dot(a.astype(BF16), b.astype(BF16), preferred_element_type=F32)


_NT = (((1,), (1,)), ((), ()))
_TN = (((0,), (0,)), ((), ()))


def _pick_tile(n, cap, mult=8):
    best = None
    for t in range(mult, cap + 1, mult):
        if n % t == 0:
            best = t
    assert best is not None, (n, cap, mult)
    return best


def _fused_mm_body(*refs, nr, na, ner, nea, prologue, epilogue, out_dtype):
    rows = refs[:nr]
    auxs = refs[nr:nr + na]
    w_ref = refs[nr + na]
    er = refs[nr + na + 1:nr + na + 1 + ner]
    ea = refs[nr + na + 1 + ner:nr + na + 1 + ner + nea]
    o_ref, act = refs[-2], refs[-1]
    i = pl.program_id(0)

    @pl.when(pl.program_id(1) == 0)
    def _():
        act[...] = prologue(i, *[r[...] for r in rows], *[a[...] for a in auxs]).astype(BF16)

    acc = jnp.dot(act[...], w_ref[...], preferred_element_type=F32)
    if epilogue is not None:
        acc = epilogue(i, acc, *[r[...] for r in er], *[a[...] for a in ea])
    o_ref[...] = acc.astype(out_dtype)


def _fused_mm(row_ins, aux_ins, w, prologue, *, tm, tn, out_dtype=F32, epilogue=None,
              epi_rows=(), epi_aux=()):
    row_ins = [r if isinstance(r, tuple) else (r, r.shape[1], 0) for r in row_ins]
    n = row_ins[0][0].shape[0]
    k, m = w.shape
    assert n % tm == 0 and m % tn == 0, (n, tm, m, tn)
    in_specs = [pl.BlockSpec((tm, bw), functools.partial(lambda i, j, cb: (i, cb), cb=cb))
                for (_, bw, cb) in row_ins]
    in_specs += [pl.BlockSpec(bs, im) for (_, bs, im) in aux_ins]
    in_specs += [pl.BlockSpec((k, tn), lambda i, j: (0, j))]
    in_specs += [pl.BlockSpec((tm, tn), lambda i, j: (i, j)) for _ in epi_rows]
    in_specs += [pl.BlockSpec(bs, im) for (_, bs, im) in epi_aux]
    body = functools.partial(_fused_mm_body, nr=len(row_ins), na=len(aux_ins), ner=len(epi_rows),
                             nea=len(epi_aux), prologue=prologue, epilogue=epilogue, out_dtype=out_dtype)
    return pl.pallas_call(
        body,
        grid=(n // tm, m // tn),
        in_specs=in_specs,
        out_specs=pl.BlockSpec((tm, tn), lambda i, j: (i, j)),
        out_shape=jax.ShapeDtypeStruct((n, m), out_dtype),
        scratch_shapes=[pltpu.VMEM((tm, k), BF16)],
        compiler_params=pltpu.CompilerParams(dimension_semantics=("parallel", "arbitrary"),
                                             vmem_limit_bytes=VMEM_LIMIT),
    )(*[a for (a, _, _) in row_ins], *[a for (a, _, _) in aux_ins], w, *epi_rows,
      *[a for (a, _, _) in epi_aux])


def _whole(a):
    nd = a.ndim
    return (a, a.shape, lambda i, j: (0,) * nd)


def _plain_mm(x, w, *, tm, tn, out_dtype=F32):
    return _fused_mm([x], [], w, lambda i, xb: xb, tm=tm, tn=tn, out_dtype=out_dtype)


def _in_proj_body(*refs, tc, shift):
    if shift:
        z_ref, g_ref, shl_ref, scl_ref, shc_ref, scc_ref, w_ref, mup_ref, mun_ref, o_ref, act = refs
    else:
        z_ref, g_ref, shl_ref, scl_ref, shc_ref, scc_ref, w_ref, o_ref, act = refs
    seq = z_ref.shape[0]
    row = lax.broadcasted_iota(jnp.int32, (seq, 1), 0)

    @pl.when(pl.program_id(1) == 0)
    def _():
        xb = z_ref[...]
        y = xb * lax.rsqrt(jnp.mean(xb * xb, axis=-1, keepdims=True) + RMS_EPS) * g_ref[...]
        is_ctx = row < tc
        sc = jnp.where(is_ctx, scc_ref[...], scl_ref[0])
        sh = jnp.where(is_ctx, shc_ref[...], shl_ref[0])
        act[...] = (y * (1.0 + sc) + sh).astype(BF16)

    acc = jnp.dot(act[...], w_ref[...], preferred_element_type=F32)
    if shift:
        prev = pltpu.roll(acc, 1, 0)
        nxt = pltpu.roll(acc, seq - 1, 0)
        prev = jnp.where((row == 0) | (row == tc), 0.0, prev)
        nxt = jnp.where((row == tc - 1) | (row == seq - 1), 0.0, nxt)
        acc = acc + mup_ref[...] * (prev - acc) + mun_ref[...] * (nxt - acc)
    o_ref[...] = acc.astype(o_ref.dtype)


def _in_proj(z, seq, tc, g, sh_l, sc_l, sh_c, sc_c, w, tn, mu_prev=None, mu_next=None, out_dtype=F32):
    n, d = z.shape
    m = w.shape[1]
    shift = mu_prev is not None
    assert m % tn == 0
    vec = lambda width: pl.BlockSpec((1, width), lambda b, j: (0, 0))
    per_b = pl.BlockSpec((1, 1, d), lambda b, j: (b, 0, 0))
    in_specs = [pl.BlockSpec((seq, d), lambda b, j: (b, 0)), vec(d), per_b, per_b, vec(d), vec(d),
                pl.BlockSpec((d, tn), lambda b, j: (0, j))]
    args = [z, g, sh_l, sc_l, sh_c, sc_c, w]
    if shift:
        in_specs += [pl.BlockSpec((1, tn), lambda b, j: (0, j))] * 2
        args += [mu_prev, mu_next]
    return pl.pallas_call(
        functools.partial(_in_proj_body, tc=tc, shift=shift),
        grid=(n // seq, m // tn),
        in_specs=in_specs,
        out_specs=pl.BlockSpec((seq, tn), lambda b, j: (b, j)),
        out_shape=jax.ShapeDtypeStruct((n, m), out_dtype),
        scratch_shapes=[pltpu.VMEM((seq, d), BF16)],
        compiler_params=pltpu.CompilerParams(dimension_semantics=("parallel", "arbitrary"),
                                             vmem_limit_bytes=VMEM_LIMIT),
    )(*args)


def _rwkv_body(rf, kf, vf, lf, rb_, kb, vb_, lb, w0_ref, w2_ref, a0_ref, a2_ref, kkw_ref, ka_ref, bd_ref,
               of_ref, ob_ref, s_ref, kk_ref, *, n_chunks):
    c_len = RWKV_CHUNK

    @pl.when(pl.program_id(1) == 0)
    def _():
        s_ref[...] = jnp.zeros_like(s_ref)

    for d, k_blk in enumerate((kf, kb)):
        kx = k_blk[...].astype(F32) * kkw_ref[...]
        kk_ref[d] = kx * lax.rsqrt(_bdot(kx * kx, bd_ref[...]) + L2_EPS)

    ri = lax.broadcasted_iota(jnp.int32, (c_len, c_len), 0)
    ci = lax.broadcasted_iota(jnp.int32, (c_len, c_len), 1)
    eye = (ri == ci).astype(F32)
    in_refs = ((rf, kf, vf, lf), (rb_, kb, vb_, lb))
    masks = []
    for d in range(2):
        before = (ri > ci) if d == 0 else (ri < ci)
        strict = before.astype(F32)
        incl = (before | (ri == ci)).astype(F32)
        m8 = jnp.where(((ri // 8) == (ci // 8)) & before, 1.0, 0.0).astype(F32)
        merges = []
        for size in (16, 32, 64):
            same = (ri // size) == (ci // size)
            inner = (ri // (size // 2)) == (ci // (size // 2))
            merges.append(jnp.where(same & jnp.logical_not(inner) & before, 1.0, 0.0).astype(F32))
        masks.append(dict(strict=strict, incl=incl, incl_b=incl.astype(BF16), m8=m8, merges=merges))
    out_refs = (of_ref, ob_ref)
    decay_scale = math.exp(-0.5)

    def chunk(c, carry):
        per_dir = []
        for d in range(2):
            cc = c if d == 0 else n_chunks - 1 - c
            sl = pl.ds(pl.multiple_of(cc * c_len, c_len), c_len)
            r_ref, k_ref, v_ref, lo_ref = in_refs[d]
            r = r_ref[sl, :].astype(F32)
            k = k_ref[sl, :].astype(F32)
            v = v_ref[sl, :].astype(F32)
            lo = lo_ref[sl, :].astype(F32)
            wlo = lo[:, d * RWKV_W_LORA:(d + 1) * RWKV_W_LORA]
            alo = lo[:, 2 * RWKV_W_LORA + d * RWKV_A_LORA:2 * RWKV_W_LORA + (d + 1) * RWKV_A_LORA]
            w_raw = w0_ref[d] + _bdot(jnp.tanh(wlo), w2_ref[d])
            lw = -decay_scale * jax.nn.sigmoid(w_raw)
            a = jax.nn.sigmoid(a0_ref[d] + _bdot(alo, a2_ref[d]))
            k_d = k * (1.0 + (a - 1.0) * ka_ref[...])
            kk = kk_ref[d, sl, :]
            hi = lw.astype(BF16)
            lo2 = (lw - hi.astype(F32)).astype(BF16)
            tri = masks[d]['incl_b']
            cum = (jnp.dot(tri, hi, preferred_element_type=F32)
                   + jnp.dot(tri, lo2, preferred_element_type=F32))
            eg = jnp.exp(cum)
            ieg = jnp.exp(-cum)
            last = c_len - 1 if d == 0 else 0
            per_dir.append(dict(
                rt=(r * eg).astype(BF16), kt=(k_d * ieg).astype(BF16), at=(-(kk * a) * ieg).astype(BF16),
                bt=(kk * jnp.exp(cum - lw)).astype(BF16), vb=v.astype(BF16), g_last=eg[last:last + 1, :],
                rows=sl))
        chains = [(d, h) for d in range(2) for h in range(RWKV_H)]
        hsl = lambda h: slice(h * RWKV_HS, (h + 1) * RWKV_HS)
        rb = [jnp.concatenate([per_dir[d]['rt'][:, hsl(h)], per_dir[d]['bt'][:, hsl(h)]], axis=0)
              for d, h in chains]
        at_h = [per_dir[d]['at'][:, hsl(h)] for d, h in chains]
        kt_h = [per_dir[d]['kt'][:, hsl(h)] for d, h in chains]
        v_h = [per_dir[d]['vb'][:, hsl(h)] for d, h in chains]
        mk = [masks[d] for d, h in chains]
        n = len(chains)
        m_a = [lax.dot_general(rb[i], at_h[i], _NT, preferred_element_type=F32) for i in range(n)]
        m_k = [lax.dot_general(rb[i], kt_h[i], _NT, preferred_element_type=F32) for i in range(n)]
        ra = [m_a[i][:c_len] * mk[i]['incl'] for i in range(n)]
        nmat = [m_a[i][c_len:] * mk[i]['strict'] for i in range(n)]
        rk = [m_k[i][:c_len] * mk[i]['incl'] for i in range(n)]
        bk = [m_k[i][c_len:] * mk[i]['strict'] for i in range(n)]
        d1 = [nmat[i] * mk[i]['m8'] for i in range(n)]
        d2 = [_bdot(d1[i], d1[i]) for i in range(n)]
        d4 = [_bdot(d2[i], d2[i]) for i in range(n)]
        tinv = [eye + d1[i] for i in range(n)]
        tinv = [tinv[i] + _bdot(tinv[i], d2[i]) for i in range(n)]
        tinv = [tinv[i] + _bdot(tinv[i], d4[i]) for i in range(n)]
        for lvl in range(3):
            e = [_bdot(tinv[i], nmat[i] * mk[i]['merges'][lvl]) for i in range(n)]
            tinv = [tinv[i] + _bdot(e[i], tinv[i]) for i in range(n)]
        s0 = [s_ref[d, h] for d, h in chains]
        rbs = [lax.dot_general(rb[i], s0[i].astype(BF16), _NT, preferred_element_type=F32) for i in range(n)]
        x = [rbs[i][c_len:] + _bdot(bk[i], v_h[i]) for i in range(n)]
        u = [_bdot(tinv[i], x[i]) for i in range(n)]
        y = [rbs[i][:c_len] + _bdot(ra[i], u[i]) + _bdot(rk[i], v_h[i]) for i in range(n)]
        for i, (d, h) in enumerate(chains):
            uv = jnp.concatenate([u[i].astype(BF16), v_h[i]], axis=0)
            ak = jnp.concatenate([at_h[i], kt_h[i]], axis=0)
            ds = lax.dot_general(uv, ak, _TN, preferred_element_type=F32)
            s_ref[d, h] = (s0[i] + ds) * per_dir[d]['g_last'][:, hsl(h)]
        for d in range(2):
            out_refs[d][per_dir[d]['rows'], :] = jnp.concatenate(
                [y[d * RWKV_H + h] for h in range(RWKV_H)], axis=1)
        return carry

    lax.fori_loop(0, n_chunks, chunk, 0)


def _rwkv_mix(p_rwkv, seq, tc, lp):
    n = p_rwkv.shape[0]
    bsz = n // seq
    tb = _pick_tile(math.gcd(seq, tc), RWKV_TB, RWKV_CHUNK)
    nblk, cblk = seq // tb, tc // tb
    assert LORA_OFF % LORA_W == 0

    def fwd_blk(t):
        return t

    def bwd_blk(t):
        return jnp.where(t < cblk, cblk - 1 - t, nblk - 1 - (t - cblk))

    def col_spec(width, cb, blk_fn):
        return pl.BlockSpec((tb, width), lambda b, t: (b * nblk + blk_fn(t), cb))

    in_specs = []
    for blk_fn in (fwd_blk, bwd_blk):
        in_specs += [col_spec(BR_W, 0, blk_fn), col_spec(BR_W, 1, blk_fn), col_spec(BR_W, 2, blk_fn),
                     col_spec(LORA_W, LORA_OFF // LORA_W, blk_fn)]
    whole = lambda a: pl.BlockSpec(a.shape, lambda b, t: (0,) * a.ndim)
    head_id = jnp.arange(BR_W) // RWKV_HS
    bd = (head_id[:, None] == head_id[None, :]).astype(BF16)
    consts = [lp['rwkv_w0'][:, None, :], lp['rwkv_w2'].astype(BF16), lp['rwkv_a0'][:, None, :],
              lp['rwkv_a2'].astype(BF16), lp['rwkv_k_k'][None, :], lp['rwkv_k_a'][None, :], bd]
    in_specs += [whole(a) for a in consts]
    return pl.pallas_call(
        functools.partial(_rwkv_body, n_chunks=tb // RWKV_CHUNK),
        grid=(bsz, nblk),
        in_specs=in_specs,
        out_specs=[col_spec(BR_W, 0, fwd_blk), col_spec(BR_W, 0, bwd_blk)],
        out_shape=[jax.ShapeDtypeStruct((n, BR_W), F32)] * 2,
        scratch_shapes=[pltpu.VMEM((2, RWKV_H, RWKV_HS, RWKV_HS), F32), pltpu.VMEM((2, tb, BR_W), F32)],
        compiler_params=pltpu.CompilerParams(
            dimension_semantics=("parallel", "arbitrary"), vmem_limit_bytes=VMEM_LIMIT),
    )(*([p_rwkv] * 8), *consts)


def _rwkv_out_proj(y_f, y_b, p_rwkv, lp, w_proj, tm):
    head_id = jnp.arange(BR_W) // RWKV_HS
    bd = (head_id[:, None] == head_id[None, :]).astype(BF16)

    def hsum(xv, bdm):
        hi = xv.astype(BF16)
        lo = (xv - hi.astype(F32)).astype(BF16)
        return (jnp.dot(hi, bdm, preferred_element_type=F32) + jnp.dot(lo, bdm, preferred_element_type=F32))

    def act(i, yf, yb, r, k, v, glo, gn_g, gn_b, r_k, g2, bdm):
        inv = 1.0 / RWKV_HS
        r, k, v, glo = [t.astype(F32) for t in (r, k, v, glo)]
        yb = yf + yb
        yc = yb - hsum(yb, bdm) * inv
        var = hsum(yc * yc, bdm) * inv
        yn = yc * lax.rsqrt(var + RWKV_GN_EPS) * gn_g + gn_b
        bonus = hsum(r * k * r_k, bdm) * v
        g = _bdot(jax.nn.sigmoid(glo), g2)
        return (yn + bonus) * g

    glo_cb = (LORA_OFF + LORA_W) // RWKV_G_LORA
    assert (LORA_OFF + LORA_W) % RWKV_G_LORA == 0
    return _fused_mm(
        [y_f, y_b, (p_rwkv, BR_W, 0), (p_rwkv, BR_W, 1), (p_rwkv, BR_W, 2), (p_rwkv, RWKV_G_LORA, glo_cb)],
        [_whole(lp['rwkv_gn_g'][None, :]), _whole(lp['rwkv_gn_b'][None, :]), _whole(lp['rwkv_r_k'][None, :]),
         _whole(lp['rwkv_g2'].astype(BF16)), _whole(bd)],
        w_proj.astype(BF16), act, tm=tm, tn=w_proj.shape[1], out_dtype=BF16)


LRU_TB = 256
LRU_HALO = 16


def _lru_body(x_ref, g_ref, cw_ref, cb_ref, wr_ref, br_ref, wi_ref, bi_ref, sp_ref, o_ref, h_ref, *,
              tb, nblk, cblk):
    seq = x_ref.shape[0]
    row = lax.broadcasted_iota(jnp.int32, (tb, 1), 0)
    n_ext = tb + LRU_HALO
    for d in range(2):
        h_ref[...] = jnp.zeros_like(h_ref)

        def blk_body(i, carry, d=d):
            if d == 0:
                blk = i
                halo_ok = (blk != 0) & (blk != cblk)
                halo0 = jnp.maximum(blk * tb - LRU_HALO, 0)
            else:
                blk = jnp.where(i < cblk, cblk - 1 - i, nblk - 1 - (i - cblk))
                halo_ok = (blk != cblk - 1) & (blk != nblk - 1)
                halo0 = jnp.minimum(blk * tb + tb, seq - LRU_HALO)
            rows = pl.ds(pl.multiple_of(blk * tb, tb), tb)
            cur = x_ref[rows, :].astype(F32)
            halo = (x_ref[pl.ds(pl.multiple_of(halo0, LRU_HALO), LRU_HALO), :].astype(F32)
                    * jnp.where(halo_ok, 1.0, 0.0))
            xc = cb_ref[d]
            if d == 0:
                ext = jnp.concatenate([halo, cur], axis=0)
                for j in range(LRU_CONV):
                    sh = LRU_CONV - 1 - j
                    tap = ext if sh == 0 else pltpu.roll(ext, sh, 0)
                    xc = xc + cw_ref[d, j:j + 1, :] * tap[LRU_HALO:, :]
            else:
                ext = jnp.concatenate([cur, halo], axis=0)
                for j in range(LRU_CONV):
                    tap = ext if j == 0 else pltpu.roll(ext, n_ext - j, 0)
                    xc = xc + cw_ref[d, j:j + 1, :] * tap[:tb, :]
            gate_r = jax.nn.sigmoid(_bdot(xc, wr_ref[d]) + br_ref[d])
            gate_i = jax.nn.sigmoid(_bdot(xc, wi_ref[d]) + bi_ref[d])
            log_a = -LRU_C * gate_r * sp_ref[d]
            a_cum = jnp.exp(log_a)
            b_cum = jnp.sqrt(1.0 - jnp.exp(2.0 * log_a)) * (gate_i * xc)
            s = 1
            while s < tb:
                if d == 0:
                    ok = row >= s
                    a_sh = jnp.where(ok, pltpu.roll(a_cum, s, 0), 1.0)
                    b_sh = jnp.where(ok, pltpu.roll(b_cum, s, 0), 0.0)
                else:
                    ok = row < tb - s
                    a_sh = jnp.where(ok, pltpu.roll(a_cum, tb - s, 0), 1.0)
                    b_sh = jnp.where(ok, pltpu.roll(b_cum, tb - s, 0), 0.0)
                b_cum = a_cum * b_sh + b_cum
                a_cum = a_cum * a_sh
                s *= 2
            h = b_cum + a_cum * h_ref[...]
            if d == 0:
                h_ref[...] = h[tb - 1:tb, :]
                o_ref[rows, :] = h
            else:
                h_ref[...] = h[0:1, :]
                o_ref[rows, :] = (o_ref[rows, :] + h) * jax.nn.gelu(g_ref[rows, :].astype(F32))
            return carry

        lax.fori_loop(0, nblk, blk_body, 0)


def _lru_mix(p_mid, seq, tc, lp):
    n = p_mid.shape[0]
    tb = _pick_tile(math.gcd(seq, tc), LRU_TB)
    nblk, cblk = seq // tb, tc // tb
    eye = jnp.eye(LRU_BLOCKS, dtype=F32)
    blockdiag = lambda w: jnp.einsum('dgij,gh->dgihj', w, eye).reshape(2, BR_W, BR_W).astype(BF16)
    consts = [lp['lru_conv_w'], lp['lru_conv_b'][:, None, :], blockdiag(lp['lru_wr']), lp['lru_br'][:, None, :],
              blockdiag(lp['lru_wi']), lp['lru_bi'][:, None, :], jax.nn.softplus(-lp['lru_lambda'])[:, None, :]]
    whole = lambda a: pl.BlockSpec(a.shape, lambda b: (0,) * a.ndim)
    x_cb = [name for name, _ in REST_COLS].index('lru_x')
    g_cb = [name for name, _ in REST_COLS].index('lru_gate')
    assert all(w == BR_W for _, w in REST_COLS[:max(x_cb, g_cb) + 1])
    return pl.pallas_call(
        functools.partial(_lru_body, tb=tb, nblk=nblk, cblk=cblk),
        grid=(n // seq,),
        in_specs=[pl.BlockSpec((seq, BR_W), lambda b: (b, x_cb)), pl.BlockSpec((seq, BR_W), lambda b: (b, g_cb))]
        + [whole(a) for a in consts],
        out_specs=pl.BlockSpec((seq, BR_W), lambda b: (b, 0)),
        out_shape=jax.ShapeDtypeStruct((n, BR_W), F32),
        scratch_shapes=[pltpu.VMEM((1, BR_W), F32)],
        compiler_params=pltpu.CompilerParams(dimension_semantics=("parallel",), vmem_limit_bytes=VMEM_LIMIT),
    )(p_mid, p_mid, *consts)


CONV_TB = 256
CONV_HALO = 16


def _conv_body(val_ref, gate_ref, w_ref, b_ref, lng_ref, lnb_ref, o_ref, *, tb, nblk, cblk):
    seq = val_ref.shape[0]
    n_ext = tb + 2 * CONV_HALO
    pad = CONV_K // 2

    def glu(rows):
        return val_ref[rows, :].astype(F32) * jax.nn.sigmoid(gate_ref[rows, :].astype(F32))

    def blk_body(blk, carry):
        t0 = blk * tb
        rows = pl.ds(pl.multiple_of(t0, tb), tb)
        lo_ok = (blk != 0) & (blk != cblk)
        hi_ok = (blk != cblk - 1) & (blk != nblk - 1)
        lo0 = jnp.maximum(t0 - CONV_HALO, 0)
        hi0 = jnp.minimum(t0 + tb, seq - CONV_HALO)
        lo = glu(pl.ds(pl.multiple_of(lo0, CONV_HALO), CONV_HALO)) * jnp.where(lo_ok, 1.0, 0.0)
        hi = glu(pl.ds(pl.multiple_of(hi0, CONV_HALO), CONV_HALO)) * jnp.where(hi_ok, 1.0, 0.0)
        ext = jnp.concatenate([lo, glu(rows), hi], axis=0)
        acc = jnp.zeros((tb, ext.shape[1]), F32) + b_ref[...]
        for r in range(8):
            rolled = ext if r == 0 else pltpu.roll(ext, n_ext - r, 0)
            for j in range(CONV_K):
                off = CONV_HALO - pad + j
                if off % 8 == r:
                    acc = acc + w_ref[j:j + 1, :] * rolled[off - r:off - r + tb, :]
        mu = jnp.mean(acc, axis=-1, keepdims=True)
        xc = acc - mu
        yn = xc * lax.rsqrt(jnp.mean(xc * xc, axis=-1, keepdims=True) + LN_EPS) * lng_ref[...] + lnb_ref[...]
        o_ref[rows, :] = yn * jax.nn.sigmoid(yn)
        return carry

    lax.fori_loop(0, nblk, blk_body, 0)


def _conv_mix(p_mid, seq, tc, lp):
    n = p_mid.shape[0]
    tb = _pick_tile(math.gcd(seq, tc), CONV_TB, CONV_HALO)
    names = [name for name, _ in REST_COLS]
    v_cb, g_cb = names.index('conv_val'), names.index('conv_gate')
    assert all(w == BR_W for _, w in REST_COLS[:max(v_cb, g_cb) + 1]) and CONV_HALO >= CONV_K // 2
    consts = [lp['conv_w'], lp['conv_b'][None, :], lp['conv_ln_g'][None, :], lp['conv_ln_b'][None, :]]
    whole = lambda a: pl.BlockSpec(a.shape, lambda b: (0,) * a.ndim)
    return pl.pallas_call(
        functools.partial(_conv_body, tb=tb, nblk=seq // tb, cblk=tc // tb),
        grid=(n // seq,),
        in_specs=[pl.BlockSpec((seq, BR_W), lambda b: (b, v_cb)), pl.BlockSpec((seq, BR_W), lambda b: (b, g_cb))]
        + [whole(a) for a in consts],
        out_specs=pl.BlockSpec((seq, BR_W), lambda b: (b, 0)),
        out_shape=jax.ShapeDtypeStruct((n, BR_W), F32),
        compiler_params=pltpu.CompilerParams(dimension_semantics=("parallel",), vmem_limit_bytes=VMEM_LIMIT),
    )(p_mid, p_mid, *consts)


RET_TB = 256
_RET_LOG_G = [math.log1p(-2.0 ** (-5.0 - h)) for h in range(RET_H)]


def _ret_body(qf, kf, vf, cf, sf, qb, kb, vb, cb, sb, dm_ref, xi_ref, zt_ref, of_ref, ob_ref, s_ref, *, n_chunks):
    c_len = RET_CHUNK
    qk_w = RET_H * RET_DK

    @pl.when(pl.program_id(1) == 0)
    def _():
        s_ref[...] = jnp.zeros_like(s_ref)

    lane = lax.broadcasted_iota(jnp.int32, (c_len, qk_w), 1)
    first_half = (lane % RET_DK) < (RET_DK // 2)

    def rotary(z, cos, sin):
        swapped = jnp.where(first_half, pltpu.roll(z, qk_w - RET_DK // 2, 1), pltpu.roll(z, RET_DK // 2, 1))
        return z * cos + swapped * sin

    in_refs = ((qf, kf, vf, cf, sf), (qb, kb, vb, cb, sb))
    out_refs = (of_ref, ob_ref)

    def chunk(c, carry):
        per_dir = []
        for d in range(2):
            cc = c if d == 0 else n_chunks - 1 - c
            sl = pl.ds(pl.multiple_of(cc * c_len, c_len), c_len)
            q_ref, k_ref, v_ref, c_ref, sn_ref = in_refs[d]
            cos, sin = c_ref[sl, :], sn_ref[sl, :]
            q = rotary(q_ref[sl, :].astype(F32), cos, sin)
            k = rotary(k_ref[sl, :].astype(F32), cos, sin) * (RET_DK ** -0.5)
            per_dir.append(dict(q=q.astype(BF16), k=k.astype(BF16), kz=(k * zt_ref[d]).astype(BF16),
                                v=v_ref[sl, :].astype(BF16), sl=sl))
        groups = [(d, h) for d in range(2) for h in range(RET_H)]
        ksl = lambda h: slice(h * RET_DK, (h + 1) * RET_DK)
        vsl = lambda h: slice(h * RET_DV, (h + 1) * RET_DV)
        q_h = [per_dir[d]['q'][:, ksl(h)] for d, h in groups]
        v_h = [per_dir[d]['v'][:, vsl(h)] for d, h in groups]
        att = [lax.dot_general(q_h[i], per_dir[d]['k'][:, ksl(h)], _NT, preferred_element_type=F32) * dm_ref[d, h]
               for i, (d, h) in enumerate(groups)]
        s0 = [s_ref[d, h] for d, h in groups]
        o = [_bdot(att[i], v_h[i]) + _bdot(q_h[i], s0[i]) * xi_ref[d][:, vsl(h)]
             for i, (d, h) in enumerate(groups)]
        for i, (d, h) in enumerate(groups):
            kv = lax.dot_general(per_dir[d]['kz'][:, ksl(h)], v_h[i], _TN, preferred_element_type=F32)
            s_ref[d, h] = s0[i] * math.exp(_RET_LOG_G[h] * c_len) + kv
        for d in range(2):
            out_refs[d][per_dir[d]['sl'], :] = jnp.concatenate([o[d * RET_H + h] for h in range(RET_H)], axis=1)
        return carry

    lax.fori_loop(0, n_chunks, chunk, 0)


def _ret_mix(p_mid, seq, tc, n_tok):
    n = p_mid.shape[0]
    tb = _pick_tile(math.gcd(seq, tc), RET_TB, RET_CHUNK)
    nblk, cblk = seq // tb, tc // tb
    c_len = RET_CHUNK
    n_rows = n_tok // GRID_W
    rows = jnp.repeat(jnp.arange(n_rows, dtype=F32), GRID_W)
    cols = jnp.tile(jnp.arange(GRID_W, dtype=F32), n_rows)
    n_freq = RET_DK // 4
    inv = ROPE_BASE ** (-jnp.arange(n_freq, dtype=F32) / n_freq)
    ang = jnp.concatenate([rows[:, None] * inv, cols[:, None] * inv], axis=-1)
    cos_h = jnp.concatenate([jnp.cos(ang), jnp.cos(ang)], axis=-1)
    sin_h = jnp.concatenate([-jnp.sin(ang), jnp.sin(ang)], axis=-1)
    pad_ctx = lambda tbl, fill: jnp.concatenate(
        [jnp.full((tc, RET_H * RET_DK), fill, F32), jnp.tile(tbl, (1, RET_H))], axis=0)
    cos_t, sin_t = pad_ctx(cos_h, 1.0), pad_ctx(sin_h, 0.0)
    log_g = jnp.asarray(_RET_LOG_G, F32)
    idx = jnp.arange(c_len, dtype=F32)
    dm, xi, zt = [], [], []
    for d in range(2):
        pos = idx if d == 0 else c_len - 1.0 - idx
        diff = pos[:, None] - pos[None, :]
        keep = diff >= 0 if d == 0 else diff > 0
        dm.append(jnp.where(keep[None], jnp.exp(log_g[:, None, None] * jnp.maximum(diff, 0.0)[None]), 0.0))
        xi.append(jnp.repeat(jnp.exp(log_g[None, :] * (pos[:, None] + 1.0)), RET_DV, axis=1))
        zt.append(jnp.repeat(jnp.exp(log_g[None, :] * (c_len - 1.0 - pos)[:, None]), RET_DK, axis=1))
    consts = [jnp.stack(dm), jnp.stack(xi), jnp.stack(zt)]

    def fwd_blk(t):
        return t

    def bwd_blk(t):
        return jnp.where(t < cblk, cblk - 1 - t, nblk - 1 - (t - cblk))

    names = [name for name, _ in REST_COLS]
    offs = {name: sum(w for _, w in REST_COLS[:i]) for i, (name, _) in enumerate(REST_COLS)}
    qk_w = RET_H * RET_DK
    in_specs = []
    for blk_fn in (fwd_blk, bwd_blk):
        spec = lambda width, cb, blk_fn=blk_fn: pl.BlockSpec((tb, width), lambda b, t: (b * nblk + blk_fn(t), cb))
        tbl = lambda blk_fn=blk_fn: pl.BlockSpec((tb, qk_w), lambda b, t: (blk_fn(t), 0))
        assert offs['ret_q'] % qk_w == 0 and offs['ret_k'] % qk_w == 0 and offs['ret_v'] % BR_W == 0
        in_specs += [spec(qk_w, offs['ret_q'] // qk_w), spec(qk_w, offs['ret_k'] // qk_w),
                     spec(BR_W, offs['ret_v'] // BR_W), tbl(), tbl()]
    whole = lambda a: pl.BlockSpec(a.shape, lambda b, t: (0,) * a.ndim)
    in_specs += [whole(a) for a in consts]
    out_spec = lambda blk_fn: pl.BlockSpec((tb, BR_W), lambda b, t: (b * nblk + blk_fn(t), 0))
    return pl.pallas_call(
        functools.partial(_ret_body, n_chunks=tb // c_len),
        grid=(n // seq, nblk),
        in_specs=in_specs,
        out_specs=[out_spec(fwd_blk), out_spec(bwd_blk)],
        out_shape=[jax.ShapeDtypeStruct((n, BR_W), F32)] * 2,
        scratch_shapes=[pltpu.VMEM((2, RET_H, RET_DK, RET_DV), F32)],
        compiler_params=pltpu.CompilerParams(
            dimension_semantics=("parallel", "arbitrary"), vmem_limit_bytes=VMEM_LIMIT),
    )(*([p_mid, p_mid, p_mid, cos_t, sin_t] * 2), *consts)


def _ret_out_proj(o_f, o_b, p_mid, lp, w_proj, tm):
    def act(i, of, ob, g, gn_g, gn_b):
        o = of + ob
        g = g.astype(F32)
        normed = []
        for h in range(RET_H):
            oh = o[:, h * RET_DV:(h + 1) * RET_DV]
            oc = oh - jnp.mean(oh, axis=-1, keepdims=True)
            normed.append(oc * lax.rsqrt(jnp.mean(oc * oc, axis=-1, keepdims=True) + LN_EPS))
        yn = jnp.concatenate(normed, axis=1) * gn_g + gn_b
        return g * jax.nn.sigmoid(g) * yn

    off_g = sum(w for _, w in REST_COLS[:[name for name, _ in REST_COLS].index('ret_g')])
    assert off_g % BR_W == 0
    return _fused_mm([o_f, o_b, (p_mid, BR_W, off_g // BR_W)],
                     [_whole(lp['ret_gn_g'][None, :]), _whole(lp['ret_gn_b'][None, :])],
                     w_proj.astype(BF16), act, tm=tm, tn=w_proj.shape[1], out_dtype=BF16)


ROUTER_PAD = 128


def _norm2_router_body(z_ref, g_ref, shl_ref, scl_ref, shc_ref, scc_ref, whi_ref, wlo_ref, tok_ref, log_ref, *,
                       skip, cblk):
    xb = z_ref[...]
    y = xb * lax.rsqrt(jnp.mean(xb * xb, axis=-1, keepdims=True) + RMS_EPS) * g_ref[...]
    is_ctx = (pl.program_id(1) + skip) < cblk
    sc = jnp.where(is_ctx, scc_ref[...], scl_ref[0])
    sh = jnp.where(is_ctx, shc_ref[...], shl_ref[0])
    h = y * (1.0 + sc) + sh
    hi = h.astype(BF16)
    lo = (h - hi.astype(F32)).astype(BF16)
    tok_ref[...] = hi
    log_ref[...] = (jnp.dot(hi, whi_ref[...], preferred_element_type=F32)
                    + jnp.dot(lo, whi_ref[...], preferred_element_type=F32)
                    + jnp.dot(hi, wlo_ref[...], preferred_element_type=F32))


def _norm2_router(z, seq, tc, g, sh_l, sc_l, sh_c, sc_c, w_router, latent_only):
    n, d = z.shape
    bsz = n // seq
    tb = _pick_tile(math.gcd(seq, tc), 256, 16)
    nblk, cblk = seq // tb, tc // tb
    skip = cblk if latent_only else 0
    nb = nblk - skip
    w_pad = jnp.pad(w_router, ((0, 0), (0, ROUTER_PAD - w_router.shape[1])))
    w_hi = w_pad.astype(BF16)
    w_lo = (w_pad - w_hi.astype(F32)).astype(BF16)
    vec = pl.BlockSpec((1, d), lambda b, i: (0, 0))
    per_b = pl.BlockSpec((1, 1, d), lambda b, i: (b, 0, 0))
    wspec = pl.BlockSpec((d, ROUTER_PAD), lambda b, i: (0, 0))
    return pl.pallas_call(
        functools.partial(_norm2_router_body, skip=skip, cblk=cblk),
        grid=(bsz, nb),
        in_specs=[pl.BlockSpec((tb, d), lambda b, i: (b * nblk + skip + i, 0)), vec, per_b, per_b, vec, vec,
                  wspec, wspec],
        out_specs=[pl.BlockSpec((tb, d), lambda b, i: (b * nb + i, 0)),
                   pl.BlockSpec((tb, ROUTER_PAD), lambda b, i: (b * nb + i, 0))],
        out_shape=[jax.ShapeDtypeStruct((bsz * nb * tb, d), BF16),
                   jax.ShapeDtypeStruct((bsz * nb * tb, ROUTER_PAD), F32)],
        compiler_params=pltpu.CompilerParams(dimension_semantics=("parallel", "parallel"),
                                             vmem_limit_bytes=VMEM_LIMIT),
    )(z, g, sh_l, sc_l, sh_c, sc_c, w_hi, w_lo)


def _moe_body(be_ref, nu_ref, x_ref, w1_ref, w3_ref, w2_ref, o_ref, w1b, w3b, w2b):
    i = pl.program_id(0)

    @pl.when((i == 0) | (be_ref[i] != be_ref[jnp.maximum(i - 1, 0)]))
    def _():
        w1b[...] = w1_ref[0].astype(BF16)
        w3b[...] = w3_ref[0].astype(BF16)
        w2b[...] = w2_ref[0].astype(BF16)

    @pl.when(i < nu_ref[0])
    def _():
        xb = x_ref[...]
        h1 = jnp.dot(xb, w1b[...], preferred_element_type=F32)
        h3 = jnp.dot(xb, w3b[...], preferred_element_type=F32)
        hid = (h1 * jax.nn.sigmoid(h1) * h3).astype(BF16)
        o_ref[...] = jnp.dot(hid, w2b[...], preferred_element_type=F32).astype(o_ref.dtype)

    @pl.when(i >= nu_ref[0])
    def _():
        o_ref[...] = jnp.zeros_like(o_ref)


def _moe_experts(xb, block_e, n_used, w1, w3, w2, layer):
    n_rows, d = xb.shape
    n_blocks = n_rows // MOE_BM
    de = w1.shape[3]
    grid_spec = pltpu.PrefetchScalarGridSpec(
        num_scalar_prefetch=2,
        grid=(n_blocks,),
        in_specs=[
            pl.BlockSpec((MOE_BM, d), lambda i, be, nu: (i, 0)),
            pl.BlockSpec((None, 1, d, de), lambda i, be, nu: (layer, be[i], 0, 0)),
            pl.BlockSpec((None, 1, d, de), lambda i, be, nu: (layer, be[i], 0, 0)),
            pl.BlockSpec((None, 1, de, d), lambda i, be, nu: (layer, be[i], 0, 0)),
        ],
        out_specs=pl.BlockSpec((MOE_BM, d), lambda i, be, nu: (i, 0)),
        scratch_shapes=[pltpu.VMEM((d, de), BF16), pltpu.VMEM((d, de), BF16), pltpu.VMEM((de, d), BF16)],
    )
    return pl.pallas_call(
        _moe_body,
        grid_spec=grid_spec,
        out_shape=jax.ShapeDtypeStruct((n_rows, d), BF16),
        compiler_params=pltpu.CompilerParams(dimension_semantics=("arbitrary",),
                                             vmem_limit_bytes=VMEM_LIMIT),
    )(block_e, n_used, xb, w1, w3, w2)


def _moe(tokens, logits, b_router, w1, w3, w2, layer):
    n_tok = tokens.shape[0]
    aff = jax.nn.sigmoid(logits[:, :N_EXPERTS])
    assert TOP_K == 2
    sel = (aff + b_router).reshape(-1, N_GROUPS, EXPERTS_PER_GROUP)

    def top2(v):
        lane = jnp.arange(v.shape[-1], dtype=jnp.int32)
        i1 = jnp.argmax(v, axis=-1).astype(jnp.int32)
        rest = jnp.where(lane == i1[..., None], -jnp.inf, v)
        i2 = jnp.argmax(rest, axis=-1).astype(jnp.int32)
        return jnp.max(v, axis=-1), jnp.max(rest, axis=-1), i1, i2

    g1, g2, _, _ = top2(sel)
    grp = jnp.argmax(g1 + g2, axis=-1)
    grp_mask = jnp.arange(N_GROUPS)[None, :] == grp[:, None]
    masked = jnp.where(grp_mask[:, :, None], sel, -jnp.inf).reshape(-1, N_EXPERTS)
    _, _, e1, e2 = top2(masked)
    e_idx = jnp.stack([e1, e2], axis=-1)
    wts = jnp.take_along_axis(aff, e_idx, axis=1)
    wts = wts / jnp.sum(wts, axis=-1, keepdims=True)

    n_asg = n_tok * TOP_K
    flat_e = e_idx.reshape(-1).astype(jnp.int32)
    order = jnp.argsort(flat_e).astype(jnp.int32)
    se, st = flat_e[order], order // TOP_K
    counts = jnp.sum(flat_e[:, None] == jnp.arange(N_EXPERTS, dtype=jnp.int32)[None, :], axis=0, dtype=jnp.int32)
    starts = jnp.cumsum(counts) - counts
    padded = (counts + MOE_BM - 1) // MOE_BM * MOE_BM
    pad_end = jnp.cumsum(padded)
    pad_start = pad_end - padded
    dest = pad_start[se] + jnp.arange(n_asg, dtype=jnp.int32) - starts[se]
    dest_asg = dest[jnp.argsort(order)].reshape(n_tok, TOP_K)
    n_blocks = -(-n_asg // MOE_BM) + N_EXPERTS
    block_start = jnp.arange(n_blocks, dtype=jnp.int32) * MOE_BM
    block_e = jnp.minimum(jnp.sum(pad_end[None, :] <= block_start[:, None], axis=1, dtype=jnp.int32),
                          N_EXPERTS - 1)
    n_used = (pad_end[-1] // MOE_BM).astype(jnp.int32).reshape(1)
    slot_e = jnp.repeat(block_e, MOE_BM)
    rank = jnp.arange(n_blocks * MOE_BM, dtype=jnp.int32) - pad_start[slot_e]
    slot_tok = st[jnp.clip(starts[slot_e] + rank, 0, n_asg - 1)]
    xb = tokens[slot_tok]
    yb = _moe_experts(xb, block_e, n_used, w1, w3, w2, layer)
    return (wts[:, 0:1] * yb[dest_asg[:, 0]].astype(F32) + wts[:, 1:2] * yb[dest_asg[:, 1]].astype(F32))


def _merge_out(z, merge, branches, b_merge, w_out_l, g1, cg1, seq, tc, d):
    tm2 = _pick_tile(seq, 576, 16)
    bpb = seq // tm2

    def seg_select(i, ctx_val, lat_val):
        row = (i % bpb) * tm2 + lax.broadcasted_iota(jnp.int32, (tm2, 1), 0)
        return jnp.where(row < tc, ctx_val, lat_val)

    def merge_pro(i, mg, b0, b1, b2, b3, bm):
        gates = jax.nn.sigmoid(mg.astype(F32) + bm)
        b0, b1, b2, b3 = [b.astype(F32) for b in (b0, b1, b2, b3)]
        return (gates[:, 0 * d:1 * d] * b0 + gates[:, 1 * d:2 * d] * b1
                + gates[:, 2 * d:3 * d] * b2 + gates[:, 3 * d:4 * d] * b3)

    def resid_epi(i, acc, zb, gl, gc):
        return zb + seg_select(i, gc, gl[0]) * acc

    tn = d // 2
    return _fused_mm([merge] + branches, [_whole(b_merge[None, :])],
                     w_out_l.astype(BF16), merge_pro, tm=tm2, tn=tn,
                     epilogue=resid_epi, epi_rows=[z],
                     epi_aux=[(g1, (1, 1, tn), lambda i, j: (i // bpb, 0, j)),
                              (cg1, (1, tn), lambda i, j: (0, j))])


def kernel(x, c, ctx, c_ctx, w_ada, b_ada, norm1_g, norm2_g, w_in, b_merge, rwkv_mu_prev, rwkv_mu_next, rwkv_w0, rwkv_w2, rwkv_a0, rwkv_a2, rwkv_g2, rwkv_k_k, rwkv_k_a, rwkv_r_k, rwkv_gn_g, rwkv_gn_b, rwkv_proj, conv_w, conv_b, conv_ln_g, conv_ln_b, conv_proj, lru_conv_w, lru_conv_b, lru_wr, lru_br, lru_wi, lru_bi, lru_lambda, lru_proj, ret_gn_g, ret_gn_b, ret_proj, w_out, w_router, b_router, e_w1, e_w3, e_w2, final_g):
    bsz, n_tok, d = x.shape
    tc = ctx.shape[1]
    seq = tc + n_tok
    depth = w_in.shape[0]
    per_layer = {
        'rwkv_w0': rwkv_w0, 'rwkv_w2': rwkv_w2, 'rwkv_a0': rwkv_a0, 'rwkv_a2': rwkv_a2, 'rwkv_g2': rwkv_g2,
        'rwkv_k_k': rwkv_k_k, 'rwkv_k_a': rwkv_k_a, 'rwkv_r_k': rwkv_r_k,
        'rwkv_gn_g': rwkv_gn_g, 'rwkv_gn_b': rwkv_gn_b,
        'conv_w': conv_w, 'conv_b': conv_b, 'conv_ln_g': conv_ln_g, 'conv_ln_b': conv_ln_b,
        'lru_conv_w': lru_conv_w, 'lru_conv_b': lru_conv_b, 'lru_wr': lru_wr, 'lru_br': lru_br,
        'lru_wi': lru_wi, 'lru_bi': lru_bi, 'lru_lambda': lru_lambda,
        'ret_gn_g': ret_gn_g, 'ret_gn_b': ret_gn_b,
    }
    tm = _pick_tile(seq, 576)

    z = jnp.concatenate([ctx, x], axis=1).reshape(bsz * seq, d)
    for layer in range(depth):
        lp = {name: arr[layer] for name, arr in per_layer.items()}
        last = layer == depth - 1
        cc = jnp.concatenate([c, c_ctx[None, :]], axis=0)
        n_mod = -(-cc.shape[0] // 8) * 8
        cc = jnp.pad(cc, ((0, n_mod - cc.shape[0]), (0, 0)))
        mod = _fused_mm([cc], [], w_ada[layer].astype(BF16), lambda i, cb: cb * jax.nn.sigmoid(cb),
                        tm=n_mod, tn=6 * d // 4,
                        epilogue=lambda i, acc, bb: acc + bb,
                        epi_aux=[(b_ada[layer][None, :], (1, 6 * d // 4), lambda i, j: (0, j))])
        mod_l = mod[:bsz].reshape(bsz, 1, 6 * d)
        mod_c = mod[bsz:bsz + 1]
        sh1, sc1, g1, sh2, sc2, g2 = [mod_l[:, :, n * d:(n + 1) * d] for n in range(6)]
        csh1, csc1, cg1, csh2, csc2, cg2 = [mod_c[:, n * d:(n + 1) * d] for n in range(6)]

        w_l = w_in[layer].astype(BF16)
        norm_args = (norm1_g[layer][None, :], sh1, sc1, csh1, csc1)
        p_rwkv = _in_proj(z, seq, tc, *norm_args, w_l[:, :RWKV_WIDTH], 640,
                          rwkv_mu_prev[layer][None, :], rwkv_mu_next[layer][None, :], out_dtype=BF16)
        p_mid = _in_proj(z, seq, tc, *norm_args, w_l[:, RWKV_WIDTH:RWKV_WIDTH + MID_WIDTH], 896, out_dtype=BF16)
        p_merge = _in_proj(z, seq, tc, *norm_args, w_l[:, RWKV_WIDTH + MID_WIDTH:], 1024, out_dtype=BF16)

        y_f, y_b = _rwkv_mix(p_rwkv, seq, tc, lp)
        o_f, o_b = _ret_mix(p_mid, seq, tc, n_tok)
        branches = [
            _rwkv_out_proj(y_f, y_b, p_rwkv, lp, rwkv_proj[layer], tm),
            _plain_mm(_conv_mix(p_mid, seq, tc, lp), conv_proj[layer].astype(BF16), tm=tm, tn=d, out_dtype=BF16),
            _plain_mm(_lru_mix(p_mid, seq, tc, lp), lru_proj[layer].astype(BF16), tm=tm, tn=d, out_dtype=BF16),
            _ret_out_proj(o_f, o_b, p_mid, lp, ret_proj[layer], tm),
        ]
        z = _merge_out(z, p_merge, branches, b_merge[layer], w_out[layer], g1, cg1, seq, tc, d)

        z3 = z.reshape(bsz, seq, d)
        tok, logits = _norm2_router(z, seq, tc, norm2_g[layer][None, :], sh2, sc2, csh2, csc2, w_router,
                                    latent_only=last)
        y = _moe(tok, logits, b_router, e_w1, e_w3, e_w2, layer)
        if not last:
            is_ctx = (jnp.arange(seq) < tc)[None, :, None]
            gate2 = jnp.where(is_ctx, cg2[None], g2)
            z = (z3 + gate2 * y.reshape(bsz, seq, d)).reshape(bsz * seq, d)
        else:
            xl = z3[:, tc:] + g2 * y.reshape(bsz, n_tok, d)
            return xl * lax.rsqrt(jnp.mean(xl * xl, axis=-1, keepdims=True) + RMS_EPS) * final_g
    return None
```

```python
import functools
import math

import jax
import jax.numpy as jnp
from jax import lax
from jax.experimental import pallas as pl
from jax.experimental.pallas import tpu as pltpu

F32 = jnp.float32
BF16 = jnp.bfloat16

D_MODEL = 1024
GRID_W = 64
N_BRANCH = 4
BR_W = D_MODEL // 2
RWKV_HS = 64
RWKV_H = BR_W // RWKV_HS
RWKV_W_LORA = 64
RWKV_A_LORA = 64
RWKV_G_LORA = 128
RWKV_GN_EPS = 64e-5
L2_EPS = 1e-12
CONV_K = 31
LRU_BLOCKS = 8
LRU_BS = BR_W // LRU_BLOCKS
LRU_CONV = 4
LRU_C = 8.0
RET_H = 4
RET_DK = 64
RET_DV = BR_W // RET_H
RET_CHUNK = 128
ROPE_BASE = 10000.0
N_EXPERTS = 32
N_GROUPS = 4
EXPERTS_PER_GROUP = N_EXPERTS // N_GROUPS
TOP_K = 2
D_EXPERT = D_MODEL // 2
RMS_EPS = 1e-6
LN_EPS = 1e-5

REST_COLS = (('conv_val', BR_W), ('conv_gate', BR_W), ('lru_x', BR_W), ('lru_gate', BR_W),
             ('ret_q', RET_H * RET_DK), ('ret_k', RET_H * RET_DK), ('ret_v', BR_W), ('ret_g', BR_W))
RWKV_WIDTH = 3 * BR_W + 2 * RWKV_W_LORA + 2 * RWKV_A_LORA + RWKV_G_LORA
LORA_OFF = 3 * BR_W
LORA_W = 2 * RWKV_W_LORA + 2 * RWKV_A_LORA
MID_WIDTH = sum(w for _, w in REST_COLS)
MERGE_WIDTH = N_BRANCH * D_MODEL

VMEM_LIMIT = 48 * 1024 * 1024
RWKV_CHUNK = 64
RWKV_TB = 256
MOE_BM = 512


def _bdot(a, b):
    return jnp.dot(a.astype(BF16), b.astype(BF16), preferred_element_type=F32)


_NT = (((1,), (1,)), ((), ()))
_TN = (((0,), (0,)), ((), ()))


def _pick_tile(n, cap, mult=8):
    best = None
    for t in range(mult, cap + 1, mult):
        if n % t == 0:
            best = t
    assert best is not None, (n, cap, mult)
    return best


def _fused_mm_body(*refs, nr, na, ner, nea, prologue, epilogue, out_dtype):
    rows = refs[:nr]
    auxs = refs[nr:nr + na]
    w_ref = refs[nr + na]
    er = refs[nr + na + 1:nr + na + 1 + ner]
    ea = refs[nr + na + 1 + ner:nr + na + 1 + ner + nea]
    o_ref, act = refs[-2], refs[-1]
    i = pl.program_id(0)

    @pl.when(pl.program_id(1) == 0)
    def _():
        act[...] = prologue(i, *[r[...] for r in rows], *[a[...] for a in auxs]).astype(BF16)

    acc = jnp.dot(act[...], w_ref[...], preferred_element_type=F32)
    if epilogue is not None:
        acc = epilogue(i, acc, *[r[...] for r in er], *[a[...] for a in ea])
    o_ref[...] = acc.astype(out_dtype)


def _fused_mm(row_ins, aux_ins, w, prologue, *, tm, tn, out_dtype=F32, epilogue=None,
              epi_rows=(), epi_aux=()):
    row_ins = [r if isinstance(r, tuple) else (r, r.shape[1], 0) for r in row_ins]
    n = row_ins[0][0].shape[0]
    k, m = w.shape
    assert n % tm == 0 and m % tn == 0, (n, tm, m, tn)
    in_specs = [pl.BlockSpec((tm, bw), functools.partial(lambda i, j, cb: (i, cb), cb=cb))
                for (_, bw, cb) in row_ins]
    in_specs += [pl.BlockSpec(bs, im) for (_, bs, im) in aux_ins]
    in_specs += [pl.BlockSpec((k, tn), lambda i, j: (0, j))]
    in_specs += [pl.BlockSpec((tm, tn), lambda i, j: (i, j)) for _ in epi_rows]
    in_specs += [pl.BlockSpec(bs, im) for (_, bs, im) in epi_aux]
    body = functools.partial(_fused_mm_body, nr=len(row_ins), na=len(aux_ins), ner=len(epi_rows),
                             nea=len(epi_aux), prologue=prologue, epilogue=epilogue, out_dtype=out_dtype)
    return pl.pallas_call(
        body,
        grid=(n // tm, m // tn),
        in_specs=in_specs,
        out_specs=pl.BlockSpec((tm, tn), lambda i, j: (i, j)),
        out_shape=jax.ShapeDtypeStruct((n, m), out_dtype),
        scratch_shapes=[pltpu.VMEM((tm, k), BF16)],
        compiler_params=pltpu.CompilerParams(dimension_semantics=("parallel", "arbitrary"),
                                             vmem_limit_bytes=VMEM_LIMIT),
    )(*[a for (a, _, _) in row_ins], *[a for (a, _, _) in aux_ins], w, *epi_rows,
      *[a for (a, _, _) in epi_aux])


def _whole(a):
    nd = a.ndim
    return (a, a.shape, lambda i, j: (0,) * nd)


def _plain_mm(x, w, *, tm, tn, out_dtype=F32):
    return _fused_mm([x], [], w, lambda i, xb: xb, tm=tm, tn=tn, out_dtype=out_dtype)


def _norm_mod_body(z_ref, g_ref, shl_ref, scl_ref, shc_ref, scc_ref, o_ref, *, cblk):
    xb = z_ref[...]
    y = xb * lax.rsqrt(jnp.mean(xb * xb, axis=-1, keepdims=True) + RMS_EPS) * g_ref[...]
    is_ctx = pl.program_id(1) < cblk
    sc = jnp.where(is_ctx, scc_ref[...], scl_ref[0])
    sh = jnp.where(is_ctx, shc_ref[...], shl_ref[0])
    o_ref[...] = (y * (1.0 + sc) + sh).astype(o_ref.dtype)


def _norm_mod(z, seq, tc, g, sh_l, sc_l, sh_c, sc_c):
    n, d = z.shape
    tb = _pick_tile(math.gcd(seq, tc), 256, 16)
    nblk = seq // tb
    vec = pl.BlockSpec((1, d), lambda b, i: (0, 0))
    per_b = pl.BlockSpec((1, 1, d), lambda b, i: (b, 0, 0))
    blk = pl.BlockSpec((tb, d), lambda b, i: (b * nblk + i, 0))
    return pl.pallas_call(
        functools.partial(_norm_mod_body, cblk=tc // tb),
        grid=(n // seq, nblk),
        in_specs=[blk, vec, per_b, per_b, vec, vec],
        out_specs=blk,
        out_shape=jax.ShapeDtypeStruct((n, d), BF16),
        compiler_params=pltpu.CompilerParams(dimension_semantics=("parallel", "parallel"),
                                             vmem_limit_bytes=VMEM_LIMIT),
    )(z, g, sh_l, sc_l, sh_c, sc_c)


def _in_proj_body(*refs, tc, shift):
    if shift:
        h_ref, w_ref, mup_ref, mun_ref, o_ref = refs
    else:
        h_ref, w_ref, o_ref = refs
    seq = h_ref.shape[0]
    row = lax.broadcasted_iota(jnp.int32, (seq, 1), 0)
    acc = jnp.dot(h_ref[...], w_ref[...], preferred_element_type=F32)
    if shift:
        prev = pltpu.roll(acc, 1, 0)
        nxt = pltpu.roll(acc, seq - 1, 0)
        prev = jnp.where((row == 0) | (row == tc), 0.0, prev)
        nxt = jnp.where((row == tc - 1) | (row == seq - 1), 0.0, nxt)
        acc = acc + mup_ref[...] * (prev - acc) + mun_ref[...] * (nxt - acc)
    o_ref[...] = acc.astype(o_ref.dtype)


def _in_proj(h, seq, tc, w, tn, mu_prev=None, mu_next=None, out_dtype=F32):
    n, d = h.shape
    m = w.shape[1]
    shift = mu_prev is not None
    assert m % tn == 0
    in_specs = [pl.BlockSpec((seq, d), lambda b, j: (b, 0)), pl.BlockSpec((d, tn), lambda b, j: (0, j))]
    args = [h, w]
    if shift:
        in_specs += [pl.BlockSpec((1, tn), lambda b, j: (0, j))] * 2
        args += [mu_prev, mu_next]
    return pl.pallas_call(
        functools.partial(_in_proj_body, tc=tc, shift=shift),
        grid=(n // seq, m // tn),
        in_specs=in_specs,
        out_specs=pl.BlockSpec((seq, tn), lambda b, j: (b, j)),
        out_shape=jax.ShapeDtypeStruct((n, m), out_dtype),
        compiler_params=pltpu.CompilerParams(dimension_semantics=("parallel", "arbitrary"),
                                             vmem_limit_bytes=VMEM_LIMIT),
    )(*args)


def _rwkv_body(rf, kf, vf, lf, rb_, kb, vb_, lb, w0_ref, w2_ref, a0_ref, a2_ref, kkw_ref, ka_ref, bd_ref,
               of_ref, ob_ref, s_ref, kk_ref, *, n_chunks):
    c_len = RWKV_CHUNK

    @pl.when(pl.program_id(1) == 0)
    def _():
        s_ref[...] = jnp.zeros_like(s_ref)

    for d, k_blk in enumerate((kf, kb)):
        kx = k_blk[...].astype(F32) * kkw_ref[...]
        kk_ref[d] = kx * lax.rsqrt(_bdot(kx * kx, bd_ref[...]) + L2_EPS)

    ri = lax.broadcasted_iota(jnp.int32, (c_len, c_len), 0)
    ci = lax.broadcasted_iota(jnp.int32, (c_len, c_len), 1)
    eye = (ri == ci).astype(F32)
    in_refs = ((rf, kf, vf, lf), (rb_, kb, vb_, lb))
    masks = []
    for d in range(2):
        before = (ri > ci) if d == 0 else (ri < ci)
        strict = before.astype(F32)
        incl = (before | (ri == ci)).astype(F32)
        m8 = jnp.where(((ri // 8) == (ci // 8)) & before, 1.0, 0.0).astype(F32)
        merges = []
        for size in (16, 32, 64):
            same = (ri // size) == (ci // size)
            inner = (ri // (size // 2)) == (ci // (size // 2))
            merges.append(jnp.where(same & jnp.logical_not(inner) & before, 1.0, 0.0).astype(F32))
        masks.append(dict(strict=strict, incl=incl, incl_b=incl.astype(BF16), m8=m8, merges=merges))
    out_refs = (of_ref, ob_ref)
    decay_scale = math.exp(-0.5)

    def chunk(c, carry):
        per_dir = []
        for d in range(2):
            cc = c if d == 0 else n_chunks - 1 - c
            sl = pl.ds(pl.multiple_of(cc * c_len, c_len), c_len)
            r_ref, k_ref, v_ref, lo_ref = in_refs[d]
            r = r_ref[sl, :].astype(F32)
            k = k_ref[sl, :].astype(F32)
            v = v_ref[sl, :].astype(F32)
            lo = lo_ref[sl, :].astype(F32)
            wlo = lo[:, d * RWKV_W_LORA:(d + 1) * RWKV_W_LORA]
            alo = lo[:, 2 * RWKV_W_LORA + d * RWKV_A_LORA:2 * RWKV_W_LORA + (d + 1) * RWKV_A_LORA]
            w_raw = w0_ref[d] + _bdot(jnp.tanh(wlo), w2_ref[d])
            lw = -decay_scale * jax.nn.sigmoid(w_raw)
            a = jax.nn.sigmoid(a0_ref[d] + _bdot(alo, a2_ref[d]))
            k_d = k * (1.0 + (a - 1.0) * ka_ref[...])
            kk = kk_ref[d, sl, :]
            hi = lw.astype(BF16)
            lo2 = (lw - hi.astype(F32)).astype(BF16)
            tri = masks[d]['incl_b']
            cum = (jnp.dot(tri, hi, preferred_element_type=F32)
                   + jnp.dot(tri, lo2, preferred_element_type=F32))
            eg = jnp.exp(cum)
            ieg = jnp.exp(-cum)
            last = c_len - 1 if d == 0 else 0
            per_dir.append(dict(
                rt=(r * eg).astype(BF16), kt=(k_d * ieg).astype(BF16), at=(-(kk * a) * ieg).astype(BF16),
                bt=(kk * jnp.exp(cum - lw)).astype(BF16), vb=v.astype(BF16), g_last=eg[last:last + 1, :],
                rows=sl))
        chains = [(d, h) for d in range(2) for h in range(RWKV_H)]
        hsl = lambda h: slice(h * RWKV_HS, (h + 1) * RWKV_HS)
        rb = [jnp.concatenate([per_dir[d]['rt'][:, hsl(h)], per_dir[d]['bt'][:, hsl(h)]], axis=0)
              for d, h in chains]
        at_h = [per_dir[d]['at'][:, hsl(h)] for d, h in chains]
        kt_h = [per_dir[d]['kt'][:, hsl(h)] for d, h in chains]
        v_h = [per_dir[d]['vb'][:, hsl(h)] for d, h in chains]
        mk = [masks[d] for d, h in chains]
        n = len(chains)
        m_a = [lax.dot_general(rb[i], at_h[i], _NT, preferred_element_type=F32) for i in range(n)]
        m_k = [lax.dot_general(rb[i], kt_h[i], _NT, preferred_element_type=F32) for i in range(n)]
        ra = [m_a[i][:c_len] * mk[i]['incl'] for i in range(n)]
        nmat = [m_a[i][c_len:] * mk[i]['strict'] for i in range(n)]
        rk = [m_k[i][:c_len] * mk[i]['incl'] for i in range(n)]
        bk = [m_k[i][c_len:] * mk[i]['strict'] for i in range(n)]
        d1 = [nmat[i] * mk[i]['m8'] for i in range(n)]
        d2 = [_bdot(d1[i], d1[i]) for i in range(n)]
        d4 = [_bdot(d2[i], d2[i]) for i in range(n)]
        tinv = [eye + d1[i] for i in range(n)]
        tinv = [tinv[i] + _bdot(tinv[i], d2[i]) for i in range(n)]
        tinv = [tinv[i] + _bdot(tinv[i], d4[i]) for i in range(n)]
        for lvl in range(3):
            e = [_bdot(tinv[i], nmat[i] * mk[i]['merges'][lvl]) for i in range(n)]
            tinv = [tinv[i] + _bdot(e[i], tinv[i]) for i in range(n)]
        s0 = [s_ref[d, h] for d, h in chains]
        rbs = [lax.dot_general(rb[i], s0[i].astype(BF16), _NT, preferred_element_type=F32) for i in range(n)]
        x = [rbs[i][c_len:] + _bdot(bk[i], v_h[i]) for i in range(n)]
        u = [_bdot(tinv[i], x[i]) for i in range(n)]
        y = [rbs[i][:c_len] + _bdot(ra[i], u[i]) + _bdot(rk[i], v_h[i]) for i in range(n)]
        for i, (d, h) in enumerate(chains):
            uv = jnp.concatenate([u[i].astype(BF16), v_h[i]], axis=0)
            ak = jnp.concatenate([at_h[i], kt_h[i]], axis=0)
            ds = lax.dot_general(uv, ak, _TN, preferred_element_type=F32)
            s_ref[d, h] = (s0[i] + ds) * per_dir[d]['g_last'][:, hsl(h)]
        for d in range(2):
            out_refs[d][per_dir[d]['rows'], :] = jnp.concatenate(
                [y[d * RWKV_H + h] for h in range(RWKV_H)], axis=1)
        return carry

    lax.fori_loop(0, n_chunks, chunk, 0)


def _rwkv_mix(p_rwkv, seq, tc, lp):
    n = p_rwkv.shape[0]
    bsz = n // seq
    tb = _pick_tile(math.gcd(seq, tc), RWKV_TB, RWKV_CHUNK)
    nblk, cblk = seq // tb, tc // tb
    assert LORA_OFF % LORA_W == 0

    def fwd_blk(t):
        return t

    def bwd_blk(t):
        return jnp.where(t < cblk, cblk - 1 - t, nblk - 1 - (t - cblk))

    def col_spec(width, cb, blk_fn):
        return pl.BlockSpec((tb, width), lambda b, t: (b * nblk + blk_fn(t), cb))

    in_specs = []
    for blk_fn in (fwd_blk, bwd_blk):
        in_specs += [col_spec(BR_W, 0, blk_fn), col_spec(BR_W, 1, blk_fn), col_spec(BR_W, 2, blk_fn),
                     col_spec(LORA_W, LORA_OFF // LORA_W, blk_fn)]
    whole = lambda a: pl.BlockSpec(a.shape, lambda b, t: (0,) * a.ndim)
    head_id = jnp.arange(BR_W) // RWKV_HS
    bd = (head_id[:, None] == head_id[None, :]).astype(BF16)
    consts = [lp['rwkv_w0'][:, None, :], lp['rwkv_w2'].astype(BF16), lp['rwkv_a0'][:, None, :],
              lp['rwkv_a2'].astype(BF16), lp['rwkv_k_k'][None, :], lp['rwkv_k_a'][None, :], bd]
    in_specs += [whole(a) for a in consts]
    return pl.pallas_call(
        functools.partial(_rwkv_body, n_chunks=tb // RWKV_CHUNK),
        grid=(bsz, nblk),
        in_specs=in_specs,
        out_specs=[col_spec(BR_W, 0, fwd_blk), col_spec(BR_W, 0, bwd_blk)],
        out_shape=[jax.ShapeDtypeStruct((n, BR_W), F32)] * 2,
        scratch_shapes=[pltpu.VMEM((2, RWKV_H, RWKV_HS, RWKV_HS), F32), pltpu.VMEM((2, tb, BR_W), F32)],
        compiler_params=pltpu.CompilerParams(
            dimension_semantics=("parallel", "arbitrary"), vmem_limit_bytes=VMEM_LIMIT),
    )(*([p_rwkv] * 8), *consts)


def _rwkv_out_proj(y_f, y_b, p_rwkv, lp, w_proj, tm):
    head_id = jnp.arange(BR_W) // RWKV_HS
    bd = (head_id[:, None] == head_id[None, :]).astype(BF16)

    def hsum(xv, bdm):
        hi = xv.astype(BF16)
        lo = (xv - hi.astype(F32)).astype(BF16)
        return (jnp.dot(hi, bdm, preferred_element_type=F32) + jnp.dot(lo, bdm, preferred_element_type=F32))

    def act(i, yf, yb, r, k, v, glo, gn_g, gn_b, r_k, g2, bdm):
        inv = 1.0 / RWKV_HS
        r, k, v, glo = [t.astype(F32) for t in (r, k, v, glo)]
        yb = yf + yb
        yc = yb - hsum(yb, bdm) * inv
        var = hsum(yc * yc, bdm) * inv
        yn = yc * lax.rsqrt(var + RWKV_GN_EPS) * gn_g + gn_b
        bonus = hsum(r * k * r_k, bdm) * v
        g = _bdot(jax.nn.sigmoid(glo), g2)
        return (yn + bonus) * g

    glo_cb = (LORA_OFF + LORA_W) // RWKV_G_LORA
    assert (LORA_OFF + LORA_W) % RWKV_G_LORA == 0
    return _fused_mm(
        [y_f, y_b, (p_rwkv, BR_W, 0), (p_rwkv, BR_W, 1), (p_rwkv, BR_W, 2), (p_rwkv, RWKV_G_LORA, glo_cb)],
        [_whole(lp['rwkv_gn_g'][None, :]), _whole(lp['rwkv_gn_b'][None, :]), _whole(lp['rwkv_r_k'][None, :]),
         _whole(lp['rwkv_g2'].astype(BF16)), _whole(bd)],
        w_proj.astype(BF16), act, tm=tm, tn=w_proj.shape[1], out_dtype=BF16)


LRU_TB = 256
LRU_HALO = 16


def _lru_body(x_ref, g_ref, cw_ref, cb_ref, wr_ref, br_ref, wi_ref, bi_ref, sp_ref, o_ref, h_ref, *,
              tb, nblk, cblk):
    seq = x_ref.shape[0]
    row = lax.broadcasted_iota(jnp.int32, (tb, 1), 0)
    n_ext = tb + LRU_HALO
    for d in range(2):
        h_ref[...] = jnp.zeros_like(h_ref)

        def blk_body(i, carry, d=d):
            if d == 0:
                blk = i
                halo_ok = (blk != 0) & (blk != cblk)
                halo0 = jnp.maximum(blk * tb - LRU_HALO, 0)
            else:
                blk = jnp.where(i < cblk, cblk - 1 - i, nblk - 1 - (i - cblk))
                halo_ok = (blk != cblk - 1) & (blk != nblk - 1)
                halo0 = jnp.minimum(blk * tb + tb, seq - LRU_HALO)
            rows = pl.ds(pl.multiple_of(blk * tb, tb), tb)
            cur = x_ref[rows, :].astype(F32)
            halo = (x_ref[pl.ds(pl.multiple_of(halo0, LRU_HALO), LRU_HALO), :].astype(F32)
                    * jnp.where(halo_ok, 1.0, 0.0))
            xc = cb_ref[d]
            if d == 0:
                ext = jnp.concatenate([halo, cur], axis=0)
                for j in range(LRU_CONV):
                    sh = LRU_CONV - 1 - j
                    tap = ext if sh == 0 else pltpu.roll(ext, sh, 0)
                    xc = xc + cw_ref[d, j:j + 1, :] * tap[LRU_HALO:, :]
            else:
                ext = jnp.concatenate([cur, halo], axis=0)
                for j in range(LRU_CONV):
                    tap = ext if j == 0 else pltpu.roll(ext, n_ext - j, 0)
                    xc = xc + cw_ref[d, j:j + 1, :] * tap[:tb, :]
            gate_r = jax.nn.sigmoid(_bdot(xc, wr_ref[d]) + br_ref[d])
            gate_i = jax.nn.sigmoid(_bdot(xc, wi_ref[d]) + bi_ref[d])
            log_a = -LRU_C * gate_r * sp_ref[d]
            a_cum = jnp.exp(log_a)
            b_cum = jnp.sqrt(1.0 - jnp.exp(2.0 * log_a)) * (gate_i * xc)
            s = 1
            while s < tb:
                if d == 0:
                    ok = row >= s
                    a_sh = jnp.where(ok, pltpu.roll(a_cum, s, 0), 1.0)
                    b_sh = jnp.where(ok, pltpu.roll(b_cum, s, 0), 0.0)
                else:
                    ok = row < tb - s
                    a_sh = jnp.where(ok, pltpu.roll(a_cum, tb - s, 0), 1.0)
                    b_sh = jnp.where(ok, pltpu.roll(b_cum, tb - s, 0), 0.0)
                b_cum = a_cum * b_sh + b_cum
                a_cum = a_cum * a_sh
                s *= 2
            h = b_cum + a_cum * h_ref[...]
            if d == 0:
                h_ref[...] = h[tb - 1:tb, :]
                o_ref[rows, :] = h
            else:
                h_ref[...] = h[0:1, :]
                o_ref[rows, :] = (o_ref[rows, :] + h) * jax.nn.gelu(g_ref[rows, :].astype(F32))
            return carry

        lax.fori_loop(0, nblk, blk_body, 0)


def _lru_mix(p_mid, seq, tc, lp):
    n = p_mid.shape[0]
    tb = _pick_tile(math.gcd(seq, tc), LRU_TB)
    nblk, cblk = seq // tb, tc // tb
    eye = jnp.eye(LRU_BLOCKS, dtype=F32)
    blockdiag = lambda w: jnp.einsum('dgij,gh->dgihj', w, eye).reshape(2, BR_W, BR_W).astype(BF16)
    consts = [lp['lru_conv_w'], lp['lru_conv_b'][:, None, :], blockdiag(lp['lru_wr']), lp['lru_br'][:, None, :],
              blockdiag(lp['lru_wi']), lp['lru_bi'][:, None, :], jax.nn.softplus(-lp['lru_lambda'])[:, None, :]]
    whole = lambda a: pl.BlockSpec(a.shape, lambda b: (0,) * a.ndim)
    x_cb = [name for name, _ in REST_COLS].index('lru_x')
    g_cb = [name for name, _ in REST_COLS].index('lru_gate')
    assert all(w == BR_W for _, w in REST_COLS[:max(x_cb, g_cb) + 1])
    return pl.pallas_call(
        functools.partial(_lru_body, tb=tb, nblk=nblk, cblk=cblk),
        grid=(n // seq,),
        in_specs=[pl.BlockSpec((seq, BR_W), lambda b: (b, x_cb)), pl.BlockSpec((seq, BR_W), lambda b: (b, g_cb))]
        + [whole(a) for a in consts],
        out_specs=pl.BlockSpec((seq, BR_W), lambda b: (b, 0)),
        out_shape=jax.ShapeDtypeStruct((n, BR_W), F32),
        scratch_shapes=[pltpu.VMEM((1, BR_W), F32)],
        compiler_params=pltpu.CompilerParams(dimension_semantics=("parallel",), vmem_limit_bytes=VMEM_LIMIT),
    )(p_mid, p_mid, *consts)


CONV_TB = 256
CONV_HALO = 16


def _conv_body(val_ref, gate_ref, w_ref, b_ref, lng_ref, lnb_ref, o_ref, *, tb, nblk, cblk):
    seq = val_ref.shape[0]
    n_ext = tb + 2 * CONV_HALO
    pad = CONV_K // 2

    def glu(rows):
        return val_ref[rows, :].astype(F32) * jax.nn.sigmoid(gate_ref[rows, :].astype(F32))

    def blk_body(blk, carry):
        t0 = blk * tb
        rows = pl.ds(pl.multiple_of(t0, tb), tb)
        lo_ok = (blk != 0) & (blk != cblk)
        hi_ok = (blk != cblk - 1) & (blk != nblk - 1)
        lo0 = jnp.maximum(t0 - CONV_HALO, 0)
        hi0 = jnp.minimum(t0 + tb, seq - CONV_HALO)
        lo = glu(pl.ds(pl.multiple_of(lo0, CONV_HALO), CONV_HALO)) * jnp.where(lo_ok, 1.0, 0.0)
        hi = glu(pl.ds(pl.multiple_of(hi0, CONV_HALO), CONV_HALO)) * jnp.where(hi_ok, 1.0, 0.0)
        ext = jnp.concatenate([lo, glu(rows), hi], axis=0)
        acc = jnp.zeros((tb, ext.shape[1]), F32) + b_ref[...]
        for r in range(8):
            rolled = ext if r == 0 else pltpu.roll(ext, n_ext - r, 0)
            for j in range(CONV_K):
                off = CONV_HALO - pad + j
                if off % 8 == r:
                    acc = acc + w_ref[j:j + 1, :] * rolled[off - r:off - r + tb, :]
        mu = jnp.mean(acc, axis=-1, keepdims=True)
        xc = acc - mu
        yn = xc * lax.rsqrt(jnp.mean(xc * xc, axis=-1, keepdims=True) + LN_EPS) * lng_ref[...] + lnb_ref[...]
        o_ref[rows, :] = yn * jax.nn.sigmoid(yn)
        return carry

    lax.fori_loop(0, nblk, blk_body, 0)


def _conv_mix(p_mid, seq, tc, lp):
    n = p_mid.shape[0]
    tb = _pick_tile(math.gcd(seq, tc), CONV_TB, CONV_HALO)
    names = [name for name, _ in REST_COLS]
    v_cb, g_cb = names.index('conv_val'), names.index('conv_gate')
    assert all(w == BR_W for _, w in REST_COLS[:max(v_cb, g_cb) + 1]) and CONV_HALO >= CONV_K // 2
    consts = [lp['conv_w'], lp['conv_b'][None, :], lp['conv_ln_g'][None, :], lp['conv_ln_b'][None, :]]
    whole = lambda a: pl.BlockSpec(a.shape, lambda b: (0,) * a.ndim)
    return pl.pallas_call(
        functools.partial(_conv_body, tb=tb, nblk=seq // tb, cblk=tc // tb),
        grid=(n // seq,),
        in_specs=[pl.BlockSpec((seq, BR_W), lambda b: (b, v_cb)), pl.BlockSpec((seq, BR_W), lambda b: (b, g_cb))]
        + [whole(a) for a in consts],
        out_specs=pl.BlockSpec((seq, BR_W), lambda b: (b, 0)),
        out_shape=jax.ShapeDtypeStruct((n, BR_W), F32),
        compiler_params=pltpu.CompilerParams(dimension_semantics=("parallel",), vmem_limit_bytes=VMEM_LIMIT),
    )(p_mid, p_mid, *consts)


RET_TB = 256
_RET_LOG_G = [math.log1p(-2.0 ** (-5.0 - h)) for h in range(RET_H)]


def _ret_body(qf, kf, vf, cf, sf, qb, kb, vb, cb, sb, dm_ref, xi_ref, zt_ref, of_ref, ob_ref, s_ref, *, n_chunks):
    c_len = RET_CHUNK
    qk_w = RET_H * RET_DK

    @pl.when(pl.program_id(1) == 0)
    def _():
        s_ref[...] = jnp.zeros_like(s_ref)

    lane = lax.broadcasted_iota(jnp.int32, (c_len, qk_w), 1)
    first_half = (lane % RET_DK) < (RET_DK // 2)

    def rotary(z, cos, sin):
        swapped = jnp.where(first_half, pltpu.roll(z, qk_w - RET_DK // 2, 1), pltpu.roll(z, RET_DK // 2, 1))
        return z * cos + swapped * sin

    in_refs = ((qf, kf, vf, cf, sf), (qb, kb, vb, cb, sb))
    out_refs = (of_ref, ob_ref)

    def chunk(c, carry):
        per_dir = []
        for d in range(2):
            cc = c if d == 0 else n_chunks - 1 - c
            sl = pl.ds(pl.multiple_of(cc * c_len, c_len), c_len)
            q_ref, k_ref, v_ref, c_ref, sn_ref = in_refs[d]
            cos, sin = c_ref[sl, :], sn_ref[sl, :]
            q = rotary(q_ref[sl, :].astype(F32), cos, sin)
            k = rotary(k_ref[sl, :].astype(F32), cos, sin) * (RET_DK ** -0.5)
            per_dir.append(dict(q=q.astype(BF16), k=k.astype(BF16), kz=(k * zt_ref[d]).astype(BF16),
                                v=v_ref[sl, :].astype(BF16), sl=sl))
        groups = [(d, h) for d in range(2) for h in range(RET_H)]
        ksl = lambda h: slice(h * RET_DK, (h + 1) * RET_DK)
        vsl = lambda h: slice(h * RET_DV, (h + 1) * RET_DV)
        q_h = [per_dir[d]['q'][:, ksl(h)] for d, h in groups]
        v_h = [per_dir[d]['v'][:, vsl(h)] for d, h in groups]
        att = [lax.dot_general(q_h[i], per_dir[d]['k'][:, ksl(h)], _NT, preferred_element_type=F32) * dm_ref[d, h]
               for i, (d, h) in enumerate(groups)]
        s0 = [s_ref[d, h] for d, h in groups]
        o = [_bdot(att[i], v_h[i]) + _bdot(q_h[i], s0[i]) * xi_ref[d][:, vsl(h)]
             for i, (d, h) in enumerate(groups)]
        for i, (d, h) in enumerate(groups):
            kv = lax.dot_general(per_dir[d]['kz'][:, ksl(h)], v_h[i], _TN, preferred_element_type=F32)
            s_ref[d, h] = s0[i] * math.exp(_RET_LOG_G[h] * c_len) + kv
        for d in range(2):
            out_refs[d][per_dir[d]['sl'], :] = jnp.concatenate([o[d * RET_H + h] for h in range(RET_H)], axis=1)
        return carry

    lax.fori_loop(0, n_chunks, chunk, 0)


def _ret_mix(p_mid, seq, tc, n_tok):
    n = p_mid.shape[0]
    tb = _pick_tile(math.gcd(seq, tc), RET_TB, RET_CHUNK)
    nblk, cblk = seq // tb, tc // tb
    c_len = RET_CHUNK
    n_rows = n_tok // GRID_W
    rows = jnp.repeat(jnp.arange(n_rows, dtype=F32), GRID_W)
    cols = jnp.tile(jnp.arange(GRID_W, dtype=F32), n_rows)
    n_freq = RET_DK // 4
    inv = ROPE_BASE ** (-jnp.arange(n_freq, dtype=F32) / n_freq)
    ang = jnp.concatenate([rows[:, None] * inv, cols[:, None] * inv], axis=-1)
    cos_h = jnp.concatenate([jnp.cos(ang), jnp.cos(ang)], axis=-1)
    sin_h = jnp.concatenate([-jnp.sin(ang), jnp.sin(ang)], axis=-1)
    pad_ctx = lambda tbl, fill: jnp.concatenate(
        [jnp.full((tc, RET_H * RET_DK), fill, F32), jnp.tile(tbl, (1, RET_H))], axis=0)
    cos_t, sin_t = pad_ctx(cos_h, 1.0), pad_ctx(sin_h, 0.0)
    log_g = jnp.asarray(_RET_LOG_G, F32)
    idx = jnp.arange(c_len, dtype=F32)
    dm, xi, zt = [], [], []
    for d in range(2):
        pos = idx if d == 0 else c_len - 1.0 - idx
        diff = pos[:, None] - pos[None, :]
        keep = diff >= 0 if d == 0 else diff > 0
        dm.append(jnp.where(keep[None], jnp.exp(log_g[:, None, None] * jnp.maximum(diff, 0.0)[None]), 0.0))
        xi.append(jnp.repeat(jnp.exp(log_g[None, :] * (pos[:, None] + 1.0)), RET_DV, axis=1))
        zt.append(jnp.repeat(jnp.exp(log_g[None, :] * (c_len - 1.0 - pos)[:, None]), RET_DK, axis=1))
    consts = [jnp.stack(dm), jnp.stack(xi), jnp.stack(zt)]

    def fwd_blk(t):
        return t

    def bwd_blk(t):
        return jnp.where(t < cblk, cblk - 1 - t, nblk - 1 - (t - cblk))

    names = [name for name, _ in REST_COLS]
    offs = {name: sum(w for _, w in REST_COLS[:i]) for i, (name, _) in enumerate(REST_COLS)}
    qk_w = RET_H * RET_DK
    in_specs = []
    for blk_fn in (fwd_blk, bwd_blk):
        spec = lambda width, cb, blk_fn=blk_fn: pl.BlockSpec((tb, width), lambda b, t: (b * nblk + blk_fn(t), cb))
        tbl = lambda blk_fn=blk_fn: pl.BlockSpec((tb, qk_w), lambda b, t: (blk_fn(t), 0))
        assert offs['ret_q'] % qk_w == 0 and offs['ret_k'] % qk_w == 0 and offs['ret_v'] % BR_W == 0
        in_specs += [spec(qk_w, offs['ret_q'] // qk_w), spec(qk_w, offs['ret_k'] // qk_w),
                     spec(BR_W, offs['ret_v'] // BR_W), tbl(), tbl()]
    whole = lambda a: pl.BlockSpec(a.shape, lambda b, t: (0,) * a.ndim)
    in_specs += [whole(a) for a in consts]
    out_spec = lambda blk_fn: pl.BlockSpec((tb, BR_W), lambda b, t: (b * nblk + blk_fn(t), 0))
    return pl.pallas_call(
        functools.partial(_ret_body, n_chunks=tb // c_len),
        grid=(n // seq, nblk),
        in_specs=in_specs,
        out_specs=[out_spec(fwd_blk), out_spec(bwd_blk)],
        out_shape=[jax.ShapeDtypeStruct((n, BR_W), F32)] * 2,
        scratch_shapes=[pltpu.VMEM((2, RET_H, RET_DK, RET_DV), F32)],
        compiler_params=pltpu.CompilerParams(
            dimension_semantics=("parallel", "arbitrary"), vmem_limit_bytes=VMEM_LIMIT),
    )(*([p_mid, p_mid, p_mid, cos_t, sin_t] * 2), *consts)


def _ret_out_proj(o_f, o_b, p_mid, lp, w_proj, tm):
    def act(i, of, ob, g, gn_g, gn_b):
        o = of + ob
        g = g.astype(F32)
        normed = []
        for h in range(RET_H):
            oh = o[:, h * RET_DV:(h + 1) * RET_DV]
            oc = oh - jnp.mean(oh, axis=-1, keepdims=True)
            normed.append(oc * lax.rsqrt(jnp.mean(oc * oc, axis=-1, keepdims=True) + LN_EPS))
        yn = jnp.concatenate(normed, axis=1) * gn_g + gn_b
        return g * jax.nn.sigmoid(g) * yn

    off_g = sum(w for _, w in REST_COLS[:[name for name, _ in REST_COLS].index('ret_g')])
    assert off_g % BR_W == 0
    return _fused_mm([o_f, o_b, (p_mid, BR_W, off_g // BR_W)],
                     [_whole(lp['ret_gn_g'][None, :]), _whole(lp['ret_gn_b'][None, :])],
                     w_proj.astype(BF16), act, tm=tm, tn=w_proj.shape[1], out_dtype=BF16)


ROUTER_PAD = 128


def _norm2_router_body(z_ref, g_ref, shl_ref, scl_ref, shc_ref, scc_ref, whi_ref, wlo_ref, tok_ref, log_ref, *,
                       skip, cblk):
    xb = z_ref[...]
    y = xb * lax.rsqrt(jnp.mean(xb * xb, axis=-1, keepdims=True) + RMS_EPS) * g_ref[...]
    is_ctx = (pl.program_id(1) + skip) < cblk
    sc = jnp.where(is_ctx, scc_ref[...], scl_ref[0])
    sh = jnp.where(is_ctx, shc_ref[...], shl_ref[0])
    h = y * (1.0 + sc) + sh
    hi = h.astype(BF16)
    lo = (h - hi.astype(F32)).astype(BF16)
    tok_ref[...] = hi
    log_ref[...] = (jnp.dot(hi, whi_ref[...], preferred_element_type=F32)
                    + jnp.dot(lo, whi_ref[...], preferred_element_type=F32)
                    + jnp.dot(hi, wlo_ref[...], preferred_element_type=F32))


def _norm2_router(z, seq, tc, g, sh_l, sc_l, sh_c, sc_c, w_router, latent_only):
    n, d = z.shape
    bsz = n // seq
    tb = _pick_tile(math.gcd(seq, tc), 256, 16)
    nblk, cblk = seq // tb, tc // tb
    skip = cblk if latent_only else 0
    nb = nblk - skip
    w_pad = jnp.pad(w_router, ((0, 0), (0, ROUTER_PAD - w_router.shape[1])))
    w_hi = w_pad.astype(BF16)
    w_lo = (w_pad - w_hi.astype(F32)).astype(BF16)
    vec = pl.BlockSpec((1, d), lambda b, i: (0, 0))
    per_b = pl.BlockSpec((1, 1, d), lambda b, i: (b, 0, 0))
    wspec = pl.BlockSpec((d, ROUTER_PAD), lambda b, i: (0, 0))
    return pl.pallas_call(
        functools.partial(_norm2_router_body, skip=skip, cblk=cblk),
        grid=(bsz, nb),
        in_specs=[pl.BlockSpec((tb, d), lambda b, i: (b * nblk + skip + i, 0)), vec, per_b, per_b, vec, vec,
                  wspec, wspec],
        out_specs=[pl.BlockSpec((tb, d), lambda b, i: (b * nb + i, 0)),
                   pl.BlockSpec((tb, ROUTER_PAD), lambda b, i: (b * nb + i, 0))],
        out_shape=[jax.ShapeDtypeStruct((bsz * nb * tb, d), BF16),
                   jax.ShapeDtypeStruct((bsz * nb * tb, ROUTER_PAD), F32)],
        compiler_params=pltpu.CompilerParams(dimension_semantics=("parallel", "parallel"),
                                             vmem_limit_bytes=VMEM_LIMIT),
    )(z, g, sh_l, sc_l, sh_c, sc_c, w_hi, w_lo)


def _moe_body(be_ref, nu_ref, x_ref, w1_ref, w3_ref, w2_ref, o_ref, w1b, w3b, w2b):
    i = pl.program_id(0)

    @pl.when((i == 0) | (be_ref[i] != be_ref[jnp.maximum(i - 1, 0)]))
    def _():
        w1b[...] = w1_ref[0].astype(BF16)
        w3b[...] = w3_ref[0].astype(BF16)
        w2b[...] = w2_ref[0].astype(BF16)

    @pl.when(i < nu_ref[0])
    def _():
        xb = x_ref[...]
        h1 = jnp.dot(xb, w1b[...], preferred_element_type=F32)
        h3 = jnp.dot(xb, w3b[...], preferred_element_type=F32)
        hid = (h1 * jax.nn.sigmoid(h1) * h3).astype(BF16)
        o_ref[...] = jnp.dot(hid, w2b[...], preferred_element_type=F32).astype(o_ref.dtype)

    @pl.when(i >= nu_ref[0])
    def _():
        o_ref[...] = jnp.zeros_like(o_ref)


def _moe_experts(xb, block_e, n_used, w1, w3, w2, layer):
    n_rows, d = xb.shape
    n_blocks = n_rows // MOE_BM
    de = w1.shape[3]
    grid_spec = pltpu.PrefetchScalarGridSpec(
        num_scalar_prefetch=2,
        grid=(n_blocks,),
        in_specs=[
            pl.BlockSpec((MOE_BM, d), lambda i, be, nu: (i, 0)),
            pl.BlockSpec((None, 1, d, de), lambda i, be, nu: (layer, be[i], 0, 0)),
            pl.BlockSpec((None, 1, d, de), lambda i, be, nu: (layer, be[i], 0, 0)),
            pl.BlockSpec((None, 1, de, d), lambda i, be, nu: (layer, be[i], 0, 0)),
        ],
        out_specs=pl.BlockSpec((MOE_BM, d), lambda i, be, nu: (i, 0)),
        scratch_shapes=[pltpu.VMEM((d, de), BF16), pltpu.VMEM((d, de), BF16), pltpu.VMEM((de, d), BF16)],
    )
    return pl.pallas_call(
        _moe_body,
        grid_spec=grid_spec,
        out_shape=jax.ShapeDtypeStruct((n_rows, d), BF16),
        compiler_params=pltpu.CompilerParams(dimension_semantics=("arbitrary",),
                                             vmem_limit_bytes=VMEM_LIMIT),
    )(block_e, n_used, xb, w1, w3, w2)


def _moe(tokens, logits, b_router, w1, w3, w2, layer):
    n_tok = tokens.shape[0]
    aff = jax.nn.sigmoid(logits[:, :N_EXPERTS])
    assert TOP_K == 2
    sel = (aff + b_router).reshape(-1, N_GROUPS, EXPERTS_PER_GROUP)

    def top2(v):
        lane = jnp.arange(v.shape[-1], dtype=jnp.int32)
        i1 = jnp.argmax(v, axis=-1).astype(jnp.int32)
        rest = jnp.where(lane == i1[..., None], -jnp.inf, v)
        i2 = jnp.argmax(rest, axis=-1).astype(jnp.int32)
        return jnp.max(v, axis=-1), jnp.max(rest, axis=-1), i1, i2

    g1, g2, _, _ = top2(sel)
    grp = jnp.argmax(g1 + g2, axis=-1)
    grp_mask = jnp.arange(N_GROUPS)[None, :] == grp[:, None]
    masked = jnp.where(grp_mask[:, :, None], sel, -jnp.inf).reshape(-1, N_EXPERTS)
    _, _, e1, e2 = top2(masked)
    e_idx = jnp.stack([e1, e2], axis=-1)
    wts = jnp.take_along_axis(aff, e_idx, axis=1)
    wts = wts / jnp.sum(wts, axis=-1, keepdims=True)

    n_asg = n_tok * TOP_K
    flat_e = e_idx.reshape(-1).astype(jnp.int32)
    order = jnp.argsort(flat_e).astype(jnp.int32)
    se, st = flat_e[order], order // TOP_K
    counts = jnp.sum(flat_e[:, None] == jnp.arange(N_EXPERTS, dtype=jnp.int32)[None, :], axis=0, dtype=jnp.int32)
    starts = jnp.cumsum(counts) - counts
    padded = (counts + MOE_BM - 1) // MOE_BM * MOE_BM
    pad_end = jnp.cumsum(padded)
    pad_start = pad_end - padded
    dest = pad_start[se] + jnp.arange(n_asg, dtype=jnp.int32) - starts[se]
    dest_asg = dest[jnp.argsort(order)].reshape(n_tok, TOP_K)
    n_blocks = -(-n_asg // MOE_BM) + N_EXPERTS
    block_start = jnp.arange(n_blocks, dtype=jnp.int32) * MOE_BM
    block_e = jnp.minimum(jnp.sum(pad_end[None, :] <= block_start[:, None], axis=1, dtype=jnp.int32),
                          N_EXPERTS - 1)
    n_used = (pad_end[-1] // MOE_BM).astype(jnp.int32).reshape(1)
    slot_e = jnp.repeat(block_e, MOE_BM)
    rank = jnp.arange(n_blocks * MOE_BM, dtype=jnp.int32) - pad_start[slot_e]
    slot_tok = st[jnp.clip(starts[slot_e] + rank, 0, n_asg - 1)]
    xb = tokens[slot_tok]
    yb = _moe_experts(xb, block_e, n_used, w1, w3, w2, layer)
    return (wts[:, 0:1] * yb[dest_asg[:, 0]].astype(F32) + wts[:, 1:2] * yb[dest_asg[:, 1]].astype(F32))


def _merge_out(z, merge, branches, b_merge, w_out_l, g1, cg1, seq, tc, d):
    tm2 = _pick_tile(seq, 576, 16)
    bpb = seq // tm2

    def seg_select(i, ctx_val, lat_val):
        row = (i % bpb) * tm2 + lax.broadcasted_iota(jnp.int32, (tm2, 1), 0)
        return jnp.where(row < tc, ctx_val, lat_val)

    def merge_pro(i, mg, b0, b1, b2, b3, bm):
        gates = jax.nn.sigmoid(mg.astype(F32) + bm)
        b0, b1, b2, b3 = [b.astype(F32) for b in (b0, b1, b2, b3)]
        return (gates[:, 0 * d:1 * d] * b0 + gates[:, 1 * d:2 * d] * b1
                + gates[:, 2 * d:3 * d] * b2 + gates[:, 3 * d:4 * d] * b3)

    def resid_epi(i, acc, zb, gl, gc):
        return zb + seg_select(i, gc, gl[0]) * acc

    tn = d // 2
    return _fused_mm([merge] + branches, [_whole(b_merge[None, :])],
                     w_out_l.astype(BF16), merge_pro, tm=tm2, tn=tn,
                     epilogue=resid_epi, epi_rows=[z],
                     epi_aux=[(g1, (1, 1, tn), lambda i, j: (i // bpb, 0, j)),
                              (cg1, (1, tn), lambda i, j: (0, j))])


def kernel(x, c, ctx, c_ctx, w_ada, b_ada, norm1_g, norm2_g, w_in, b_merge, rwkv_mu_prev, rwkv_mu_next, rwkv_w0, rwkv_w2, rwkv_a0, rwkv_a2, rwkv_g2, rwkv_k_k, rwkv_k_a, rwkv_r_k, rwkv_gn_g, rwkv_gn_b, rwkv_proj, conv_w, conv_b, conv_ln_g, conv_ln_b, conv_proj, lru_conv_w, lru_conv_b, lru_wr, lru_br, lru_wi, lru_bi, lru_lambda, lru_proj, ret_gn_g, ret_gn_b, ret_proj, w_out, w_router, b_router, e_w1, e_w3, e_w2, final_g):
    bsz, n_tok, d = x.shape
    tc = ctx.shape[1]
    seq = tc + n_tok
    depth = w_in.shape[0]
    per_layer = {
        'rwkv_w0': rwkv_w0, 'rwkv_w2': rwkv_w2, 'rwkv_a0': rwkv_a0, 'rwkv_a2': rwkv_a2, 'rwkv_g2': rwkv_g2,
        'rwkv_k_k': rwkv_k_k, 'rwkv_k_a': rwkv_k_a, 'rwkv_r_k': rwkv_r_k,
        'rwkv_gn_g': rwkv_gn_g, 'rwkv_gn_b': rwkv_gn_b,
        'conv_w': conv_w, 'conv_b': conv_b, 'conv_ln_g': conv_ln_g, 'conv_ln_b': conv_ln_b,
        'lru_conv_w': lru_conv_w, 'lru_conv_b': lru_conv_b, 'lru_wr': lru_wr, 'lru_br': lru_br,
        'lru_wi': lru_wi, 'lru_bi': lru_bi, 'lru_lambda': lru_lambda,
        'ret_gn_g': ret_gn_g, 'ret_gn_b': ret_gn_b,
    }
    tm = _pick_tile(seq, 576)

    z = jnp.concatenate([ctx, x], axis=1).reshape(bsz * seq, d)
    for layer in range(depth):
        lp = {name: arr[layer] for name, arr in per_layer.items()}
        last = layer == depth - 1
        cc = jnp.concatenate([c, c_ctx[None, :]], axis=0)
        n_mod = -(-cc.shape[0] // 8) * 8
        cc = jnp.pad(cc, ((0, n_mod - cc.shape[0]), (0, 0)))
        mod = _fused_mm([cc], [], w_ada[layer].astype(BF16), lambda i, cb: cb * jax.nn.sigmoid(cb),
                        tm=n_mod, tn=6 * d // 4,
                        epilogue=lambda i, acc, bb: acc + bb,
                        epi_aux=[(b_ada[layer][None, :], (1, 6 * d // 4), lambda i, j: (0, j))])
        mod_l = mod[:bsz].reshape(bsz, 1, 6 * d)
        mod_c = mod[bsz:bsz + 1]
        sh1, sc1, g1, sh2, sc2, g2 = [mod_l[:, :, n * d:(n + 1) * d] for n in range(6)]
        csh1, csc1, cg1, csh2, csc2, cg2 = [mod_c[:, n * d:(n + 1) * d] for n in range(6)]

        w_l = w_in[layer].astype(BF16)
        h1 = _norm_mod(z, seq, tc, norm1_g[layer][None, :], sh1, sc1, csh1, csc1)
        p_rwkv = _in_proj(h1, seq, tc, w_l[:, :RWKV_WIDTH], 640,
                          rwkv_mu_prev[layer][None, :], rwkv_mu_next[layer][None, :], out_dtype=BF16)
        p_mid = _in_proj(h1, seq, tc, w_l[:, RWKV_WIDTH:RWKV_WIDTH + MID_WIDTH], 896, out_dtype=BF16)
        p_merge = _in_proj(h1, seq, tc, w_l[:, RWKV_WIDTH + MID_WIDTH:], 1024, out_dtype=BF16)

        y_f, y_b = _rwkv_mix(p_rwkv, seq, tc, lp)
        o_f, o_b = _ret_mix(p_mid, seq, tc, n_tok)
        branches = [
            _rwkv_out_proj(y_f, y_b, p_rwkv, lp, rwkv_proj[layer], tm),
            _plain_mm(_conv_mix(p_mid, seq, tc, lp), conv_proj[layer].astype(BF16), tm=tm, tn=d, out_dtype=BF16),
            _plain_mm(_lru_mix(p_mid, seq, tc, lp), lru_proj[layer].astype(BF16), tm=tm, tn=d, out_dtype=BF16),
            _ret_out_proj(o_f, o_b, p_mid, lp, ret_proj[layer], tm),
        ]
        z = _merge_out(z, p_merge, branches, b_merge[layer], w_out[layer], g1, cg1, seq, tc, d)

        z3 = z.reshape(bsz, seq, d)
        tok, logits = _norm2_router(z, seq, tc, norm2_g[layer][None, :], sh2, sc2, csh2, csc2, w_router,
                                    latent_only=last)
        y = _moe(tok, logits, b_router, e_w1, e_w3, e_w2, layer)
        if not last:
            is_ctx = (jnp.arange(seq) < tc)[None, :, None]
            gate2 = jnp.where(is_ctx, cg2[None], g2)
            z = (z3 + gate2 * y.reshape(bsz, seq, d)).reshape(bsz * seq, d)
        else:
            xl = z3[:, tc:] + g2 * y.reshape(bsz, n_tok, d)
            return xl * lax.rsqrt(jnp.mean(xl * xl, axis=-1, keepdims=True) + RMS_EPS) * final_g
    return None
```

```python
import functools
import math

import jax
import jax.numpy as jnp
from jax import lax
from jax.experimental import pallas as pl
from jax.experimental.pallas import tpu as pltpu

F32 = jnp.float32
BF16 = jnp.bfloat16

D_MODEL = 1024
GRID_W = 64
N_BRANCH = 4
BR_W = D_MODEL // 2
RWKV_HS = 64
RWKV_H = BR_W // RWKV_HS
RWKV_W_LORA = 64
RWKV_A_LORA = 64
RWKV_G_LORA = 128
RWKV_GN_EPS = 64e-5
L2_EPS = 1e-12
CONV_K = 31
LRU_BLOCKS = 8
LRU_BS = BR_W // LRU_BLOCKS
LRU_CONV = 4
LRU_C = 8.0
RET_H = 4
RET_DK = 64
RET_DV = BR_W // RET_H
RET_CHUNK = 128
ROPE_BASE = 10000.0
N_EXPERTS = 32
N_GROUPS = 4
EXPERTS_PER_GROUP = N_EXPERTS // N_GROUPS
TOP_K = 2
D_EXPERT = D_MODEL // 2
RMS_EPS = 1e-6
LN_EPS = 1e-5

REST_COLS = (('conv_val', BR_W), ('conv_gate', BR_W), ('lru_x', BR_W), ('lru_gate', BR_W),
             ('ret_q', RET_H * RET_DK), ('ret_k', RET_H * RET_DK), ('ret_v', BR_W), ('ret_g', BR_W))
RWKV_WIDTH = 3 * BR_W + 2 * RWKV_W_LORA + 2 * RWKV_A_LORA + RWKV_G_LORA
LORA_OFF = 3 * BR_W
LORA_W = 2 * RWKV_W_LORA + 2 * RWKV_A_LORA
MID_WIDTH = sum(w for _, w in REST_COLS)
MERGE_WIDTH = N_BRANCH * D_MODEL

VMEM_LIMIT = 48 * 1024 * 1024
RWKV_CHUNK = 64
RWKV_TB = 256
MOE_BM = 512


def _bdot(a, b):
    return jnp.dot(a.astype(BF16), b.astype(BF16), preferred_element_type=F32)


_NT = (((1,), (1,)), ((), ()))
_TN = (((0,), (0,)), ((), ()))


def _pick_tile(n, cap, mult=8):
    best = None
    for t in range(mult, cap + 1, mult):
        if n % t == 0:
            best = t
    assert best is not None, (n, cap, mult)
    return best


def _fused_mm_body(*refs, nr, na, ner, nea, prologue, epilogue, out_dtype):
    rows = refs[:nr]
    auxs = refs[nr:nr + na]
    w_ref = refs[nr + na]
    er = refs[nr + na + 1:nr + na + 1 + ner]
    ea = refs[nr + na + 1 + ner:nr + na + 1 + ner + nea]
    o_ref, act = refs[-2], refs[-1]
    i = pl.program_id(0)

    @pl.when(pl.program_id(1) == 0)
    def _():
        act[...] = prologue(i, *[r[...] for r in rows], *[a[...] for a in auxs]).astype(BF16)

    acc = jnp.dot(act[...], w_ref[...], preferred_element_type=F32)
    if epilogue is not None:
        acc = epilogue(i, acc, *[r[...] for r in er], *[a[...] for a in ea])
    o_ref[...] = acc.astype(out_dtype)


def _fused_mm(row_ins, aux_ins, w, prologue, *, tm, tn, out_dtype=F32, epilogue=None,
              epi_rows=(), epi_aux=()):
    row_ins = [r if isinstance(r, tuple) else (r, r.shape[1], 0) for r in row_ins]
    n = row_ins[0][0].shape[0]
    k, m = w.shape
    assert n % tm == 0 and m % tn == 0, (n, tm, m, tn)
    in_specs = [pl.BlockSpec((tm, bw), functools.partial(lambda i, j, cb: (i, cb), cb=cb))
                for (_, bw, cb) in row_ins]
    in_specs += [pl.BlockSpec(bs, im) for (_, bs, im) in aux_ins]
    in_specs += [pl.BlockSpec((k, tn), lambda i, j: (0, j))]
    in_specs += [pl.BlockSpec((tm, tn), lambda i, j: (i, j)) for _ in epi_rows]
    in_specs += [pl.BlockSpec(bs, im) for (_, bs, im) in epi_aux]
    body = functools.partial(_fused_mm_body, nr=len(row_ins), na=len(aux_ins), ner=len(epi_rows),
                             nea=len(epi_aux), prologue=prologue, epilogue=epilogue, out_dtype=out_dtype)
    return pl.pallas_call(
        body,
        grid=(n // tm, m // tn),
        in_specs=in_specs,
        out_specs=pl.BlockSpec((tm, tn), lambda i, j: (i, j)),
        out_shape=jax.ShapeDtypeStruct((n, m), out_dtype),
        scratch_shapes=[pltpu.VMEM((tm, k), BF16)],
        compiler_params=pltpu.CompilerParams(dimension_semantics=("parallel", "arbitrary"),
                                             vmem_limit_bytes=VMEM_LIMIT),
    )(*[a for (a, _, _) in row_ins], *[a for (a, _, _) in aux_ins], w, *epi_rows,
      *[a for (a, _, _) in epi_aux])


def _whole(a):
    nd = a.ndim
    return (a, a.shape, lambda i, j: (0,) * nd)


def _plain_mm(x, w, *, tm, tn, out_dtype=F32):
    return _fused_mm([x], [], w, lambda i, xb: xb, tm=tm, tn=tn, out_dtype=out_dtype)


def _in_proj_body(*refs, tc, shift):
    if shift:
        z_ref, g_ref, shl_ref, scl_ref, shc_ref, scc_ref, w_ref, mup_ref, mun_ref, o_ref, act = refs
    else:
        z_ref, g_ref, shl_ref, scl_ref, shc_ref, scc_ref, w_ref, o_ref, act = refs
    seq = z_ref.shape[0]
    row = lax.broadcasted_iota(jnp.int32, (seq, 1), 0)

    @pl.when(pl.program_id(1) == 0)
    def _():
        xb = z_ref[...]
        y = xb * lax.rsqrt(jnp.mean(xb * xb, axis=-1, keepdims=True) + RMS_EPS) * g_ref[...]
        is_ctx = row < tc
        sc = jnp.where(is_ctx, scc_ref[...], scl_ref[0])
        sh = jnp.where(is_ctx, shc_ref[...], shl_ref[0])
        act[...] = (y * (1.0 + sc) + sh).astype(BF16)

    acc = jnp.dot(act[...], w_ref[...], preferred_element_type=F32)
    if shift:
        prev = pltpu.roll(acc, 1, 0)
        nxt = pltpu.roll(acc, seq - 1, 0)
        prev = jnp.where((row == 0) | (row == tc), 0.0, prev)
        nxt = jnp.where((row == tc - 1) | (row == seq - 1), 0.0, nxt)
        acc = acc + mup_ref[...] * (prev - acc) + mun_ref[...] * (nxt - acc)
    o_ref[...] = acc.astype(o_ref.dtype)


def _in_proj(z, seq, tc, g, sh_l, sc_l, sh_c, sc_c, w, tn, mu_prev=None, mu_next=None, out_dtype=F32):
    n, d = z.shape
    m = w.shape[1]
    shift = mu_prev is not None
    assert m % tn == 0
    vec = lambda width: pl.BlockSpec((1, width), lambda b, j: (0, 0))
    per_b = pl.BlockSpec((1, 1, d), lambda b, j: (b, 0, 0))
    in_specs = [pl.BlockSpec((seq, d), lambda b, j: (b, 0)), vec(d), per_b, per_b, vec(d), vec(d),
                pl.BlockSpec((d, tn), lambda b, j: (0, j))]
    args = [z, g, sh_l, sc_l, sh_c, sc_c, w]
    if shift:
        in_specs += [pl.BlockSpec((1, tn), lambda b, j: (0, j))] * 2
        args += [mu_prev, mu_next]
    return pl.pallas_call(
        functools.partial(_in_proj_body, tc=tc, shift=shift),
        grid=(n // seq, m // tn),
        in_specs=in_specs,
        out_specs=pl.BlockSpec((seq, tn), lambda b, j: (b, j)),
        out_shape=jax.ShapeDtypeStruct((n, m), out_dtype),
        scratch_shapes=[pltpu.VMEM((seq, d), BF16)],
        compiler_params=pltpu.CompilerParams(dimension_semantics=("parallel", "arbitrary"),
                                             vmem_limit_bytes=VMEM_LIMIT),
    )(*args)


def _rwkv_body(rf, kf, vf, lf, rb_, kb, vb_, lb, w0_ref, w2_ref, a0_ref, a2_ref, kkw_ref, ka_ref, bd_ref,
               of_ref, ob_ref, s_ref, kk_ref, *, n_chunks):
    c_len = RWKV_CHUNK

    @pl.when(pl.program_id(1) == 0)
    def _():
        s_ref[...] = jnp.zeros_like(s_ref)

    for d, k_blk in enumerate((kf, kb)):
        kx = k_blk[...].astype(F32) * kkw_ref[...]
        kk_ref[d] = kx * lax.rsqrt(_bdot(kx * kx, bd_ref[...]) + L2_EPS)

    ri = lax.broadcasted_iota(jnp.int32, (c_len, c_len), 0)
    ci = lax.broadcasted_iota(jnp.int32, (c_len, c_len), 1)
    eye = (ri == ci).astype(F32)
    in_refs = ((rf, kf, vf, lf), (rb_, kb, vb_, lb))
    masks = []
    for d in range(2):
        before = (ri > ci) if d == 0 else (ri < ci)
        strict = before.astype(F32)
        incl = (before | (ri == ci)).astype(F32)
        m8 = jnp.where(((ri // 8) == (ci // 8)) & before, 1.0, 0.0).astype(F32)
        merges = []
        for size in (16, 32, 64):
            same = (ri // size) == (ci // size)
            inner = (ri // (size // 2)) == (ci // (size // 2))
            merges.append(jnp.where(same & jnp.logical_not(inner) & before, 1.0, 0.0).astype(F32))
        masks.append(dict(strict=strict, incl=incl, incl_b=incl.astype(BF16), m8=m8, merges=merges))
    out_refs = (of_ref, ob_ref)
    decay_scale = math.exp(-0.5)

    def chunk(c, carry):
        per_dir = []
        for d in range(2):
            cc = c if d == 0 else n_chunks - 1 - c
            sl = pl.ds(pl.multiple_of(cc * c_len, c_len), c_len)
            r_ref, k_ref, v_ref, lo_ref = in_refs[d]
            r = r_ref[sl, :].astype(F32)
            k = k_ref[sl, :].astype(F32)
            v = v_ref[sl, :].astype(F32)
            lo = lo_ref[sl, :].astype(F32)
            wlo = lo[:, d * RWKV_W_LORA:(d + 1) * RWKV_W_LORA]
            alo = lo[:, 2 * RWKV_W_LORA + d * RWKV_A_LORA:2 * RWKV_W_LORA + (d + 1) * RWKV_A_LORA]
            w_raw = w0_ref[d] + _bdot(jnp.tanh(wlo), w2_ref[d])
            lw = -decay_scale * jax.nn.sigmoid(w_raw)
            a = jax.nn.sigmoid(a0_ref[d] + _bdot(alo, a2_ref[d]))
            k_d = k * (1.0 + (a - 1.0) * ka_ref[...])
            kk = kk_ref[d, sl, :]
            hi = lw.astype(BF16)
            lo2 = (lw - hi.astype(F32)).astype(BF16)
            tri = masks[d]['incl_b']
            cum = (jnp.dot(tri, hi, preferred_element_type=F32)
                   + jnp.dot(tri, lo2, preferred_element_type=F32))
            eg = jnp.exp(cum)
            ieg = jnp.exp(-cum)
            last = c_len - 1 if d == 0 else 0
            per_dir.append(dict(
                rt=(r * eg).astype(BF16), kt=(k_d * ieg).astype(BF16), at=(-(kk * a) * ieg).astype(BF16),
                bt=(kk * jnp.exp(cum - lw)).astype(BF16), vb=v.astype(BF16), g_last=eg[last:last + 1, :],
                rows=sl))
        chains = [(d, h) for d in range(2) for h in range(RWKV_H)]
        hsl = lambda h: slice(h * RWKV_HS, (h + 1) * RWKV_HS)
        rb = [jnp.concatenate([per_dir[d]['rt'][:, hsl(h)], per_dir[d]['bt'][:, hsl(h)]], axis=0)
              for d, h in chains]
        at_h = [per_dir[d]['at'][:, hsl(h)] for d, h in chains]
        kt_h = [per_dir[d]['kt'][:, hsl(h)] for d, h in chains]
        v_h = [per_dir[d]['vb'][:, hsl(h)] for d, h in chains]
        mk = [masks[d] for d, h in chains]
        n = len(chains)
        m_a = [lax.dot_general(rb[i], at_h[i], _NT, preferred_element_type=F32) for i in range(n)]
        m_k = [lax.dot_general(rb[i], kt_h[i], _NT, preferred_element_type=F32) for i in range(n)]
        ra = [m_a[i][:c_len] * mk[i]['incl'] for i in range(n)]
        nmat = [m_a[i][c_len:] * mk[i]['strict'] for i in range(n)]
        rk = [m_k[i][:c_len] * mk[i]['incl'] for i in range(n)]
        bk = [m_k[i][c_len:] * mk[i]['strict'] for i in range(n)]
        d1 = [nmat[i] * mk[i]['m8'] for i in range(n)]
        d2 = [_bdot(d1[i], d1[i]) for i in range(n)]
        d4 = [_bdot(d2[i], d2[i]) for i in range(n)]
        tinv = [eye + d1[i] for i in range(n)]
        tinv = [tinv[i] + _bdot(tinv[i], d2[i]) for i in range(n)]
        tinv = [tinv[i] + _bdot(tinv[i], d4[i]) for i in range(n)]
        for lvl in range(3):
            e = [_bdot(tinv[i], nmat[i] * mk[i]['merges'][lvl]) for i in range(n)]
            tinv = [tinv[i] + _bdot(e[i], tinv[i]) for i in range(n)]
        s0 = [s_ref[d, h] for d, h in chains]
        rbs = [lax.dot_general(rb[i], s0[i].astype(BF16), _NT, preferred_element_type=F32) for i in range(n)]
        x = [rbs[i][c_len:] + _bdot(bk[i], v_h[i]) for i in range(n)]
        u = [_bdot(tinv[i], x[i]) for i in range(n)]
        y = [rbs[i][:c_len] + _bdot(ra[i], u[i]) + _bdot(rk[i], v_h[i]) for i in range(n)]
        for i, (d, h) in enumerate(chains):
            uv = jnp.concatenate([u[i].astype(BF16), v_h[i]], axis=0)
            ak = jnp.concatenate([at_h[i], kt_h[i]], axis=0)
            ds = lax.dot_general(uv, ak, _TN, preferred_element_type=F32)
            s_ref[d, h] = (s0[i] + ds) * per_dir[d]['g_last'][:, hsl(h)]
        for d in range(2):
            out_refs[d][per_dir[d]['rows'], :] = jnp.concatenate(
                [y[d * RWKV_H + h] for h in range(RWKV_H)], axis=1)
        return carry

    lax.fori_loop(0, n_chunks, chunk, 0)


def _rwkv_mix(p_rwkv, seq, tc, lp):
    n = p_rwkv.shape[0]
    bsz = n // seq
    tb = _pick_tile(math.gcd(seq, tc), RWKV_TB, RWKV_CHUNK)
    nblk, cblk = seq // tb, tc // tb
    assert LORA_OFF % LORA_W == 0

    def fwd_blk(t):
        return t

    def bwd_blk(t):
        return jnp.where(t < cblk, cblk - 1 - t, nblk - 1 - (t - cblk))

    def col_spec(width, cb, blk_fn):
        return pl.BlockSpec((tb, width), lambda b, t: (b * nblk + blk_fn(t), cb))

    in_specs = []
    for blk_fn in (fwd_blk, bwd_blk):
        in_specs += [col_spec(BR_W, 0, blk_fn), col_spec(BR_W, 1, blk_fn), col_spec(BR_W, 2, blk_fn),
                     col_spec(LORA_W, LORA_OFF // LORA_W, blk_fn)]
    whole = lambda a: pl.BlockSpec(a.shape, lambda b, t: (0,) * a.ndim)
    head_id = jnp.arange(BR_W) // RWKV_HS
    bd = (head_id[:, None] == head_id[None, :]).astype(BF16)
    consts = [lp['rwkv_w0'][:, None, :], lp['rwkv_w2'].astype(BF16), lp['rwkv_a0'][:, None, :],
              lp['rwkv_a2'].astype(BF16), lp['rwkv_k_k'][None, :], lp['rwkv_k_a'][None, :], bd]
    in_specs += [whole(a) for a in consts]
    return pl.pallas_call(
        functools.partial(_rwkv_body, n_chunks=tb // RWKV_CHUNK),
        grid=(bsz, nblk),
        in_specs=in_specs,
        out_specs=[col_spec(BR_W, 0, fwd_blk), col_spec(BR_W, 0, bwd_blk)],
        out_shape=[jax.ShapeDtypeStruct((n, BR_W), F32)] * 2,
        scratch_shapes=[pltpu.VMEM((2, RWKV_H, RWKV_HS, RWKV_HS), F32), pltpu.VMEM((2, tb, BR_W), F32)],
        compiler_params=pltpu.CompilerParams(
            dimension_semantics=("parallel", "arbitrary"), vmem_limit_bytes=VMEM_LIMIT),
    )(*([p_rwkv] * 8), *consts)


def _rwkv_out_proj(y_f, y_b, p_rwkv, lp, w_proj, tm):
    head_id = jnp.arange(BR_W) // RWKV_HS
    bd = (head_id[:, None] == head_id[None, :]).astype(BF16)

    def hsum(xv, bdm):
        hi = xv.astype(BF16)
        lo = (xv - hi.astype(F32)).astype(BF16)
        return (jnp.dot(hi, bdm, preferred_element_type=F32) + jnp.dot(lo, bdm, preferred_element_type=F32))

    def act(i, yf, yb, r, k, v, glo, gn_g, gn_b, r_k, g2, bdm):
        inv = 1.0 / RWKV_HS
        r, k, v, glo = [t.astype(F32) for t in (r, k, v, glo)]
        yb = yf + yb
        yc = yb - hsum(yb, bdm) * inv
        var = hsum(yc * yc, bdm) * inv
        yn = yc * lax.rsqrt(var + RWKV_GN_EPS) * gn_g + gn_b
        bonus = hsum(r * k * r_k, bdm) * v
        g = _bdot(jax.nn.sigmoid(glo), g2)
        return (yn + bonus) * g

    glo_cb = (LORA_OFF + LORA_W) // RWKV_G_LORA
    assert (LORA_OFF + LORA_W) % RWKV_G_LORA == 0
    return _fused_mm(
        [y_f, y_b, (p_rwkv, BR_W, 0), (p_rwkv, BR_W, 1), (p_rwkv, BR_W, 2), (p_rwkv, RWKV_G_LORA, glo_cb)],
        [_whole(lp['rwkv_gn_g'][None, :]), _whole(lp['rwkv_gn_b'][None, :]), _whole(lp['rwkv_r_k'][None, :]),
         _whole(lp['rwkv_g2'].astype(BF16)), _whole(bd)],
        w_proj.astype(BF16), act, tm=tm, tn=w_proj.shape[1], out_dtype=BF16)


LRU_TB = 256
LRU_HALO = 16


def _lru_body(x_ref, g_ref, cw_ref, cb_ref, wr_ref, br_ref, wi_ref, bi_ref, sp_ref, o_ref, h_ref, *,
              tb, nblk, cblk):
    seq = x_ref.shape[0]
    row = lax.broadcasted_iota(jnp.int32, (tb, 1), 0)
    n_ext = tb + LRU_HALO
    for d in range(2):
        h_ref[...] = jnp.zeros_like(h_ref)

        def blk_body(i, carry, d=d):
            if d == 0:
                blk = i
                halo_ok = (blk != 0) & (blk != cblk)
                halo0 = jnp.maximum(blk * tb - LRU_HALO, 0)
            else:
                blk = jnp.where(i < cblk, cblk - 1 - i, nblk - 1 - (i - cblk))
                halo_ok = (blk != cblk - 1) & (blk != nblk - 1)
                halo0 = jnp.minimum(blk * tb + tb, seq - LRU_HALO)
            rows = pl.ds(pl.multiple_of(blk * tb, tb), tb)
            cur = x_ref[rows, :].astype(F32)
            halo = (x_ref[pl.ds(pl.multiple_of(halo0, LRU_HALO), LRU_HALO), :].astype(F32)
                    * jnp.where(halo_ok, 1.0, 0.0))
            xc = cb_ref[d]
            if d == 0:
                ext = jnp.concatenate([halo, cur], axis=0)
                for j in range(LRU_CONV):
                    sh = LRU_CONV - 1 - j
                    tap = ext if sh == 0 else pltpu.roll(ext, sh, 0)
                    xc = xc + cw_ref[d, j:j + 1, :] * tap[LRU_HALO:, :]
            else:
                ext = jnp.concatenate([cur, halo], axis=0)
                for j in range(LRU_CONV):
                    tap = ext if j == 0 else pltpu.roll(ext, n_ext - j, 0)
                    xc = xc + cw_ref[d, j:j + 1, :] * tap[:tb, :]
            gate_r = jax.nn.sigmoid(_bdot(xc, wr_ref[d]) + br_ref[d])
            gate_i = jax.nn.sigmoid(_bdot(xc, wi_ref[d]) + bi_ref[d])
            log_a = -LRU_C * gate_r * sp_ref[d]
            a_cum = jnp.exp(log_a)
            b_cum = jnp.sqrt(1.0 - jnp.exp(2.0 * log_a)) * (gate_i * xc)
            s = 1
            while s < tb:
                if d == 0:
                    ok = row >= s
                    a_sh = jnp.where(ok, pltpu.roll(a_cum, s, 0), 1.0)
                    b_sh = jnp.where(ok, pltpu.roll(b_cum, s, 0), 0.0)
                else:
                    ok = row < tb - s
                    a_sh = jnp.where(ok, pltpu.roll(a_cum, tb - s, 0), 1.0)
                    b_sh = jnp.where(ok, pltpu.roll(b_cum, tb - s, 0), 0.0)
                b_cum = a_cum * b_sh + b_cum
                a_cum = a_cum * a_sh
                s *= 2
            h = b_cum + a_cum * h_ref[...]
            if d == 0:
                h_ref[...] = h[tb - 1:tb, :]
                o_ref[rows, :] = h
            else:
                h_ref[...] = h[0:1, :]
                o_ref[rows, :] = (o_ref[rows, :] + h) * jax.nn.gelu(g_ref[rows, :].astype(F32))
            return carry

        lax.fori_loop(0, nblk, blk_body, 0)


def _lru_mix(p_mid, seq, tc, lp):
    n = p_mid.shape[0]
    tb = _pick_tile(math.gcd(seq, tc), LRU_TB)
    nblk, cblk = seq // tb, tc // tb
    eye = jnp.eye(LRU_BLOCKS, dtype=F32)
    blockdiag = lambda w: jnp.einsum('dgij,gh->dgihj', w, eye).reshape(2, BR_W, BR_W).astype(BF16)
    consts = [lp['lru_conv_w'], lp['lru_conv_b'][:, None, :], blockdiag(lp['lru_wr']), lp['lru_br'][:, None, :],
              blockdiag(lp['lru_wi']), lp['lru_bi'][:, None, :], jax.nn.softplus(-lp['lru_lambda'])[:, None, :]]
    whole = lambda a: pl.BlockSpec(a.shape, lambda b: (0,) * a.ndim)
    x_cb = [name for name, _ in REST_COLS].index('lru_x')
    g_cb = [name for name, _ in REST_COLS].index('lru_gate')
    assert all(w == BR_W for _, w in REST_COLS[:max(x_cb, g_cb) + 1])
    return pl.pallas_call(
        functools.partial(_lru_body, tb=tb, nblk=nblk, cblk=cblk),
        grid=(n // seq,),
        in_specs=[pl.BlockSpec((seq, BR_W), lambda b: (b, x_cb)), pl.BlockSpec((seq, BR_W), lambda b: (b, g_cb))]
        + [whole(a) for a in consts],
        out_specs=pl.BlockSpec((seq, BR_W), lambda b: (b, 0)),
        out_shape=jax.ShapeDtypeStruct((n, BR_W), F32),
        scratch_shapes=[pltpu.VMEM((1, BR_W), F32)],
        compiler_params=pltpu.CompilerParams(dimension_semantics=("parallel",), vmem_limit_bytes=VMEM_LIMIT),
    )(p_mid, p_mid, *consts)


CONV_TB = 256
CONV_HALO = 16


def _conv_body(val_ref, gate_ref, w_ref, b_ref, lng_ref, lnb_ref, o_ref, *, tb, nblk, cblk):
    seq = val_ref.shape[0]
    n_ext = tb + 2 * CONV_HALO
    pad = CONV_K // 2

    def glu(rows):
        return val_ref[rows, :].astype(F32) * jax.nn.sigmoid(gate_ref[rows, :].astype(F32))

    def blk_body(blk, carry):
        t0 = blk * tb
        rows = pl.ds(pl.multiple_of(t0, tb), tb)
        lo_ok = (blk != 0) & (blk != cblk)
        hi_ok = (blk != cblk - 1) & (blk != nblk - 1)
        lo0 = jnp.maximum(t0 - CONV_HALO, 0)
        hi0 = jnp.minimum(t0 + tb, seq - CONV_HALO)
        lo = glu(pl.ds(pl.multiple_of(lo0, CONV_HALO), CONV_HALO)) * jnp.where(lo_ok, 1.0, 0.0)
        hi = glu(pl.ds(pl.multiple_of(hi0, CONV_HALO), CONV_HALO)) * jnp.where(hi_ok, 1.0, 0.0)
        ext = jnp.concatenate([lo, glu(rows), hi], axis=0)
        acc = jnp.zeros((tb, ext.shape[1]), F32) + b_ref[...]
        for r in range(8):
            rolled = ext if r == 0 else pltpu.roll(ext, n_ext - r, 0)
            for j in range(CONV_K):
                off = CONV_HALO - pad + j
                if off % 8 == r:
                    acc = acc + w_ref[j:j + 1, :] * rolled[off - r:off - r + tb, :]
        mu = jnp.mean(acc, axis=-1, keepdims=True)
        xc = acc - mu
        yn = xc * lax.rsqrt(jnp.mean(xc * xc, axis=-1, keepdims=True) + LN_EPS) * lng_ref[...] + lnb_ref[...]
        o_ref[rows, :] = yn * jax.nn.sigmoid(yn)
        return carry

    lax.fori_loop(0, nblk, blk_body, 0)


def _conv_mix(p_mid, seq, tc, lp):
    n = p_mid.shape[0]
    tb = _pick_tile(math.gcd(seq, tc), CONV_TB, CONV_HALO)
    names = [name for name, _ in REST_COLS]
    v_cb, g_cb = names.index('conv_val'), names.index('conv_gate')
    assert all(w == BR_W for _, w in REST_COLS[:max(v_cb, g_cb) + 1]) and CONV_HALO >= CONV_K // 2
    consts = [lp['conv_w'], lp['conv_b'][None, :], lp['conv_ln_g'][None, :], lp['conv_ln_b'][None, :]]
    whole = lambda a: pl.BlockSpec(a.shape, lambda b: (0,) * a.ndim)
    return pl.pallas_call(
        functools.partial(_conv_body, tb=tb, nblk=seq // tb, cblk=tc // tb),
        grid=(n // seq,),
        in_specs=[pl.BlockSpec((seq, BR_W), lambda b: (b, v_cb)), pl.BlockSpec((seq, BR_W), lambda b: (b, g_cb))]
        + [whole(a) for a in consts],
        out_specs=pl.BlockSpec((seq, BR_W), lambda b: (b, 0)),
        out_shape=jax.ShapeDtypeStruct((n, BR_W), F32),
        compiler_params=pltpu.CompilerParams(dimension_semantics=("parallel",), vmem_limit_bytes=VMEM_LIMIT),
    )(p_mid, p_mid, *consts)


RET_TB = 256
_RET_LOG_G = [math.log1p(-2.0 ** (-5.0 - h)) for h in range(RET_H)]


def _ret_body(qf, kf, vf, cf, sf, qb, kb, vb, cb, sb, dm_ref, xi_ref, zt_ref, of_ref, ob_ref, s_ref, *, n_chunks):
    c_len = RET_CHUNK
    qk_w = RET_H * RET_DK

    @pl.when(pl.program_id(1) == 0)
    def _():
        s_ref[...] = jnp.zeros_like(s_ref)

    lane = lax.broadcasted_iota(jnp.int32, (c_len, qk_w), 1)
    first_half = (lane % RET_DK) < (RET_DK // 2)

    def rotary(z, cos, sin):
        swapped = jnp.where(first_half, pltpu.roll(z, qk_w - RET_DK // 2, 1), pltpu.roll(z, RET_DK // 2, 1))
        return z * cos + swapped * sin

    in_refs = ((qf, kf, vf, cf, sf), (qb, kb, vb, cb, sb))
    out_refs = (of_ref, ob_ref)

    def chunk(c, carry):
        per_dir = []
        for d in range(2):
            cc = c if d == 0 else n_chunks - 1 - c
            sl = pl.ds(pl.multiple_of(cc * c_len, c_len), c_len)
            q_ref, k_ref, v_ref, c_ref, sn_ref = in_refs[d]
            cos, sin = c_ref[sl, :], sn_ref[sl, :]
            q = rotary(q_ref[sl, :].astype(F32), cos, sin)
            k = rotary(k_ref[sl, :].astype(F32), cos, sin) * (RET_DK ** -0.5)
            per_dir.append(dict(q=q.astype(BF16), k=k.astype(BF16), kz=(k * zt_ref[d]).astype(BF16),
                                v=v_ref[sl, :].astype(BF16), sl=sl))
        groups = [(d, h) for d in range(2) for h in range(RET_H)]
        ksl = lambda h: slice(h * RET_DK, (h + 1) * RET_DK)
        vsl = lambda h: slice(h * RET_DV, (h + 1) * RET_DV)
        q_h = [per_dir[d]['q'][:, ksl(h)] for d, h in groups]
        v_h = [per_dir[d]['v'][:, vsl(h)] for d, h in groups]
        att = [lax.dot_general(q_h[i], per_dir[d]['k'][:, ksl(h)], _NT, preferred_element_type=F32) * dm_ref[d, h]
               for i, (d, h) in enumerate(groups)]
        s0 = [s_ref[d, h] for d, h in groups]
        o = [_bdot(att[i], v_h[i]) + _bdot(q_h[i], s0[i]) * xi_ref[d][:, vsl(h)]
             for i, (d, h) in enumerate(groups)]
        for i, (d, h) in enumerate(groups):
            kv = lax.dot_general(per_dir[d]['kz'][:, ksl(h)], v_h[i], _TN, preferred_element_type=F32)
            s_ref[d, h] = s0[i] * math.exp(_RET_LOG_G[h] * c_len) + kv
        for d in range(2):
            out_refs[d][per_dir[d]['sl'], :] = jnp.concatenate([o[d * RET_H + h] for h in range(RET_H)], axis=1)
        return carry

    lax.fori_loop(0, n_chunks, chunk, 0)


def _ret_mix(p_mid, seq, tc, n_tok):
    n = p_mid.shape[0]
    tb = _pick_tile(math.gcd(seq, tc), RET_TB, RET_CHUNK)
    nblk, cblk = seq // tb, tc // tb
    c_len = RET_CHUNK
    n_rows = n_tok // GRID_W
    rows = jnp.repeat(jnp.arange(n_rows, dtype=F32), GRID_W)
    cols = jnp.tile(jnp.arange(GRID_W, dtype=F32), n_rows)
    n_freq = RET_DK // 4
    inv = ROPE_BASE ** (-jnp.arange(n_freq, dtype=F32) / n_freq)
    ang = jnp.concatenate([rows[:, None] * inv, cols[:, None] * inv], axis=-1)
    cos_h = jnp.concatenate([jnp.cos(ang), jnp.cos(ang)], axis=-1)
    sin_h = jnp.concatenate([-jnp.sin(ang), jnp.sin(ang)], axis=-1)
    pad_ctx = lambda tbl, fill: jnp.concatenate(
        [jnp.full((tc, RET_H * RET_DK), fill, F32), jnp.tile(tbl, (1, RET_H))], axis=0)
    cos_t, sin_t = pad_ctx(cos_h, 1.0), pad_ctx(sin_h, 0.0)
    log_g = jnp.asarray(_RET_LOG_G, F32)
    idx = jnp.arange(c_len, dtype=F32)
    dm, xi, zt = [], [], []
    for d in range(2):
        pos = idx if d == 0 else c_len - 1.0 - idx
        diff = pos[:, None] - pos[None, :]
        keep = diff >= 0 if d == 0 else diff > 0
        dm.append(jnp.where(keep[None], jnp.exp(log_g[:, None, None] * jnp.maximum(diff, 0.0)[None]), 0.0))
        xi.append(jnp.repeat(jnp.exp(log_g[None, :] * (pos[:, None] + 1.0)), RET_DV, axis=1))
        zt.append(jnp.repeat(jnp.exp(log_g[None, :] * (c_len - 1.0 - pos)[:, None]), RET_DK, axis=1))
    consts = [jnp.stack(dm), jnp.stack(xi), jnp.stack(zt)]

    def fwd_blk(t):
        return t

    def bwd_blk(t):
        return jnp.where(t < cblk, cblk - 1 - t, nblk - 1 - (t - cblk))

    names = [name for name, _ in REST_COLS]
    offs = {name: sum(w for _, w in REST_COLS[:i]) for i, (name, _) in enumerate(REST_COLS)}
    qk_w = RET_H * RET_DK
    in_specs = []
    for blk_fn in (fwd_blk, bwd_blk):
        spec = lambda width, cb, blk_fn=blk_fn: pl.BlockSpec((tb, width), lambda b, t: (b * nblk + blk_fn(t), cb))
        tbl = lambda blk_fn=blk_fn: pl.BlockSpec((tb, qk_w), lambda b, t: (blk_fn(t), 0))
        assert offs['ret_q'] % qk_w == 0 and offs['ret_k'] % qk_w == 0 and offs['ret_v'] % BR_W == 0
        in_specs += [spec(qk_w, offs['ret_q'] // qk_w), spec(qk_w, offs['ret_k'] // qk_w),
                     spec(BR_W, offs['ret_v'] // BR_W), tbl(), tbl()]
    whole = lambda a: pl.BlockSpec(a.shape, lambda b, t: (0,) * a.ndim)
    in_specs += [whole(a) for a in consts]
    out_spec = lambda blk_fn: pl.BlockSpec((tb, BR_W), lambda b, t: (b * nblk + blk_fn(t), 0))
    return pl.pallas_call(
        functools.partial(_ret_body, n_chunks=tb // c_len),
        grid=(n // seq, nblk),
        in_specs=in_specs,
        out_specs=[out_spec(fwd_blk), out_spec(bwd_blk)],
        out_shape=[jax.ShapeDtypeStruct((n, BR_W), F32)] * 2,
        scratch_shapes=[pltpu.VMEM((2, RET_H, RET_DK, RET_DV), F32)],
        compiler_params=pltpu.CompilerParams(
            dimension_semantics=("parallel", "arbitrary"), vmem_limit_bytes=VMEM_LIMIT),
    )(*([p_mid, p_mid, p_mid, cos_t, sin_t] * 2), *consts)


def _ret_out_proj(o_f, o_b, p_mid, lp, w_proj, tm):
    def act(i, of, ob, g, gn_g, gn_b):
        o = of + ob
        g = g.astype(F32)
        normed = []
        for h in range(RET_H):
            oh = o[:, h * RET_DV:(h + 1) * RET_DV]
            oc = oh - jnp.mean(oh, axis=-1, keepdims=True)
            normed.append(oc * lax.rsqrt(jnp.mean(oc * oc, axis=-1, keepdims=True) + LN_EPS))
        yn = jnp.concatenate(normed, axis=1) * gn_g + gn_b
        return g * jax.nn.sigmoid(g) * yn

    off_g = sum(w for _, w in REST_COLS[:[name for name, _ in REST_COLS].index('ret_g')])
    assert off_g % BR_W == 0
    return _fused_mm([o_f, o_b, (p_mid, BR_W, off_g // BR_W)],
                     [_whole(lp['ret_gn_g'][None, :]), _whole(lp['ret_gn_b'][None, :])],
                     w_proj.astype(BF16), act, tm=tm, tn=w_proj.shape[1], out_dtype=BF16)


ROUTER_PAD = 128


def _norm2_router_body(z_ref, g_ref, shl_ref, scl_ref, shc_ref, scc_ref, whi_ref, wlo_ref, tok_ref, log_ref, *,
                       skip, cblk):
    xb = z_ref[...]
    y = xb * lax.rsqrt(jnp.mean(xb * xb, axis=-1, keepdims=True) + RMS_EPS) * g_ref[...]
    is_ctx = (pl.program_id(1) + skip) < cblk
    sc = jnp.where(is_ctx, scc_ref[...], scl_ref[0])
    sh = jnp.where(is_ctx, shc_ref[...], shl_ref[0])
    h = y * (1.0 + sc) + sh
    hi = h.astype(BF16)
    lo = (h - hi.astype(F32)).astype(BF16)
    tok_ref[...] = hi
    log_ref[...] = (jnp.dot(hi, whi_ref[...], preferred_element_type=F32)
                    + jnp.dot(lo, whi_ref[...], preferred_element_type=F32)
                    + jnp.dot(hi, wlo_ref[...], preferred_element_type=F32))


def _norm2_router(z, seq, tc, g, sh_l, sc_l, sh_c, sc_c, w_router, latent_only):
    n, d = z.shape
    bsz = n // seq
    tb = _pick_tile(math.gcd(seq, tc), 256, 16)
    nblk, cblk = seq // tb, tc // tb
    skip = cblk if latent_only else 0
    nb = nblk - skip
    w_pad = jnp.pad(w_router, ((0, 0), (0, ROUTER_PAD - w_router.shape[1])))
    w_hi = w_pad.astype(BF16)
    w_lo = (w_pad - w_hi.astype(F32)).astype(BF16)
    vec = pl.BlockSpec((1, d), lambda b, i: (0, 0))
    per_b = pl.BlockSpec((1, 1, d), lambda b, i: (b, 0, 0))
    wspec = pl.BlockSpec((d, ROUTER_PAD), lambda b, i: (0, 0))
    return pl.pallas_call(
        functools.partial(_norm2_router_body, skip=skip, cblk=cblk),
        grid=(bsz, nb),
        in_specs=[pl.BlockSpec((tb, d), lambda b, i: (b * nblk + skip + i, 0)), vec, per_b, per_b, vec, vec,
                  wspec, wspec],
        out_specs=[pl.BlockSpec((tb, d), lambda b, i: (b * nb + i, 0)),
                   pl.BlockSpec((tb, ROUTER_PAD), lambda b, i: (b * nb + i, 0))],
        out_shape=[jax.ShapeDtypeStruct((bsz * nb * tb, d), BF16),
                   jax.ShapeDtypeStruct((bsz * nb * tb, ROUTER_PAD), F32)],
        compiler_params=pltpu.CompilerParams(dimension_semantics=("parallel", "parallel"),
                                             vmem_limit_bytes=VMEM_LIMIT),
    )(z, g, sh_l, sc_l, sh_c, sc_c, w_hi, w_lo)


def _moe_body(be_ref, nu_ref, x_ref, w1_ref, w3_ref, w2_ref, o_ref, w1b, w3b, w2b):
    i = pl.program_id(0)

    @pl.when((i == 0) | (be_ref[i] != be_ref[jnp.maximum(i - 1, 0)]))
    def _():
        w1b[...] = w1_ref[0].astype(BF16)
        w3b[...] = w3_ref[0].astype(BF16)
        w2b[...] = w2_ref[0].astype(BF16)

    @pl.when(i < nu_ref[0])
    def _():
        xb = x_ref[...]
        h1 = jnp.dot(xb, w1b[...], preferred_element_type=F32)
        h3 = jnp.dot(xb, w3b[...], preferred_element_type=F32)
        hid = (h1 * jax.nn.sigmoid(h1) * h3).astype(BF16)
        o_ref[...] = jnp.dot(hid, w2b[...], preferred_element_type=F32).astype(o_ref.dtype)

    @pl.when(i >= nu_ref[0])
    def _():
        o_ref[...] = jnp.zeros_like(o_ref)


def _moe_experts(xb, block_e, n_used, w1, w3, w2, layer):
    n_rows, d = xb.shape
    n_blocks = n_rows // MOE_BM
    de = w1.shape[3]
    grid_spec = pltpu.PrefetchScalarGridSpec(
        num_scalar_prefetch=2,
        grid=(n_blocks,),
        in_specs=[
            pl.BlockSpec((MOE_BM, d), lambda i, be, nu: (i, 0)),
            pl.BlockSpec((None, 1, d, de), lambda i, be, nu: (layer, be[i], 0, 0)),
            pl.BlockSpec((None, 1, d, de), lambda i, be, nu: (layer, be[i], 0, 0)),
            pl.BlockSpec((None, 1, de, d), lambda i, be, nu: (layer, be[i], 0, 0)),
        ],
        out_specs=pl.BlockSpec((MOE_BM, d), lambda i, be, nu: (i, 0)),
        scratch_shapes=[pltpu.VMEM((d, de), BF16), pltpu.VMEM((d, de), BF16), pltpu.VMEM((de, d), BF16)],
    )
    return pl.pallas_call(
        _moe_body,
        grid_spec=grid_spec,
        out_shape=jax.ShapeDtypeStruct((n_rows, d), BF16),
        compiler_params=pltpu.CompilerParams(dimension_semantics=("arbitrary",),
                                             vmem_limit_bytes=VMEM_LIMIT),
    )(block_e, n_used, xb, w1, w3, w2)


def _moe(tokens, logits, b_router, w1, w3, w2, layer):
    n_tok = tokens.shape[0]
    aff = jax.nn.sigmoid(logits[:, :N_EXPERTS])
    assert TOP_K == 2
    sel = (aff + b_router).reshape(-1, N_GROUPS, EXPERTS_PER_GROUP)

    def top2(v):
        lane = jnp.arange(v.shape[-1], dtype=jnp.int32)
        i1 = jnp.argmax(v, axis=-1).astype(jnp.int32)
        rest = jnp.where(lane == i1[..., None], -jnp.inf, v)
        i2 = jnp.argmax(rest, axis=-1).astype(jnp.int32)
        return jnp.max(v, axis=-1), jnp.max(rest, axis=-1), i1, i2

    g1, g2, _, _ = top2(sel)
    grp = jnp.argmax(g1 + g2, axis=-1)
    grp_mask = jnp.arange(N_GROUPS)[None, :] == grp[:, None]
    masked = jnp.where(grp_mask[:, :, None], sel, -jnp.inf).reshape(-1, N_EXPERTS)
    _, _, e1, e2 = top2(masked)
    e_idx = jnp.stack([e1, e2], axis=-1)
    wts = jnp.take_along_axis(aff, e_idx, axis=1)
    wts = wts / jnp.sum(wts, axis=-1, keepdims=True)

    n_asg = n_tok * TOP_K
    flat_e = e_idx.reshape(-1).astype(jnp.int32)
    order = jnp.argsort(flat_e).astype(jnp.int32)
    se, st = flat_e[order], order // TOP_K
    counts = jnp.sum(flat_e[:, None] == jnp.arange(N_EXPERTS, dtype=jnp.int32)[None, :], axis=0, dtype=jnp.int32)
    starts = jnp.cumsum(counts) - counts
    padded = (counts + MOE_BM - 1) // MOE_BM * MOE_BM
    pad_end = jnp.cumsum(padded)
    pad_start = pad_end - padded
    dest = pad_start[se] + jnp.arange(n_asg, dtype=jnp.int32) - starts[se]
    dest_asg = dest[jnp.argsort(order)].reshape(n_tok, TOP_K)
    n_blocks = -(-n_asg // MOE_BM) + N_EXPERTS
    block_start = jnp.arange(n_blocks, dtype=jnp.int32) * MOE_BM
    block_e = jnp.minimum(jnp.sum(pad_end[None, :] <= block_start[:, None], axis=1, dtype=jnp.int32),
                          N_EXPERTS - 1)
    n_used = (pad_end[-1] // MOE_BM).astype(jnp.int32).reshape(1)
    slot_e = jnp.repeat(block_e, MOE_BM)
    slot = jnp.arange(n_blocks * MOE_BM, dtype=jnp.int32)
    rank = slot - pad_start[slot_e]
    slot_tok = jnp.where(rank < counts[slot_e], st[jnp.clip(starts[slot_e] + rank, 0, n_asg - 1)], slot % n_tok)
    xb = tokens[slot_tok]
    yb = _moe_experts(xb, block_e, n_used, w1, w3, w2, layer)
    return (wts[:, 0:1] * yb[dest_asg[:, 0]].astype(F32) + wts[:, 1:2] * yb[dest_asg[:, 1]].astype(F32))


def _merge_out(z, merge, branches, b_merge, w_out_l, g1, cg1, seq, tc, d):
    tm2 = _pick_tile(seq, 576, 16)
    bpb = seq // tm2

    def seg_select(i, ctx_val, lat_val):
        row = (i % bpb) * tm2 + lax.broadcasted_iota(jnp.int32, (tm2, 1), 0)
        return jnp.where(row < tc, ctx_val, lat_val)

    def merge_pro(i, mg, b0, b1, b2, b3, bm):
        gates = jax.nn.sigmoid(mg.astype(F32) + bm)
        b0, b1, b2, b3 = [b.astype(F32) for b in (b0, b1, b2, b3)]
        return (gates[:, 0 * d:1 * d] * b0 + gates[:, 1 * d:2 * d] * b1
                + gates[:, 2 * d:3 * d] * b2 + gates[:, 3 * d:4 * d] * b3)

    def resid_epi(i, acc, zb, gl, gc):
        return zb + seg_select(i, gc, gl[0]) * acc

    tn = d // 2
    return _fused_mm([merge] + branches, [_whole(b_merge[None, :])],
                     w_out_l.astype(BF16), merge_pro, tm=tm2, tn=tn,
                     epilogue=resid_epi, epi_rows=[z],
                     epi_aux=[(g1, (1, 1, tn), lambda i, j: (i // bpb, 0, j)),
                              (cg1, (1, tn), lambda i, j: (0, j))])


def kernel(x, c, ctx, c_ctx, w_ada, b_ada, norm1_g, norm2_g, w_in, b_merge, rwkv_mu_prev, rwkv_mu_next, rwkv_w0, rwkv_w2, rwkv_a0, rwkv_a2, rwkv_g2, rwkv_k_k, rwkv_k_a, rwkv_r_k, rwkv_gn_g, rwkv_gn_b, rwkv_proj, conv_w, conv_b, conv_ln_g, conv_ln_b, conv_proj, lru_conv_w, lru_conv_b, lru_wr, lru_br, lru_wi, lru_bi, lru_lambda, lru_proj, ret_gn_g, ret_gn_b, ret_proj, w_out, w_router, b_router, e_w1, e_w3, e_w2, final_g):
    bsz, n_tok, d = x.shape
    tc = ctx.shape[1]
    seq = tc + n_tok
    depth = w_in.shape[0]
    per_layer = {
        'rwkv_w0': rwkv_w0, 'rwkv_w2': rwkv_w2, 'rwkv_a0': rwkv_a0, 'rwkv_a2': rwkv_a2, 'rwkv_g2': rwkv_g2,
        'rwkv_k_k': rwkv_k_k, 'rwkv_k_a': rwkv_k_a, 'rwkv_r_k': rwkv_r_k,
        'rwkv_gn_g': rwkv_gn_g, 'rwkv_gn_b': rwkv_gn_b,
        'conv_w': conv_w, 'conv_b': conv_b, 'conv_ln_g': conv_ln_g, 'conv_ln_b': conv_ln_b,
        'lru_conv_w': lru_conv_w, 'lru_conv_b': lru_conv_b, 'lru_wr': lru_wr, 'lru_br': lru_br,
        'lru_wi': lru_wi, 'lru_bi': lru_bi, 'lru_lambda': lru_lambda,
        'ret_gn_g': ret_gn_g, 'ret_gn_b': ret_gn_b,
    }
    tm = _pick_tile(seq, 576)

    z = jnp.concatenate([ctx, x], axis=1).reshape(bsz * seq, d)
    for layer in range(depth):
        lp = {name: arr[layer] for name, arr in per_layer.items()}
        last = layer == depth - 1
        cc = jnp.concatenate([c, c_ctx[None, :]], axis=0)
        n_mod = -(-cc.shape[0] // 8) * 8
        cc = jnp.pad(cc, ((0, n_mod - cc.shape[0]), (0, 0)))
        mod = _fused_mm([cc], [], w_ada[layer].astype(BF16), lambda i, cb: cb * jax.nn.sigmoid(cb),
                        tm=n_mod, tn=6 * d // 4,
                        epilogue=lambda i, acc, bb: acc + bb,
                        epi_aux=[(b_ada[layer][None, :], (1, 6 * d // 4), lambda i, j: (0, j))])
        mod_l = mod[:bsz].reshape(bsz, 1, 6 * d)
        mod_c = mod[bsz:bsz + 1]
        sh1, sc1, g1, sh2, sc2, g2 = [mod_l[:, :, n * d:(n + 1) * d] for n in range(6)]
        csh1, csc1, cg1, csh2, csc2, cg2 = [mod_c[:, n * d:(n + 1) * d] for n in range(6)]

        w_l = w_in[layer].astype(BF16)
        norm_args = (norm1_g[layer][None, :], sh1, sc1, csh1, csc1)
        p_rwkv = _in_proj(z, seq, tc, *norm_args, w_l[:, :RWKV_WIDTH], 640,
                          rwkv_mu_prev[layer][None, :], rwkv_mu_next[layer][None, :], out_dtype=BF16)
        p_mid = _in_proj(z, seq, tc, *norm_args, w_l[:, RWKV_WIDTH:RWKV_WIDTH + MID_WIDTH], 896, out_dtype=BF16)
        p_merge = _in_proj(z, seq, tc, *norm_args, w_l[:, RWKV_WIDTH + MID_WIDTH:], 1024, out_dtype=BF16)

        y_f, y_b = _rwkv_mix(p_rwkv, seq, tc, lp)
        o_f, o_b = _ret_mix(p_mid, seq, tc, n_tok)
        branches = [
            _rwkv_out_proj(y_f, y_b, p_rwkv, lp, rwkv_proj[layer], tm),
            _plain_mm(_conv_mix(p_mid, seq, tc, lp), conv_proj[layer].astype(BF16), tm=tm, tn=d, out_dtype=BF16),
            _plain_mm(_lru_mix(p_mid, seq, tc, lp), lru_proj[layer].astype(BF16), tm=tm, tn=d, out_dtype=BF16),
            _ret_out_proj(o_f, o_b, p_mid, lp, ret_proj[layer], tm),
        ]
        z = _merge_out(z, p_merge, branches, b_merge[layer], w_out[layer], g1, cg1, seq, tc, d)

        z3 = z.reshape(bsz, seq, d)
        tok, logits = _norm2_router(z, seq, tc, norm2_g[layer][None, :], sh2, sc2, csh2, csc2, w_router,
                                    latent_only=last)
        y = _moe(tok, logits, b_router, e_w1, e_w3, e_w2, layer)
        if not last:
            is_ctx = (jnp.arange(seq) < tc)[None, :, None]
            gate2 = jnp.where(is_ctx, cg2[None], g2)
            z = (z3 + gate2 * y.reshape(bsz, seq, d)).reshape(bsz * seq, d)
        else:
            xl = z3[:, tc:] + g2 * y.reshape(bsz, n_tok, d)
            return xl * lax.rsqrt(jnp.mean(xl * xl, axis=-1, keepdims=True) + RMS_EPS) * final_g
    return None
```
